```python
import jax, jax.numpy as jnp
from jax import lax
import numpy as np

D_MODEL = 1024
BATCH = 8
SEQ = 2048
DEPTH = 1

MEM_LEN = 256
HD = 64
SB_HEADS = 8
FOX_HEADS = 8
MEM_HEADS = 4
MEM_HD = 128
D_SB = SB_HEADS * HD
D_FOX = FOX_HEADS * HD
D_MEM = MEM_HEADS * MEM_HD
N_BRANCH = 3
D_FF = 4 * D_MODEL
BLOCK_Q = 128
EPS = 1e-6
NEG_INF = -1e30
SPLITS = (D_SB, D_SB, D_SB, D_FOX, D_FOX, D_FOX, FOX_HEADS, D_MEM, N_BRANCH * D_MODEL)
D_IN = sum(SPLITS)

kernel_name = "hybrid_stickbreak_fox_memory_block"


def split_columns(t, sizes):
    pieces = []
    start = 0
    for n in sizes:
        pieces.append(t[..., start:start + n])
        start += n
    return pieces


def rmsnorm(x, g):
    xf = x.astype(jnp.float32)
    y = xf * lax.rsqrt(jnp.mean(xf * xf, axis=-1, keepdims=True) + EPS)
    return (y * g.astype(jnp.float32)).astype(x.dtype)


def to_heads(t, n, d):
    b, s, _ = t.shape
    return t.reshape(b, s, n, d).transpose(0, 2, 1, 3)


def from_heads(t):
    b, h, s, d = t.shape
    return t.transpose(0, 2, 1, 3).reshape(b, s, h * d)


def stick_breaking_attention(q, k, v):
    s_len = q.shape[2]
    scale = HD ** -0.5
    outs = []
    for i in range(s_len // BLOCK_Q):
        q0 = i * BLOCK_Q
        kend = q0 + BLOCK_Q
        z = jnp.einsum('bhtd,bhsd->bhts', q[:, :, q0:kend], k[:, :, :kend]).astype(jnp.float32) * scale
        t_idx = q0 + jnp.arange(BLOCK_Q)[:, None]
        s_idx = jnp.arange(kend)[None, :]
        strict = s_idx < t_idx
        log_rem = jnp.where(strict, jax.nn.log_sigmoid(-z), 0.0)
        after = lax.cumsum(log_rem, axis=3, reverse=True) - log_rem
        w = jnp.where(strict, jnp.exp(jax.nn.log_sigmoid(z) + after), 0.0)
        outs.append(jnp.einsum('bhts,bhsd->bhtd', w.astype(v.dtype), v[:, :, :kend]))
    return jnp.concatenate(outs, axis=2)


def forgetting_attention(q, k, v, log_f):
    s_len = q.shape[2]
    scale = HD ** -0.5
    F = lax.cumsum(log_f.astype(jnp.float32), axis=2)
    outs = []
    for i in range(s_len // BLOCK_Q):
        q0 = i * BLOCK_Q
        kend = q0 + BLOCK_Q
        z = jnp.einsum('bhtd,bhsd->bhts', q[:, :, q0:kend], k[:, :, :kend]).astype(jnp.float32) * scale
        z = z + F[:, :, q0:kend, None] - F[:, :, None, :kend]
        causal = jnp.arange(kend)[None, :] <= (q0 + jnp.arange(BLOCK_Q)[:, None])
        p = jax.nn.softmax(jnp.where(causal, z, NEG_INF), axis=-1)
        outs.append(jnp.einsum('bhts,bhsd->bhtd', p.astype(v.dtype), v[:, :, :kend]))
    return jnp.concatenate(outs, axis=2)


def memory_attention(q, k, v):
    z = jnp.einsum('bhtd,bhmd->bhtm', q, k).astype(jnp.float32) * (MEM_HD ** -0.5)
    p = jax.nn.softmax(z, axis=-1)
    return jnp.einsum('bhtm,bhmd->bhtd', p.astype(v.dtype), v)


def setup_inputs(seed: int = 0) -> dict:
    key = jax.random.key(seed)
    ks = jax.random.split(key, 20)

    def w(k, shape, fan_in):
        return jax.random.normal(k, shape, jnp.float32) * fan_in ** -0.5

    def gain(k, shape):
        return 1.0 + 0.02 * jax.random.normal(k, shape, jnp.float32)

    L = DEPTH
    return {
        "x": jax.random.normal(ks[0], (BATCH, SEQ, D_MODEL), jnp.float32),
        "mem": jax.random.normal(ks[1], (BATCH, MEM_LEN, D_MODEL), jnp.float32),
        "g_mix_norm": gain(ks[2], (L, D_MODEL)),
        "g_mem_norm": gain(ks[3], (L, D_MODEL)),
        "w_in": w(ks[4], (L, D_MODEL, D_IN), D_MODEL),
        "b_forget": 3.0 + 0.5 * jax.random.normal(ks[5], (L, FOX_HEADS), jnp.float32),
        "g_fox_q": gain(ks[6], (L, HD)),
        "g_fox_k": gain(ks[7], (L, HD)),
        "g_mem_q": gain(ks[8], (L, MEM_HD)),
        "g_mem_k": gain(ks[9], (L, MEM_HD)),
        "w_mem_kv": w(ks[10], (L, D_MODEL, 2 * D_MEM), D_MODEL),
        "w_branch_sb": w(ks[11], (L, D_SB, D_MODEL), D_SB),
        "w_branch_fox": w(ks[12], (L, D_FOX, D_MODEL), D_FOX),
        "w_branch_mem": w(ks[13], (L, D_MEM, D_MODEL), D_MEM),
        "w_out": w(ks[14], (L, D_MODEL, D_MODEL), D_MODEL),
        "g_mlp_norm": gain(ks[15], (L, D_MODEL)),
        "w_ff_up": w(ks[16], (L, D_MODEL, D_FF), D_MODEL),
        "w_ff_down": w(ks[17], (L, D_FF, D_MODEL), D_FF),
    }


def reference(x, mem, g_mix_norm, g_mem_norm, w_in, b_forget, g_fox_q, g_fox_k, g_mem_q, g_mem_k,
              w_mem_kv, w_branch_sb, w_branch_fox, w_branch_mem, w_out, g_mlp_norm, w_ff_up, w_ff_down):
    b, s, _ = x.shape
    for l in range(DEPTH):
        h = rmsnorm(x, g_mix_norm[l])
        proj = jnp.einsum('bsd,de->bse', h, w_in[l])
        sb_q, sb_k, sb_v, fx_q, fx_k, fx_v, f_logit, m_q, gate_logit = split_columns(proj, SPLITS)

        o_sb = stick_breaking_attention(to_heads(sb_q, SB_HEADS, HD), to_heads(sb_k, SB_HEADS, HD),
                                        to_heads(sb_v, SB_HEADS, HD))

        fq = rmsnorm(to_heads(fx_q, FOX_HEADS, HD), g_fox_q[l])
        fk = rmsnorm(to_heads(fx_k, FOX_HEADS, HD), g_fox_k[l])
        log_f = jax.nn.log_sigmoid((f_logit + b_forget[l]).astype(jnp.float32)).transpose(0, 2, 1)
        o_fox = forgetting_attention(fq, fk, to_heads(fx_v, FOX_HEADS, HD), log_f)

        mh = rmsnorm(mem, g_mem_norm[l])
        mkv = jnp.einsum('bmd,de->bme', mh, w_mem_kv[l])
        mk, mv = split_columns(mkv, (D_MEM, D_MEM))
        mq = rmsnorm(to_heads(m_q, MEM_HEADS, MEM_HD), g_mem_q[l])
        mk = rmsnorm(to_heads(mk, MEM_HEADS, MEM_HD), g_mem_k[l])
        o_mem = memory_attention(mq, mk, to_heads(mv, MEM_HEADS, MEM_HD))

        gates = jax.nn.sigmoid(gate_logit.reshape(b, s, N_BRANCH, D_MODEL))
        br_sb = jnp.einsum('bse,ed->bsd', from_heads(o_sb), w_branch_sb[l])
        br_fox = jnp.einsum('bse,ed->bsd', from_heads(o_fox), w_branch_fox[l])
        br_mem = jnp.einsum('bse,ed->bsd', from_heads(o_mem), w_branch_mem[l])
        merged = gates[:, :, 0] * br_sb + gates[:, :, 1] * br_fox + gates[:, :, 2] * br_mem
        x = x + jnp.einsum('bsd,de->bse', merged, w_out[l])

        h2 = rmsnorm(x, g_mlp_norm[l])
        u = jax.nn.relu(jnp.einsum('bsd,df->bsf', h2, w_ff_up[l]))
        x = x + jnp.einsum('bsf,fd->bsd', u * u, w_ff_down[l])
    return x
```

```python
import functools

import numpy as np
import jax
import jax.numpy as jnp
from jax import lax
from jax.experimental import pallas as pl
from jax.experimental.pallas import tpu as pltpu

D_MODEL = 1024
HD = 64
SB_HEADS = 8
FOX_HEADS = 8
MEM_HEADS = 4
MEM_HD = 128
D_SB = SB_HEADS * HD
D_FOX = FOX_HEADS * HD
D_MEM = MEM_HEADS * MEM_HD
N_BRANCH = 3
D_FF = 4 * D_MODEL
EPS = 1e-6
NEG_INF = -1e30

LANES = 128
AUG_STRIDE = 8
ROW_TILE = 512
ATT_TILE = 256
MEM_Q_TILE = 512
FF_CHUNK = 512
VMEM_LIMIT = 56 * 1024 * 1024

F32 = jnp.float32
BF16 = jnp.bfloat16
_NT = (((1,), (1,)), ((), ()))


def _const_spec(shape):
    nd = len(shape)
    return pl.BlockSpec(shape, lambda *_: (0,) * nd, pipeline_mode=pl.Buffered(1))


def _params(n_axes):
    return pltpu.CompilerParams(
        dimension_semantics=("arbitrary",) * n_axes, vmem_limit_bytes=VMEM_LIMIT)


def _rms(x, g):
    ms = jnp.sum(x * x, axis=-1, keepdims=True) * (1.0 / x.shape[-1])
    return (x * lax.rsqrt(ms + EPS)) * g


def _headnorm(p, g, hd):
    rows, n = p.shape
    lane = lax.broadcasted_iota(jnp.int32, (rows, LANES), 1)
    outs = []
    for j in range(n // LANES):
        c = p[:, j * LANES:(j + 1) * LANES]
        c2 = c * c
        if hd == LANES:
            ms = jnp.sum(c2, axis=1, keepdims=True) * (1.0 / LANES)
        else:
            lo = jnp.sum(jnp.where(lane < hd, c2, 0.0), axis=1, keepdims=True)
            hi = jnp.sum(jnp.where(lane >= hd, c2, 0.0), axis=1, keepdims=True)
            ms = jnp.where(lane < hd, lo, hi) * (1.0 / hd)
        outs.append(c * lax.rsqrt(ms + EPS))
    return jnp.concatenate(outs, axis=1) * g


def _log_sigmoid(x):
    return jnp.minimum(x, 0.0) - jnp.log1p(jnp.exp(-jnp.abs(x)))


def _proj_kernel(x_ref, g_ref, w_ref, bf_ref, gq_ref, gk_ref, gm_ref,
                 sbq_ref, sbk_ref, sbv_ref, fxq_ref, fxk_ref, fxv_ref, mq_ref, lf_ref):
    h = _rms(x_ref[...], g_ref[...]).astype(BF16)

    def mm(c0, n):
        return jnp.dot(h, w_ref[:, c0:c0 + n], preferred_element_type=F32)

    sbq_ref[...] = (mm(0, D_SB) * HD ** -0.5).astype(BF16)
    sbk_ref[...] = mm(D_SB, D_SB).astype(BF16)
    sbv_ref[...] = mm(2 * D_SB, D_SB).astype(BF16)
    c0 = 3 * D_SB
    fxq_ref[...] = (_headnorm(mm(c0, D_FOX), gq_ref[...], HD) * HD ** -0.5).astype(BF16)
    fxk_ref[...] = _headnorm(mm(c0 + D_FOX, D_FOX), gk_ref[...], HD).astype(BF16)
    fxv_ref[...] = mm(c0 + 2 * D_FOX, D_FOX).astype(BF16)
    c0 += 3 * D_FOX
    mq_ref[...] = _headnorm(mm(c0, D_MEM), gm_ref[...], MEM_HD).astype(BF16)
    lf_ref[...] = _log_sigmoid(mm(c0 + D_MEM, LANES) + bf_ref[...])


def _proj(x2, g, w, bf, gq, gk, gm):
    t = x2.shape[0]
    n_w = w.shape[1]
    row = lambda n: pl.BlockSpec((ROW_TILE, n), lambda i: (i, 0))
    out_shapes = [jax.ShapeDtypeStruct((t, D_SB), BF16)] * 7 + [jax.ShapeDtypeStruct((t, LANES), F32)]
    return pl.pallas_call(
        _proj_kernel,
        grid=(t // ROW_TILE,),
        in_specs=[row(D_MODEL), _const_spec((1, D_MODEL)), _const_spec((D_MODEL, n_w)),
                  _const_spec((1, LANES)), _const_spec((1, D_FOX)), _const_spec((1, D_FOX)),
                  _const_spec((1, D_MEM))],
        out_specs=[row(D_SB)] * 7 + [row(LANES)],
        out_shape=out_shapes,
        compiler_params=_params(1),
        name="proj",
    )(x2, g, w, bf, gq, gk, gm)


def _forget_kernel(lf_ref, p_ref, oq_ref, ok_ref, aq_ref, ak_ref):
    f = lf_ref[0]
    s_len = f.shape[0]
    row = lax.broadcasted_iota(jnp.int32, f.shape, 0)
    sh = 1
    while sh < s_len:
        f = f + jnp.where(row >= sh, pltpu.roll(f, sh, 0), 0.0)
        sh *= 2
    f_hi = f.astype(BF16)
    r1 = f - f_hi.astype(F32)
    f_mid = r1.astype(BF16)
    f_lo = (r1 - f_mid.astype(F32)).astype(BF16)
    pieces = jnp.concatenate([f_hi, f_mid, f_lo], axis=1)
    placed = jnp.dot(pieces, p_ref[...], preferred_element_type=F32)
    aq_ref[0] = (placed[:, :LANES] + oq_ref[...]).astype(BF16)
    ak_ref[0] = (placed[:, LANES:] + ok_ref[...]).astype(BF16)


def _forget_constants():
    p = np.zeros((3 * LANES, 2 * LANES), np.float32)
    oq = np.zeros((1, LANES), np.float32)
    ok = np.zeros((1, LANES), np.float32)
    for h in range(FOX_HEADS):
        for j in range(3):
            p[j * LANES + h, h * AUG_STRIDE + j] = 1.0
            p[j * LANES + h, LANES + h * AUG_STRIDE + 3 + j] = -1.0
            oq[0, h * AUG_STRIDE + 3 + j] = 1.0
            ok[0, h * AUG_STRIDE + j] = 1.0
    return jnp.asarray(p, BF16), jnp.asarray(oq), jnp.asarray(ok)


def _forget(lf3):
    b, s, _ = lf3.shape
    p, oq, ok = _forget_constants()
    blk = pl.BlockSpec((1, s, LANES), lambda i: (i, 0, 0))
    return pl.pallas_call(
        _forget_kernel,
        grid=(b,),
        in_specs=[blk, _const_spec(p.shape), _const_spec(oq.shape), _const_spec(ok.shape)],
        out_specs=[blk, blk],
        out_shape=[jax.ShapeDtypeStruct((b, s, LANES), BF16)] * 2,
        compiler_params=_params(1),
        name="forget",
    )(lf3, p, oq, ok)


def _sb_kernel(q_ref, k_ref, v_ref, m_ref, o_ref, acc_ref, carry_ref):
    tq = q_ref.shape[1]
    qi = pl.program_id(2)
    q = q_ref[0]
    lane = lax.broadcasted_iota(jnp.int32, (tq, LANES), 1)
    zero = jnp.zeros_like(q)
    qh = (jnp.where(lane < HD, q, zero), jnp.where(lane >= HD, q, zero))
    strict = (lax.broadcasted_iota(jnp.int32, (tq, tq), 1)
              < lax.broadcasted_iota(jnp.int32, (tq, tq), 0))
    acc_ref[...] = jnp.zeros_like(acc_ref)
    carry_ref[...] = jnp.zeros_like(carry_ref)

    def tile(kb, diag):
        start = pl.multiple_of(kb * tq, tq)
        k = k_ref[0, pl.ds(start, tq), :]
        v = v_ref[0, pl.ds(start, tq), :]
        msum = m_ref[...]
        for h in range(2):
            z = lax.dot_general(qh[h], k, _NT, preferred_element_type=F32)
            sp = jnp.maximum(z, 0.0) + jnp.log1p(jnp.exp(-jnp.abs(z)))
            if diag:
                sp = jnp.where(strict, sp, 0.0)
            hi = sp.astype(BF16)
            lo = (sp - hi.astype(F32)).astype(BF16)
            c = (jnp.dot(hi, msum, preferred_element_type=F32)
                 + jnp.dot(lo, msum, preferred_element_type=F32))
            carry = carry_ref[h]
            w = jnp.exp(z - c - jnp.concatenate([carry] * (tq // LANES), axis=1))
            if diag:
                w = jnp.where(strict, w, 0.0)
            acc_ref[h] += jnp.dot(w.astype(BF16), v, preferred_element_type=F32)
            carry_ref[h] = carry + jnp.broadcast_to(c[:, :1], (tq, LANES))

    tile(qi, True)

    def body(j, _):
        tile(qi - 1 - j, False)
        return 0

    lax.fori_loop(0, qi, body, 0)
    o_ref[0] = jnp.where(lane < HD, acc_ref[0], acc_ref[1]).astype(BF16)


def _sb_attention(q, k, v):
    b, s, d = q.shape
    t = ATT_TILE
    ii = np.arange(t)
    msum = jnp.asarray(ii[:, None] >= ii[None, :], BF16)
    qspec = pl.BlockSpec((1, t, LANES), lambda bi, hp, qi: (bi, qi, hp))
    kvspec = pl.BlockSpec((1, s, LANES), lambda bi, hp, qi: (bi, 0, hp))
    return pl.pallas_call(
        _sb_kernel,
        grid=(b, d // LANES, s // t),
        in_specs=[qspec, kvspec, kvspec, _const_spec((t, t))],
        out_specs=qspec,
        out_shape=jax.ShapeDtypeStruct((b, s, d), BF16),
        scratch_shapes=[pltpu.VMEM((2, t, LANES), F32), pltpu.VMEM((2, t, LANES), F32)],
        compiler_params=_params(3),
        name="sb_attn",
    )(q, k, v, msum)


def _fox_kernel(q_ref, aq_ref, k_ref, ak_ref, v_ref, o_ref, acc_ref, m_ref, l_ref):
    tq = q_ref.shape[1]
    hp = pl.program_id(1)
    qi = pl.program_id(2)
    q = q_ref[0]
    aq = aq_ref[0]
    lane = lax.broadcasted_iota(jnp.int32, (tq, LANES), 1)
    zero = jnp.zeros_like(q)
    qext = []
    for h in range(2):
        qm = jnp.where((lane < HD) if h == 0 else (lane >= HD), q, zero)
        am = jnp.where(lane // AUG_STRIDE == 2 * hp + h, aq, zero)
        qext.append(jnp.concatenate([qm, am], axis=1))
    causal = (lax.broadcasted_iota(jnp.int32, (tq, tq), 1)
              <= lax.broadcasted_iota(jnp.int32, (tq, tq), 0))
    acc_ref[...] = jnp.zeros_like(acc_ref)
    l_ref[...] = jnp.zeros_like(l_ref)
    m_ref[...] = jnp.full(m_ref.shape, NEG_INF, F32)
    reps = tq // LANES

    def tile(kb, diag):
        start = pl.multiple_of(kb * tq, tq)
        kext = jnp.concatenate([k_ref[0, pl.ds(start, tq), :], ak_ref[0, pl.ds(start, tq), :]], axis=1)
        v = v_ref[0, pl.ds(start, tq), :]
        for h in range(2):
            z = lax.dot_general(qext[h], kext, _NT, preferred_element_type=F32)
            if diag:
                z = jnp.where(causal, z, NEG_INF)
            m_prev = m_ref[h]
            m_new = jnp.maximum(m_prev, jnp.max(z, axis=1, keepdims=True))
            alpha = jnp.exp(m_prev - m_new)
            p = jnp.exp(z - jnp.concatenate([m_new] * reps, axis=1))
            l_ref[h] = alpha * l_ref[h] + jnp.sum(p, axis=1, keepdims=True)
            acc_ref[h] = alpha * acc_ref[h] + jnp.dot(p.astype(BF16), v, preferred_element_type=F32)
            m_ref[h] = m_new

    tile(qi, True)

    def body(j, _):
        tile(qi - 1 - j, False)
        return 0

    lax.fori_loop(0, qi, body, 0)
    o_ref[0] = jnp.where(lane < HD, acc_ref[0] / l_ref[0], acc_ref[1] / l_ref[1]).astype(BF16)


def _fox_attention(q, k, v, aq, ak):
    b, s, d = q.shape
    t = ATT_TILE
    qspec = pl.BlockSpec((1, t, LANES), lambda bi, hp, qi: (bi, qi, hp))
    aqspec = pl.BlockSpec((1, t, LANES), lambda bi, hp, qi: (bi, qi, 0))
    kvspec = pl.BlockSpec((1, s, LANES), lambda bi, hp, qi: (bi, 0, hp))
    akspec = pl.BlockSpec((1, s, LANES), lambda bi, hp, qi: (bi, 0, 0))
    return pl.pallas_call(
        _fox_kernel,
        grid=(b, d // LANES, s // t),
        in_specs=[qspec, aqspec, kvspec, akspec, kvspec],
        out_specs=qspec,
        out_shape=jax.ShapeDtypeStruct((b, s, d), BF16),
        scratch_shapes=[pltpu.VMEM((2, t, LANES), F32)] * 3,
        compiler_params=_params(3),
        name="fox_attn",
    )(q, aq, k, ak, v)


def _mem_kv_kernel(mem_ref, g_ref, w_ref, gk_ref, mk_ref, mv_ref):
    mh = _rms(mem_ref[0], g_ref[...]).astype(BF16)
    kv = jnp.dot(mh, w_ref[...], preferred_element_type=F32)
    mk_ref[0] = _headnorm(kv[:, :D_MEM], gk_ref[...], MEM_HD).astype(BF16)
    mv_ref[0] = kv[:, D_MEM:].astype(BF16)


def _mem_kv(mem, g, w, gk):
    b, m, _ = mem.shape
    out = pl.BlockSpec((1, m, D_MEM), lambda i: (i, 0, 0))
    return pl.pallas_call(
        _mem_kv_kernel,
        grid=(b,),
        in_specs=[pl.BlockSpec((1, m, D_MODEL), lambda i: (i, 0, 0)), _const_spec((1, D_MODEL)),
                  _const_spec((D_MODEL, 2 * D_MEM)), _const_spec((1, D_MEM))],
        out_specs=[out, out],
        out_shape=[jax.ShapeDtypeStruct((b, m, D_MEM), BF16)] * 2,
        compiler_params=_params(1),
        name="mem_kv",
    )(mem, g, w, gk)


def _mem_attn_kernel(q_ref, k_ref, v_ref, o_ref):
    for h in range(MEM_HEADS):
        sl = slice(h * MEM_HD, (h + 1) * MEM_HD)
        z = lax.dot_general(q_ref[0, :, sl], k_ref[0, :, sl], _NT,
                            preferred_element_type=F32) * MEM_HD ** -0.5
        e = jnp.exp(z - jnp.max(z, axis=1, keepdims=True))
        p = e / jnp.sum(e, axis=1, keepdims=True)
        o_ref[0, :, sl] = jnp.dot(p.astype(BF16), v_ref[0, :, sl],
                                  preferred_element_type=F32).astype(BF16)


def _mem_attention(q, k, v):
    b, s, d = q.shape
    m = k.shape[1]
    qspec = pl.BlockSpec((1, MEM_Q_TILE, d), lambda bi, qi: (bi, qi, 0))
    kvspec = pl.BlockSpec((1, m, d), lambda bi, qi: (bi, 0, 0))
    return pl.pallas_call(
        _mem_attn_kernel,
        grid=(b, s // MEM_Q_TILE),
        in_specs=[qspec, kvspec, kvspec],
        out_specs=qspec,
        out_shape=jax.ShapeDtypeStruct((b, s, d), BF16),
        compiler_params=_params(2),
        name="mem_attn",
    )(q, k, v)


def _merge_kernel(x_ref, g_ref, wg_ref, osb_ref, ofx_ref, omem_ref,
                  wsb_ref, wfx_ref, wmem_ref, wout_ref, o_ref):
    x = x_ref[...]
    h = _rms(x, g_ref[...]).astype(BF16)
    merged = None
    for i, (o_br, w_br) in enumerate(((osb_ref, wsb_ref), (ofx_ref, wfx_ref), (omem_ref, wmem_ref))):
        gate = jax.nn.sigmoid(jnp.dot(h, wg_ref[:, i * D_MODEL:(i + 1) * D_MODEL],
                                      preferred_element_type=F32))
        term = gate * jnp.dot(o_br[...], w_br[...], preferred_element_type=F32)
        merged = term if merged is None else merged + term
    o_ref[...] = x + jnp.dot(merged.astype(BF16), wout_ref[...], preferred_element_type=F32)


def _merge(x2, g, wg, osb, ofx, omem, wsb, wfx, wmem, wout):
    t = x2.shape[0]
    row = lambda n: pl.BlockSpec((ROW_TILE, n), lambda i: (i, 0))
    return pl.pallas_call(
        _merge_kernel,
        grid=(t // ROW_TILE,),
        in_specs=[row(D_MODEL), _const_spec((1, D_MODEL)), _const_spec(wg.shape),
                  row(D_SB), row(D_FOX), row(D_MEM),
                  _const_spec(wsb.shape), _const_spec(wfx.shape), _const_spec(wmem.shape),
                  _const_spec(wout.shape)],
        out_specs=row(D_MODEL),
        out_shape=jax.ShapeDtypeStruct((t, D_MODEL), F32),
        compiler_params=_params(1),
        name="merge",
    )(x2, g, wg, osb, ofx, omem, wsb, wfx, wmem, wout)


def _mlp_kernel(x_ref, g_ref, wup_ref, wdn_ref, o_ref):
    x = x_ref[...]
    h = _rms(x, g_ref[...]).astype(BF16)
    acc = x
    for c in range(D_FF // FF_CHUNK):
        sl = slice(c * FF_CHUNK, (c + 1) * FF_CHUNK)
        u = jnp.maximum(jnp.dot(h, wup_ref[:, sl], preferred_element_type=F32), 0.0)
        acc = acc + jnp.dot((u * u).astype(BF16), wdn_ref[sl, :], preferred_element_type=F32)
    o_ref[...] = acc


def _mlp(x2, g, wup, wdn):
    t = x2.shape[0]
    row = pl.BlockSpec((ROW_TILE, D_MODEL), lambda i: (i, 0))
    return pl.pallas_call(
        _mlp_kernel,
        grid=(t // ROW_TILE,),
        in_specs=[row, _const_spec((1, D_MODEL)), _const_spec(wup.shape), _const_spec(wdn.shape)],
        out_specs=row,
        out_shape=jax.ShapeDtypeStruct((t, D_MODEL), F32),
        compiler_params=_params(1),
        name="mlp",
    )(x2, g, wup, wdn)


def _layer(x, mem, g_mix, g_memn, w_in, b_forget, g_fq, g_fk, g_mq, g_mk, w_mem_kv,
           w_sb, w_fox, w_mem, w_out, g_mlp, w_up, w_dn):
    b, s, _ = x.shape
    t = b * s
    x2 = x.reshape(t, D_MODEL)
    n_qkv = 3 * D_SB + 3 * D_FOX
    w_f = jnp.pad(w_in[:, n_qkv:n_qkv + FOX_HEADS], ((0, 0), (0, LANES - FOX_HEADS)))
    c_mq = n_qkv + FOX_HEADS
    w_proj = jnp.concatenate([w_in[:, :n_qkv], w_in[:, c_mq:c_mq + D_MEM], w_f], axis=1).astype(BF16)
    w_gate = w_in[:, c_mq + D_MEM:].astype(BF16)
    bf = jnp.pad(b_forget, (0, LANES - FOX_HEADS)).reshape(1, LANES)
    row = lambda a: a.reshape(1, -1)

    sbq, sbk, sbv, fxq, fxk, fxv, mq, lf = _proj(
        x2, row(g_mix), w_proj, bf, row(jnp.tile(g_fq, FOX_HEADS)), row(jnp.tile(g_fk, FOX_HEADS)),
        row(jnp.tile(g_mq, MEM_HEADS)))
    to3 = lambda a: a.reshape(b, s, a.shape[-1])

    aq, ak = _forget(to3(lf))
    o_sb = _sb_attention(to3(sbq), to3(sbk), to3(sbv))
    o_fox = _fox_attention(to3(fxq), to3(fxk), to3(fxv), aq, ak)
    mk, mv = _mem_kv(mem, row(g_memn), w_mem_kv.astype(BF16), row(jnp.tile(g_mk, MEM_HEADS)))
    o_mem = _mem_attention(to3(mq), mk, mv)

    x1 = _merge(x2, row(g_mix), w_gate, o_sb.reshape(t, D_SB), o_fox.reshape(t, D_FOX),
                o_mem.reshape(t, D_MEM), w_sb.astype(BF16), w_fox.astype(BF16), w_mem.astype(BF16),
                w_out.astype(BF16))
    out = _mlp(x1, row(g_mlp), w_up.astype(BF16), w_dn.astype(BF16))
    return out.reshape(b, s, D_MODEL)


def kernel(x, mem, g_mix_norm, g_mem_norm, w_in, b_forget, g_fox_q, g_fox_k, g_mem_q, g_mem_k,
           w_mem_kv, w_branch_sb, w_branch_fox, w_branch_mem, w_out, g_mlp_norm, w_ff_up, w_ff_down):
    for l in range(w_in.shape[0]):
        x = _layer(x, mem, g_mix_norm[l], g_mem_norm[l], w_in[l], b_forget[l], g_fox_q[l], g_fox_k[l],
                   g_mem_q[l], g_mem_k[l], w_mem_kv[l], w_branch_sb[l], w_branch_fox[l],
                   w_branch_mem[l], w_out[l], g_mlp_norm[l], w_ff_up[l], w_ff_down[l])
    return x
```

```python
import functools

import numpy as np
import jax
import jax.numpy as jnp
from jax import lax
from jax.experimental import pallas as pl
from jax.experimental.pallas import tpu as pltpu

D_MODEL = 1024
HD = 64
SB_HEADS = 8
FOX_HEADS = 8
MEM_HEADS = 4
MEM_HD = 128
D_SB = SB_HEADS * HD
D_FOX = FOX_HEADS * HD
D_MEM = MEM_HEADS * MEM_HD
N_BRANCH = 3
D_FF = 4 * D_MODEL
EPS = 1e-6
NEG_INF = -1e30

LANES = 128
AUG_STRIDE = 8
ROW_TILE = 512
ATT_Q_TILE = 512
ATT_K_TILE = 256
ATT_FLAGS = None
ATT_CHAINS = 1
MEM_Q_TILE = 512
FF_CHUNK = 512
VMEM_LIMIT = 56 * 1024 * 1024

F32 = jnp.float32
BF16 = jnp.bfloat16
_NT = (((1,), (1,)), ((), ()))
_LOG2E = 1.4426950408889634


def _const_spec(shape):
    nd = len(shape)
    return pl.BlockSpec(shape, lambda *_: (0,) * nd, pipeline_mode=pl.Buffered(1))


def _params(n_axes, flags=None):
    return pltpu.CompilerParams(
        dimension_semantics=("arbitrary",) * n_axes, vmem_limit_bytes=VMEM_LIMIT, flags=flags)


def _rms(x, g):
    ms = jnp.sum(x * x, axis=-1, keepdims=True) * (1.0 / x.shape[-1])
    return (x * lax.rsqrt(ms + EPS)) * g


def _headnorm(p, g, hd):
    rows, n = p.shape
    lane = lax.broadcasted_iota(jnp.int32, (rows, LANES), 1)
    outs = []
    for j in range(n // LANES):
        c = p[:, j * LANES:(j + 1) * LANES]
        c2 = c * c
        if hd == LANES:
            ms = jnp.sum(c2, axis=1, keepdims=True) * (1.0 / LANES)
        else:
            lo = jnp.sum(jnp.where(lane < hd, c2, 0.0), axis=1, keepdims=True)
            hi = jnp.sum(jnp.where(lane >= hd, c2, 0.0), axis=1, keepdims=True)
            ms = jnp.where(lane < hd, lo, hi) * (1.0 / hd)
        outs.append(c * lax.rsqrt(ms + EPS))
    return jnp.concatenate(outs, axis=1) * g


def _log_sigmoid(x):
    return jnp.minimum(x, 0.0) - jnp.log(1.0 + jnp.exp(-jnp.abs(x)))


def _proj_kernel(x_ref, g_ref, w_ref, bf_ref, gq_ref, gk_ref, gm_ref,
                 sbq_ref, sbk_ref, sbv_ref, fxq_ref, fxk_ref, fxv_ref, mq_ref, lf_ref):
    h = _rms(x_ref[...], g_ref[...]).astype(BF16)

    def mm(c0, n):
        return jnp.dot(h, w_ref[:, c0:c0 + n], preferred_element_type=F32)

    sbq_ref[...] = (mm(0, D_SB) * HD ** -0.5).astype(BF16)
    sbk_ref[...] = mm(D_SB, D_SB).astype(BF16)
    sbv_ref[...] = mm(2 * D_SB, D_SB).astype(BF16)
    c0 = 3 * D_SB
    fxq_ref[...] = (_headnorm(mm(c0, D_FOX), gq_ref[...], HD) * HD ** -0.5).astype(BF16)
    fxk_ref[...] = _headnorm(mm(c0 + D_FOX, D_FOX), gk_ref[...], HD).astype(BF16)
    fxv_ref[...] = mm(c0 + 2 * D_FOX, D_FOX).astype(BF16)
    c0 += 3 * D_FOX
    mq_ref[...] = _headnorm(mm(c0, D_MEM), gm_ref[...], MEM_HD).astype(BF16)
    lf_ref[...] = _log_sigmoid(mm(c0 + D_MEM, LANES) + bf_ref[...])


def _proj(x2, g, w, bf, gq, gk, gm):
    t = x2.shape[0]
    n_w = w.shape[1]
    row = lambda n: pl.BlockSpec((ROW_TILE, n), lambda i: (i, 0))
    out_shapes = [jax.ShapeDtypeStruct((t, D_SB), BF16)] * 7 + [jax.ShapeDtypeStruct((t, LANES), F32)]
    return pl.pallas_call(
        _proj_kernel,
        grid=(t // ROW_TILE,),
        in_specs=[row(D_MODEL), _const_spec((1, D_MODEL)), _const_spec((D_MODEL, n_w)),
                  _const_spec((1, LANES)), _const_spec((1, D_FOX)), _const_spec((1, D_FOX)),
                  _const_spec((1, D_MEM))],
        out_specs=[row(D_SB)] * 7 + [row(LANES)],
        out_shape=out_shapes,
        compiler_params=_params(1),
        name="proj",
    )(x2, g, w, bf, gq, gk, gm)


def _forget_kernel(lf_ref, p_ref, oq_ref, ok_ref, aq_ref, ak_ref):
    f = lf_ref[0]
    s_len = f.shape[0]
    row = lax.broadcasted_iota(jnp.int32, f.shape, 0)
    sh = 1
    while sh < s_len:
        f = f + jnp.where(row >= sh, pltpu.roll(f, sh, 0), 0.0)
        sh *= 2
    f_hi = f.astype(BF16)
    r1 = f - f_hi.astype(F32)
    f_mid = r1.astype(BF16)
    f_lo = (r1 - f_mid.astype(F32)).astype(BF16)
    pieces = jnp.concatenate([f_hi, f_mid, f_lo], axis=1)
    placed = jnp.dot(pieces, p_ref[...], preferred_element_type=F32)
    aq_ref[0] = (placed[:, :LANES] + oq_ref[...]).astype(BF16)
    ak_ref[0] = (placed[:, LANES:] + ok_ref[...]).astype(BF16)


def _forget_constants():
    p = np.zeros((3 * LANES, 2 * LANES), np.float32)
    oq = np.zeros((1, LANES), np.float32)
    ok = np.zeros((1, LANES), np.float32)
    for h in range(FOX_HEADS):
        for j in range(3):
            p[j * LANES + h, h * AUG_STRIDE + j] = 1.0
            p[j * LANES + h, LANES + h * AUG_STRIDE + 3 + j] = -1.0
            oq[0, h * AUG_STRIDE + 3 + j] = 1.0
            ok[0, h * AUG_STRIDE + j] = 1.0
    return jnp.asarray(p, BF16), jnp.asarray(oq), jnp.asarray(ok)


def _forget(lf3):
    b, s, _ = lf3.shape
    p, oq, ok = _forget_constants()
    blk = pl.BlockSpec((1, s, LANES), lambda i: (i, 0, 0))
    return pl.pallas_call(
        _forget_kernel,
        grid=(b,),
        in_specs=[blk, _const_spec(p.shape), _const_spec(oq.shape), _const_spec(ok.shape)],
        out_specs=[blk, blk],
        out_shape=[jax.ShapeDtypeStruct((b, s, LANES), BF16)] * 2,
        compiler_params=_params(1),
        name="forget",
    )(lf3, p, oq, ok)


def _stack_heads(x, lane):
    zero = jnp.zeros_like(x)
    return jnp.concatenate([jnp.where(lane < HD, x, zero), jnp.where(lane >= HD, x, zero)], axis=0)


def _sb_kernel(q_ref, k_ref, v_ref, m_ref, o_ref, acc_ref, carry_ref):
    tq = q_ref.shape[1]
    tk = m_ref.shape[0]
    qi = pl.program_id(2)
    lane = lax.broadcasted_iota(jnp.int32, (tq, LANES), 1)
    qs = _stack_heads(q_ref[0], lane)
    t_loc = lax.broadcasted_iota(jnp.int32, (2 * tq, tk), 0) & (tq - 1)
    col = lax.broadcasted_iota(jnp.int32, (2 * tq, tk), 1)
    n_sub = tq // tk

    def tiles(base, masks, acc, carry):
        order = list(reversed(range(n_sub)))
        msum = m_ref[...]
        starts = {j: pl.multiple_of((base + j) * tk, tk) for j in order}
        z = {j: lax.dot_general(qs, k_ref[0, pl.ds(starts[j], tk), :], _NT,
                                preferred_element_type=F32) for j in order}
        sp, c = {}, {}
        for j in order:
            sp[j] = jnp.maximum(z[j], 0.0) + jnp.log(1.0 + jnp.exp2(jnp.abs(z[j]) * (-_LOG2E)))
            if masks[j] is not None:
                sp[j] = jnp.where(masks[j], sp[j], 0.0)
            hi = sp[j].astype(BF16)
            lo = (sp[j] - hi.astype(F32)).astype(BF16)
            c[j] = (jnp.dot(hi, msum, preferred_element_type=F32)
                    + jnp.dot(lo, msum, preferred_element_type=F32))
        for j in order:
            w = jnp.exp(z[j] - c[j] - jnp.concatenate([carry] * (tk // LANES), axis=1))
            if masks[j] is not None:
                w = jnp.where(masks[j], w, 0.0)
            acc = acc + jnp.dot(w.astype(BF16), v_ref[0, pl.ds(starts[j], tk), :],
                                preferred_element_type=F32)
            carry = carry + jnp.sum(sp[j], axis=1, keepdims=True)
        return acc, carry

    zeros = jnp.zeros((2 * tq, LANES), F32)
    diag_masks = [col + j * tk < t_loc for j in range(n_sub)]
    acc_ref[...], carry_ref[...] = tiles(qi * n_sub, diag_masks, zeros, zeros)

    def body(i, _):
        acc_ref[...], carry_ref[...] = tiles((qi - 1 - i) * n_sub, [None] * n_sub,
                                             acc_ref[...], carry_ref[...])
        return 0

    lax.fori_loop(0, qi, body, 0)
    o_ref[0] = jnp.where(lane < HD, acc_ref[:tq, :], acc_ref[tq:, :]).astype(BF16)


def _sb_attention(q, k, v):
    b, s, d = q.shape
    tq, tk = ATT_Q_TILE, ATT_K_TILE
    ii = np.arange(tk)
    msum = jnp.asarray(ii[:, None] >= ii[None, :], BF16)
    qspec = pl.BlockSpec((1, tq, LANES), lambda bi, hp, qi: (bi, qi, hp))
    kvspec = pl.BlockSpec((1, s, LANES), lambda bi, hp, qi: (bi, 0, hp))
    return pl.pallas_call(
        _sb_kernel,
        grid=(b, d // LANES, s // tq),
        in_specs=[qspec, kvspec, kvspec, _const_spec((tk, tk))],
        out_specs=qspec,
        out_shape=jax.ShapeDtypeStruct((b, s, d), BF16),
        scratch_shapes=[pltpu.VMEM((2 * tq, LANES), F32), pltpu.VMEM((2 * tq, LANES), F32)],
        compiler_params=_params(3, ATT_FLAGS),
        name="sb_attn",
    )(q, k, v, msum)


def _fox_kernel(q_ref, aq_ref, k_ref, ak_ref, v_ref, o_ref, acc_ref, m_ref):
    tq = q_ref.shape[1]
    hp = pl.program_id(1)
    qi = pl.program_id(2)
    lane = lax.broadcasted_iota(jnp.int32, (tq, LANES), 1)
    aq = aq_ref[0]
    zero = jnp.zeros_like(aq)
    group = lane // AUG_STRIDE
    aug = jnp.concatenate([jnp.where(group == 2 * hp, aq, zero),
                           jnp.where(group == 2 * hp + 1, aq, zero)], axis=0)
    qext = jnp.concatenate([_stack_heads(q_ref[0], lane), aug], axis=1)
    causal = (lax.broadcasted_iota(jnp.int32, (2 * tq, tq), 1)
              <= (lax.broadcasted_iota(jnp.int32, (2 * tq, tq), 0) & (tq - 1)))
    acc_ref[...] = jnp.zeros_like(acc_ref)
    m_ref[...] = jnp.full(m_ref.shape, NEG_INF, F32)
    ones = jnp.ones((tq, LANES), BF16)

    def tile(kb, diag):
        start = pl.multiple_of(kb * tq, tq)
        kext = jnp.concatenate([k_ref[0, pl.ds(start, tq), :], ak_ref[0, pl.ds(start, tq), :]], axis=1)
        vext = jnp.concatenate([v_ref[0, pl.ds(start, tq), :], ones], axis=1)
        half = tq // 2
        zs = [lax.dot_general(qext, kext[j * half:(j + 1) * half], _NT, preferred_element_type=F32)
              for j in range(2)]
        if diag:
            zs = [jnp.where(causal[:, j * half:(j + 1) * half], zs[j], NEG_INF) for j in range(2)]
        m_prev = m_ref[...]
        m_new = jnp.maximum(m_prev, jnp.max(jnp.maximum(zs[0], zs[1]), axis=1, keepdims=True))
        alpha = jnp.exp(m_prev - m_new)
        m_b = jnp.concatenate([m_new] * (half // LANES), axis=1)
        pv = [jnp.dot(jnp.exp(zs[j] - m_b).astype(BF16), vext[j * half:(j + 1) * half],
                      preferred_element_type=F32) for j in range(2)]
        acc_ref[...] = jnp.concatenate([alpha, alpha], axis=1) * acc_ref[...] + pv[0] + pv[1]
        m_ref[...] = m_new

    tile(qi, True)

    def body(j, _):
        tile(qi - 1 - j, False)
        return 0

    lax.fori_loop(0, qi, body, 0)
    o = acc_ref[:, :LANES] / acc_ref[:, LANES:]
    o_ref[0] = jnp.where(lane < HD, o[:tq], o[tq:]).astype(BF16)


def _fox_attention(q, k, v, aq, ak):
    b, s, d = q.shape
    t = ATT_Q_TILE
    qspec = pl.BlockSpec((1, t, LANES), lambda bi, hp, qi: (bi, qi, hp))
    aqspec = pl.BlockSpec((1, t, LANES), lambda bi, hp, qi: (bi, qi, 0))
    kvspec = pl.BlockSpec((1, s, LANES), lambda bi, hp, qi: (bi, 0, hp))
    akspec = pl.BlockSpec((1, s, LANES), lambda bi, hp, qi: (bi, 0, 0))
    return pl.pallas_call(
        _fox_kernel,
        grid=(b, d // LANES, s // t),
        in_specs=[qspec, aqspec, kvspec, akspec, kvspec],
        out_specs=qspec,
        out_shape=jax.ShapeDtypeStruct((b, s, d), BF16),
        scratch_shapes=[pltpu.VMEM((2 * t, 2 * LANES), F32), pltpu.VMEM((2 * t, LANES), F32)],
        compiler_params=_params(3, ATT_FLAGS),
        name="fox_attn",
    )(q, aq, k, ak, v)


def _mem_kv_kernel(mem_ref, g_ref, w_ref, gk_ref, mk_ref, mv_ref):
    mh = _rms(mem_ref[0], g_ref[...]).astype(BF16)
    kv = jnp.dot(mh, w_ref[...], preferred_element_type=F32)
    mk_ref[0] = _headnorm(kv[:, :D_MEM], gk_ref[...], MEM_HD).astype(BF16)
    mv_ref[0] = kv[:, D_MEM:].astype(BF16)


def _mem_kv(mem, g, w, gk):
    b, m, _ = mem.shape
    out = pl.BlockSpec((1, m, D_MEM), lambda i: (i, 0, 0))
    return pl.pallas_call(
        _mem_kv_kernel,
        grid=(b,),
        in_specs=[pl.BlockSpec((1, m, D_MODEL), lambda i: (i, 0, 0)), _const_spec((1, D_MODEL)),
                  _const_spec((D_MODEL, 2 * D_MEM)), _const_spec((1, D_MEM))],
        out_specs=[out, out],
        out_shape=[jax.ShapeDtypeStruct((b, m, D_MEM), BF16)] * 2,
        compiler_params=_params(1),
        name="mem_kv",
    )(mem, g, w, gk)


def _mem_attn_kernel(q_ref, k_ref, v_ref, o_ref):
    for h in range(MEM_HEADS):
        sl = slice(h * MEM_HD, (h + 1) * MEM_HD)
        z = lax.dot_general(q_ref[0, :, sl], k_ref[0, :, sl], _NT,
                            preferred_element_type=F32) * MEM_HD ** -0.5
        e = jnp.exp(z - jnp.max(z, axis=1, keepdims=True))
        p = e / jnp.sum(e, axis=1, keepdims=True)
        o_ref[0, :, sl] = jnp.dot(p.astype(BF16), v_ref[0, :, sl],
                                  preferred_element_type=F32).astype(BF16)


def _mem_attention(q, k, v):
    b, s, d = q.shape
    m = k.shape[1]
    qspec = pl.BlockSpec((1, MEM_Q_TILE, d), lambda bi, qi: (bi, qi, 0))
    kvspec = pl.BlockSpec((1, m, d), lambda bi, qi: (bi, 0, 0))
    return pl.pallas_call(
        _mem_attn_kernel,
        grid=(b, s // MEM_Q_TILE),
        in_specs=[qspec, kvspec, kvspec],
        out_specs=qspec,
        out_shape=jax.ShapeDtypeStruct((b, s, d), BF16),
        compiler_params=_params(2),
        name="mem_attn",
    )(q, k, v)


def _merge_kernel(x_ref, g_ref, wg_ref, osb_ref, ofx_ref, omem_ref,
                  wsb_ref, wfx_ref, wmem_ref, wout_ref, o_ref):
    x = x_ref[...]
    h = _rms(x, g_ref[...]).astype(BF16)
    merged = None
    for i, (o_br, w_br) in enumerate(((osb_ref, wsb_ref), (ofx_ref, wfx_ref), (omem_ref, wmem_ref))):
        gate = jax.nn.sigmoid(jnp.dot(h, wg_ref[:, i * D_MODEL:(i + 1) * D_MODEL],
                                      preferred_element_type=F32))
        term = gate * jnp.dot(o_br[...], w_br[...], preferred_element_type=F32)
        merged = term if merged is None else merged + term
    o_ref[...] = x + jnp.dot(merged.astype(BF16), wout_ref[...], preferred_element_type=F32)


def _merge(x2, g, wg, osb, ofx, omem, wsb, wfx, wmem, wout):
    t = x2.shape[0]
    row = lambda n: pl.BlockSpec((ROW_TILE, n), lambda i: (i, 0))
    return pl.pallas_call(
        _merge_kernel,
        grid=(t // ROW_TILE,),
        in_specs=[row(D_MODEL), _const_spec((1, D_MODEL)), _const_spec(wg.shape),
                  row(D_SB), row(D_FOX), row(D_MEM),
                  _const_spec(wsb.shape), _const_spec(wfx.shape), _const_spec(wmem.shape),
                  _const_spec(wout.shape)],
        out_specs=row(D_MODEL),
        out_shape=jax.ShapeDtypeStruct((t, D_MODEL), F32),
        compiler_params=_params(1),
        name="merge",
    )(x2, g, wg, osb, ofx, omem, wsb, wfx, wmem, wout)


def _mlp_kernel(x_ref, g_ref, wup_ref, wdn_ref, o_ref):
    x = x_ref[...]
    h = _rms(x, g_ref[...]).astype(BF16)
    acc = x
    for c in range(D_FF // FF_CHUNK):
        sl = slice(c * FF_CHUNK, (c + 1) * FF_CHUNK)
        u = jnp.maximum(jnp.dot(h, wup_ref[:, sl], preferred_element_type=F32), 0.0)
        acc = acc + jnp.dot((u * u).astype(BF16), wdn_ref[sl, :], preferred_element_type=F32)
    o_ref[...] = acc


def _mlp(x2, g, wup, wdn):
    t = x2.shape[0]
    row = pl.BlockSpec((ROW_TILE, D_MODEL), lambda i: (i, 0))
    return pl.pallas_call(
        _mlp_kernel,
        grid=(t // ROW_TILE,),
        in_specs=[row, _const_spec((1, D_MODEL)), _const_spec(wup.shape), _const_spec(wdn.shape)],
        out_specs=row,
        out_shape=jax.ShapeDtypeStruct((t, D_MODEL), F32),
        compiler_params=_params(1),
        name="mlp",
    )(x2, g, wup, wdn)


def _layer(x, mem, g_mix, g_memn, w_in, b_forget, g_fq, g_fk, g_mq, g_mk, w_mem_kv,
           w_sb, w_fox, w_mem, w_out, g_mlp, w_up, w_dn):
    b, s, _ = x.shape
    t = b * s
    x2 = x.reshape(t, D_MODEL)
    n_qkv = 3 * D_SB + 3 * D_FOX
    w_f = jnp.pad(w_in[:, n_qkv:n_qkv + FOX_HEADS], ((0, 0), (0, LANES - FOX_HEADS)))
    c_mq = n_qkv + FOX_HEADS
    w_proj = jnp.concatenate([w_in[:, :n_qkv], w_in[:, c_mq:c_mq + D_MEM], w_f], axis=1).astype(BF16)
    w_gate = w_in[:, c_mq + D_MEM:].astype(BF16)
    bf = jnp.pad(b_forget, (0, LANES - FOX_HEADS)).reshape(1, LANES)
    row = lambda a: a.reshape(1, -1)

    sbq, sbk, sbv, fxq, fxk, fxv, mq, lf = _proj(
        x2, row(g_mix), w_proj, bf, row(jnp.tile(g_fq, FOX_HEADS)), row(jnp.tile(g_fk, FOX_HEADS)),
        row(jnp.tile(g_mq, MEM_HEADS)))
    to3 = lambda a: a.reshape(b, s, a.shape[-1])

    aq, ak = _forget(to3(lf))
    o_sb = _sb_attention(to3(sbq), to3(sbk), to3(sbv))
    o_fox = _fox_attention(to3(fxq), to3(fxk), to3(fxv), aq, ak)
    mk, mv = _mem_kv(mem, row(g_memn), w_mem_kv.astype(BF16), row(jnp.tile(g_mk, MEM_HEADS)))
    o_mem = _mem_attention(to3(mq), mk, mv)

    x1 = _merge(x2, row(g_mix), w_gate, o_sb.reshape(t, D_SB), o_fox.reshape(t, D_FOX),
                o_mem.reshape(t, D_MEM), w_sb.astype(BF16), w_fox.astype(BF16), w_mem.astype(BF16),
                w_out.astype(BF16))
    out = _mlp(x1, row(g_mlp), w_up.astype(BF16), w_dn.astype(BF16))
    return out.reshape(b, s, D_MODEL)


def kernel(x, mem, g_mix_norm, g_mem_norm, w_in, b_forget, g_fox_q, g_fox_k, g_mem_q, g_mem_k,
           w_mem_kv, w_branch_sb, w_branch_fox, w_branch_mem, w_out, g_mlp_norm, w_ff_up, w_ff_down):
    for l in range(w_in.shape[0]):
        x = _layer(x, mem, g_mix_norm[l], g_mem_norm[l], w_in[l], b_forget[l], g_fox_q[l], g_fox_k[l],
                   g_mem_q[l], g_mem_k[l], w_mem_kv[l], w_branch_sb[l], w_branch_fox[l],
                   w_branch_mem[l], w_out[l], g_mlp_norm[l], w_ff_up[l], w_ff_down[l])
    return x
```

```python
import numpy as np
import jax
import jax.numpy as jnp
from jax import lax
from jax.experimental import pallas as pl
from jax.experimental.pallas import tpu as pltpu

D_MODEL = 1024
HD = 64
SB_HEADS = 8
FOX_HEADS = 8
MEM_HEADS = 4
MEM_HD = 128
D_SB = SB_HEADS * HD
D_FOX = FOX_HEADS * HD
D_MEM = MEM_HEADS * MEM_HD
N_BRANCH = 3
D_FF = 4 * D_MODEL
EPS = 1e-6
NEG_INF = -1e30

LANES = 128
AUG_STRIDE = 8
ROW_TILE = 512
ATT_Q_TILE = 512
ATT_K_TILE = 256
MEM_Q_TILE = 512
FF_CHUNK = 512
VMEM_LIMIT = 56 * 1024 * 1024

F32 = jnp.float32
BF16 = jnp.bfloat16
_NT = (((1,), (1,)), ((), ()))
_LOG2E = 1.4426950408889634


def _const_spec(shape):
    nd = len(shape)
    return pl.BlockSpec(shape, lambda *_: (0,) * nd, pipeline_mode=pl.Buffered(1))


def _params(n_axes):
    return pltpu.CompilerParams(
        dimension_semantics=("arbitrary",) * n_axes, vmem_limit_bytes=VMEM_LIMIT)


def _rms(x, g):
    ms = jnp.sum(x * x, axis=-1, keepdims=True) * (1.0 / x.shape[-1])
    return (x * lax.rsqrt(ms + EPS)) * g


def _headnorm(p, g, hd):
    rows, n = p.shape
    lane = lax.broadcasted_iota(jnp.int32, (rows, LANES), 1)
    outs = []
    for j in range(n // LANES):
        c = p[:, j * LANES:(j + 1) * LANES]
        c2 = c * c
        if hd == LANES:
            ms = jnp.sum(c2, axis=1, keepdims=True) * (1.0 / LANES)
        else:
            lo = jnp.sum(jnp.where(lane < hd, c2, 0.0), axis=1, keepdims=True)
            hi = jnp.sum(jnp.where(lane >= hd, c2, 0.0), axis=1, keepdims=True)
            ms = jnp.where(lane < hd, lo, hi) * (1.0 / hd)
        outs.append(c * lax.rsqrt(ms + EPS))
    return jnp.concatenate(outs, axis=1) * g


def _log_sigmoid(x):
    return jnp.minimum(x, 0.0) - jnp.log(1.0 + jnp.exp(-jnp.abs(x)))


def _proj_kernel(x_ref, g_ref, w_ref, bf_ref, gq_ref, gk_ref, gm_ref,
                 sbq_ref, sbk_ref, sbv_ref, fxq_ref, fxk_ref, fxv_ref, mq_ref, lf_ref):
    h = _rms(x_ref[...], g_ref[...]).astype(BF16)

    def mm(c0, n):
        return jnp.dot(h, w_ref[:, c0:c0 + n], preferred_element_type=F32)

    sbq_ref[...] = (mm(0, D_SB) * HD ** -0.5).astype(BF16)
    sbk_ref[...] = mm(D_SB, D_SB).astype(BF16)
    sbv_ref[...] = mm(2 * D_SB, D_SB).astype(BF16)
    c0 = 3 * D_SB
    fxq_ref[...] = (_headnorm(mm(c0, D_FOX), gq_ref[...], HD) * HD ** -0.5).astype(BF16)
    fxk_ref[...] = _headnorm(mm(c0 + D_FOX, D_FOX), gk_ref[...], HD).astype(BF16)
    fxv_ref[...] = mm(c0 + 2 * D_FOX, D_FOX).astype(BF16)
    c0 += 3 * D_FOX
    mq_ref[...] = _headnorm(mm(c0, D_MEM), gm_ref[...], MEM_HD).astype(BF16)
    lf_ref[...] = _log_sigmoid(mm(c0 + D_MEM, LANES) + bf_ref[...])


def _proj(x2, g, w, bf, gq, gk, gm):
    t = x2.shape[0]
    n_w = w.shape[1]
    row = lambda n: pl.BlockSpec((ROW_TILE, n), lambda i: (i, 0))
    out_shapes = [jax.ShapeDtypeStruct((t, D_SB), BF16)] * 7 + [jax.ShapeDtypeStruct((t, LANES), F32)]
    return pl.pallas_call(
        _proj_kernel,
        grid=(t // ROW_TILE,),
        in_specs=[row(D_MODEL), _const_spec((1, D_MODEL)), _const_spec((D_MODEL, n_w)),
                  _const_spec((1, LANES)), _const_spec((1, D_FOX)), _const_spec((1, D_FOX)),
                  _const_spec((1, D_MEM))],
        out_specs=[row(D_SB)] * 7 + [row(LANES)],
        out_shape=out_shapes,
        compiler_params=_params(1),
        name="proj",
    )(x2, g, w, bf, gq, gk, gm)


def _forget_kernel(lf_ref, p_ref, oq_ref, ok_ref, aq_ref, ak_ref):
    f = lf_ref[0]
    s_len = f.shape[0]
    row = lax.broadcasted_iota(jnp.int32, f.shape, 0)
    sh = 1
    while sh < s_len:
        f = f + jnp.where(row >= sh, pltpu.roll(f, sh, 0), 0.0)
        sh *= 2
    f_hi = f.astype(BF16)
    r1 = f - f_hi.astype(F32)
    f_mid = r1.astype(BF16)
    f_lo = (r1 - f_mid.astype(F32)).astype(BF16)
    pieces = jnp.concatenate([f_hi, f_mid, f_lo], axis=1)
    placed = jnp.dot(pieces, p_ref[...], preferred_element_type=F32)
    aq_ref[0] = (placed[:, :LANES] + oq_ref[...]).astype(BF16)
    ak_ref[0] = (placed[:, LANES:] + ok_ref[...]).astype(BF16)


def _forget_constants():
    p = np.zeros((3 * LANES, 2 * LANES), np.float32)
    oq = np.zeros((1, LANES), np.float32)
    ok = np.zeros((1, LANES), np.float32)
    for h in range(FOX_HEADS):
        for j in range(3):
            p[j * LANES + h, h * AUG_STRIDE + j] = 1.0
            p[j * LANES + h, LANES + h * AUG_STRIDE + 3 + j] = -1.0
            oq[0, h * AUG_STRIDE + 3 + j] = 1.0
            ok[0, h * AUG_STRIDE + j] = 1.0
    return jnp.asarray(p, BF16), jnp.asarray(oq), jnp.asarray(ok)


def _forget(lf3):
    b, s, _ = lf3.shape
    p, oq, ok = _forget_constants()
    blk = pl.BlockSpec((1, s, LANES), lambda i: (i, 0, 0))
    return pl.pallas_call(
        _forget_kernel,
        grid=(b,),
        in_specs=[blk, _const_spec(p.shape), _const_spec(oq.shape), _const_spec(ok.shape)],
        out_specs=[blk, blk],
        out_shape=[jax.ShapeDtypeStruct((b, s, LANES), BF16)] * 2,
        compiler_params=_params(1),
        name="forget",
    )(lf3, p, oq, ok)


def _stack_heads(x, lane):
    zero = jnp.zeros_like(x)
    return jnp.concatenate([jnp.where(lane < HD, x, zero), jnp.where(lane >= HD, x, zero)], axis=0)


def _unstack_heads(x, lane):
    rows = x.shape[0] // 2
    return jnp.where(lane < HD, x[:rows], x[rows:])


def _sb_kernel(q_ref, k_ref, v_ref, m_ref, o_ref):
    s_len = q_ref.shape[1]
    tq, tk = ATT_Q_TILE, ATT_K_TILE
    n_sub = tq // tk
    lane = lax.broadcasted_iota(jnp.int32, (tq, LANES), 1)
    t_loc = lax.broadcasted_iota(jnp.int32, (2 * tq, tk), 0) & (tq - 1)
    col = lax.broadcasted_iota(jnp.int32, (2 * tq, tk), 1)
    msum = m_ref[...]
    tiles = [(qi, j) for qi in range(s_len // tq) for j in reversed(range((qi + 1) * n_sub))]
    qs, z, sp, c, acc, carry = {}, {}, {}, {}, {}, {}

    def mask_of(qi, j):
        off = j * tk - qi * tq
        return (col + off < t_loc) if off >= 0 else None

    def scores(n):
        qi, j = tiles[n]
        if qi not in qs:
            qs[qi] = _stack_heads(q_ref[0, qi * tq:(qi + 1) * tq, :], lane)
        z[n] = lax.dot_general(qs[qi], k_ref[0, j * tk:(j + 1) * tk, :], _NT,
                               preferred_element_type=F32)

    def suffix_sums(n):
        mask = mask_of(*tiles[n])
        s = jnp.maximum(z[n], 0.0) + jnp.log(1.0 + jnp.exp2(jnp.abs(z[n]) * (-_LOG2E)))
        if mask is not None:
            s = jnp.where(mask, s, 0.0)
        hi = s.astype(BF16)
        lo = (s - hi.astype(F32)).astype(BF16)
        sp[n] = s
        c[n] = (jnp.dot(hi, msum, preferred_element_type=F32)
                + jnp.dot(lo, msum, preferred_element_type=F32))

    def weights_pv(n):
        qi, j = tiles[n]
        mask = mask_of(qi, j)
        cr = carry.get(qi, jnp.zeros((2 * tq, LANES), F32))
        w = jnp.exp(z.pop(n) - c.pop(n) - jnp.concatenate([cr] * (tk // LANES), axis=1))
        if mask is not None:
            w = jnp.where(mask, w, 0.0)
        pv = jnp.dot(w.astype(BF16), v_ref[0, j * tk:(j + 1) * tk, :], preferred_element_type=F32)
        acc[qi] = pv if qi not in acc else acc[qi] + pv
        carry[qi] = cr + jnp.sum(sp.pop(n), axis=1, keepdims=True)
        if j == 0:
            o_ref[0, qi * tq:(qi + 1) * tq, :] = _unstack_heads(acc.pop(qi), lane).astype(BF16)
            carry.pop(qi)
            qs.pop(qi)

    n_tiles = len(tiles)
    for step in range(n_tiles + 2):
        if step < n_tiles:
            scores(step)
        if 1 <= step <= n_tiles:
            suffix_sums(step - 1)
        if step >= 2:
            weights_pv(step - 2)


def _sb_attention(q, k, v):
    b, s, d = q.shape
    tk = ATT_K_TILE
    ii = np.arange(tk)
    msum = jnp.asarray(ii[:, None] >= ii[None, :], BF16)
    spec = pl.BlockSpec((1, s, LANES), lambda bi, hp: (bi, 0, hp))
    return pl.pallas_call(
        _sb_kernel,
        grid=(b, d // LANES),
        in_specs=[spec, spec, spec, _const_spec((tk, tk))],
        out_specs=spec,
        out_shape=jax.ShapeDtypeStruct((b, s, d), BF16),
        compiler_params=_params(2),
        name="sb_attn",
    )(q, k, v, msum)


def _fox_kernel(q_ref, aq_ref, k_ref, ak_ref, v_ref, o_ref):
    s_len = q_ref.shape[1]
    t = ATT_Q_TILE
    hp = pl.program_id(1)
    lane = lax.broadcasted_iota(jnp.int32, (t, LANES), 1)
    group = lane // AUG_STRIDE
    causal = (lax.broadcasted_iota(jnp.int32, (2 * t, t), 1)
              <= (lax.broadcasted_iota(jnp.int32, (2 * t, t), 0) & (t - 1)))
    ones = jnp.ones((t, LANES), BF16)
    tiles = [(qi, kb) for qi in range(s_len // t) for kb in range(qi + 1)]
    qext, kext, vext, z, m, acc = {}, {}, {}, {}, {}, {}

    def scores(n):
        qi, kb = tiles[n]
        rows = slice(qi * t, (qi + 1) * t)
        keys = slice(kb * t, (kb + 1) * t)
        if qi not in qext:
            aq = aq_ref[0, rows, :]
            zero = jnp.zeros_like(aq)
            aug = jnp.concatenate([jnp.where(group == 2 * hp, aq, zero),
                                   jnp.where(group == 2 * hp + 1, aq, zero)], axis=0)
            qext[qi] = jnp.concatenate([_stack_heads(q_ref[0, rows, :], lane), aug], axis=1)
        if kb not in kext:
            kext[kb] = jnp.concatenate([k_ref[0, keys, :], ak_ref[0, keys, :]], axis=1)
            vext[kb] = jnp.concatenate([v_ref[0, keys, :], ones], axis=1)
        s = lax.dot_general(qext[qi], kext[kb], _NT, preferred_element_type=F32)
        z[n] = jnp.where(causal, s, NEG_INF) if kb == qi else s

    def softmax_pv(n):
        qi, kb = tiles[n]
        s = z.pop(n)
        row_max = jnp.max(s, axis=1, keepdims=True)
        if qi not in m:
            m_new = jnp.broadcast_to(row_max, (2 * t, LANES))
        else:
            m_new = jnp.maximum(m[qi], row_max)
        p = jnp.exp(s - jnp.concatenate([m_new] * (t // LANES), axis=1))
        pv = jnp.dot(p.astype(BF16), vext[kb], preferred_element_type=F32)
        if qi not in m:
            acc[qi] = pv
        else:
            alpha = jnp.exp(m[qi] - m_new)
            acc[qi] = jnp.concatenate([alpha, alpha], axis=1) * acc[qi] + pv
        m[qi] = m_new
        if kb == qi:
            a = acc.pop(qi)
            o_ref[0, qi * t:(qi + 1) * t, :] = _unstack_heads(a[:, :LANES] / a[:, LANES:], lane).astype(BF16)
            m.pop(qi)
            qext.pop(qi)

    n_tiles = len(tiles)
    for step in range(n_tiles + 1):
        if step < n_tiles:
            scores(step)
        if step >= 1:
            softmax_pv(step - 1)


def _fox_attention(q, k, v, aq, ak):
    b, s, d = q.shape
    spec = pl.BlockSpec((1, s, LANES), lambda bi, hp: (bi, 0, hp))
    aspec = pl.BlockSpec((1, s, LANES), lambda bi, hp: (bi, 0, 0))
    return pl.pallas_call(
        _fox_kernel,
        grid=(b, d // LANES),
        in_specs=[spec, aspec, spec, aspec, spec],
        out_specs=spec,
        out_shape=jax.ShapeDtypeStruct((b, s, d), BF16),
        compiler_params=_params(2),
        name="fox_attn",
    )(q, aq, k, ak, v)


def _mem_kv_kernel(mem_ref, g_ref, w_ref, gk_ref, mk_ref, mv_ref):
    mh = _rms(mem_ref[0], g_ref[...]).astype(BF16)
    kv = jnp.dot(mh, w_ref[...], preferred_element_type=F32)
    mk_ref[0] = _headnorm(kv[:, :D_MEM], gk_ref[...], MEM_HD).astype(BF16)
    mv_ref[0] = kv[:, D_MEM:].astype(BF16)


def _mem_kv(mem, g, w, gk):
    b, m, _ = mem.shape
    out = pl.BlockSpec((1, m, D_MEM), lambda i: (i, 0, 0))
    return pl.pallas_call(
        _mem_kv_kernel,
        grid=(b,),
        in_specs=[pl.BlockSpec((1, m, D_MODEL), lambda i: (i, 0, 0)), _const_spec((1, D_MODEL)),
                  _const_spec((D_MODEL, 2 * D_MEM)), _const_spec((1, D_MEM))],
        out_specs=[out, out],
        out_shape=[jax.ShapeDtypeStruct((b, m, D_MEM), BF16)] * 2,
        compiler_params=_params(1),
        name="mem_kv",
    )(mem, g, w, gk)


def _mem_attn_kernel(q_ref, k_ref, v_ref, o_ref):
    for h in range(MEM_HEADS):
        sl = slice(h * MEM_HD, (h + 1) * MEM_HD)
        z = lax.dot_general(q_ref[0, :, sl], k_ref[0, :, sl], _NT,
                            preferred_element_type=F32) * MEM_HD ** -0.5
        e = jnp.exp(z - jnp.max(z, axis=1, keepdims=True))
        p = e / jnp.sum(e, axis=1, keepdims=True)
        o_ref[0, :, sl] = jnp.dot(p.astype(BF16), v_ref[0, :, sl],
                                  preferred_element_type=F32).astype(BF16)


def _mem_attention(q, k, v):
    b, s, d = q.shape
    m = k.shape[1]
    qspec = pl.BlockSpec((1, MEM_Q_TILE, d), lambda bi, qi: (bi, qi, 0))
    kvspec = pl.BlockSpec((1, m, d), lambda bi, qi: (bi, 0, 0))
    return pl.pallas_call(
        _mem_attn_kernel,
        grid=(b, s // MEM_Q_TILE),
        in_specs=[qspec, kvspec, kvspec],
        out_specs=qspec,
        out_shape=jax.ShapeDtypeStruct((b, s, d), BF16),
        compiler_params=_params(2),
        name="mem_attn",
    )(q, k, v)


def _merge_kernel(x_ref, g_ref, wg_ref, osb_ref, ofx_ref, omem_ref,
                  wsb_ref, wfx_ref, wmem_ref, wout_ref, o_ref):
    x = x_ref[...]
    h = _rms(x, g_ref[...]).astype(BF16)
    merged = None
    for i, (o_br, w_br) in enumerate(((osb_ref, wsb_ref), (ofx_ref, wfx_ref), (omem_ref, wmem_ref))):
        gate = jax.nn.sigmoid(jnp.dot(h, wg_ref[:, i * D_MODEL:(i + 1) * D_MODEL],
                                      preferred_element_type=F32))
        term = gate * jnp.dot(o_br[...], w_br[...], preferred_element_type=F32)
        merged = term if merged is None else merged + term
    o_ref[...] = x + jnp.dot(merged.astype(BF16), wout_ref[...], preferred_element_type=F32)


def _merge(x2, g, wg, osb, ofx, omem, wsb, wfx, wmem, wout):
    t = x2.shape[0]
    row = lambda n: pl.BlockSpec((ROW_TILE, n), lambda i: (i, 0))
    return pl.pallas_call(
        _merge_kernel,
        grid=(t // ROW_TILE,),
        in_specs=[row(D_MODEL), _const_spec((1, D_MODEL)), _const_spec(wg.shape),
                  row(D_SB), row(D_FOX), row(D_MEM),
                  _const_spec(wsb.shape), _const_spec(wfx.shape), _const_spec(wmem.shape),
                  _const_spec(wout.shape)],
        out_specs=row(D_MODEL),
        out_shape=jax.ShapeDtypeStruct((t, D_MODEL), F32),
        compiler_params=_params(1),
        name="merge",
    )(x2, g, wg, osb, ofx, omem, wsb, wfx, wmem, wout)


def _mlp_kernel(x_ref, g_ref, wup_ref, wdn_ref, o_ref):
    x = x_ref[...]
    h = _rms(x, g_ref[...]).astype(BF16)
    acc = x
    for c in range(D_FF // FF_CHUNK):
        sl = slice(c * FF_CHUNK, (c + 1) * FF_CHUNK)
        u = jnp.maximum(jnp.dot(h, wup_ref[:, sl], preferred_element_type=F32), 0.0)
        acc = acc + jnp.dot((u * u).astype(BF16), wdn_ref[sl, :], preferred_element_type=F32)
    o_ref[...] = acc


def _mlp(x2, g, wup, wdn):
    t = x2.shape[0]
    row = pl.BlockSpec((ROW_TILE, D_MODEL), lambda i: (i, 0))
    return pl.pallas_call(
        _mlp_kernel,
        grid=(t // ROW_TILE,),
        in_specs=[row, _const_spec((1, D_MODEL)), _const_spec(wup.shape), _const_spec(wdn.shape)],
        out_specs=row,
        out_shape=jax.ShapeDtypeStruct((t, D_MODEL), F32),
        compiler_params=_params(1),
        name="mlp",
    )(x2, g, wup, wdn)


def _layer(x, mem, g_mix, g_memn, w_in, b_forget, g_fq, g_fk, g_mq, g_mk, w_mem_kv,
           w_sb, w_fox, w_mem, w_out, g_mlp, w_up, w_dn):
    b, s, _ = x.shape
    t = b * s
    x2 = x.reshape(t, D_MODEL)
    n_qkv = 3 * D_SB + 3 * D_FOX
    w_f = jnp.pad(w_in[:, n_qkv:n_qkv + FOX_HEADS], ((0, 0), (0, LANES - FOX_HEADS)))
    c_mq = n_qkv + FOX_HEADS
    w_proj = jnp.concatenate([w_in[:, :n_qkv], w_in[:, c_mq:c_mq + D_MEM], w_f], axis=1).astype(BF16)
    w_gate = w_in[:, c_mq + D_MEM:].astype(BF16)
    bf = jnp.pad(b_forget, (0, LANES - FOX_HEADS)).reshape(1, LANES)
    row = lambda a: a.reshape(1, -1)

    sbq, sbk, sbv, fxq, fxk, fxv, mq, lf = _proj(
        x2, row(g_mix), w_proj, bf, row(jnp.tile(g_fq, FOX_HEADS)), row(jnp.tile(g_fk, FOX_HEADS)),
        row(jnp.tile(g_mq, MEM_HEADS)))
    to3 = lambda a: a.reshape(b, s, a.shape[-1])

    aq, ak = _forget(to3(lf))
    o_sb = _sb_attention(to3(sbq), to3(sbk), to3(sbv))
    o_fox = _fox_attention(to3(fxq), to3(fxk), to3(fxv), aq, ak)
    mk, mv = _mem_kv(mem, row(g_memn), w_mem_kv.astype(BF16), row(jnp.tile(g_mk, MEM_HEADS)))
    o_mem = _mem_attention(to3(mq), mk, mv)

    x1 = _merge(x2, row(g_mix), w_gate, o_sb.reshape(t, D_SB), o_fox.reshape(t, D_FOX),
                o_mem.reshape(t, D_MEM), w_sb.astype(BF16), w_fox.astype(BF16), w_mem.astype(BF16),
                w_out.astype(BF16))
    out = _mlp(x1, row(g_mlp), w_up.astype(BF16), w_dn.astype(BF16))
    return out.reshape(b, s, D_MODEL)


def kernel(x, mem, g_mix_norm, g_mem_norm, w_in, b_forget, g_fox_q, g_fox_k, g_mem_q, g_mem_k,
           w_mem_kv, w_branch_sb, w_branch_fox, w_branch_mem, w_out, g_mlp_norm, w_ff_up, w_ff_down):
    for l in range(w_in.shape[0]):
        x = _layer(x, mem, g_mix_norm[l], g_mem_norm[l], w_in[l], b_forget[l], g_fox_q[l], g_fox_k[l],
                   g_mem_q[l], g_mem_k[l], w_mem_kv[l], w_branch_sb[l], w_branch_fox[l],
                   w_branch_mem[l], w_out[l], g_mlp_norm[l], w_ff_up[l], w_ff_down[l])
    return x
```

```python
import numpy as np
import jax
import jax.numpy as jnp
from jax import lax
from jax.experimental import pallas as pl
from jax.experimental.pallas import tpu as pltpu

D_MODEL = 1024
HD = 64
SB_HEADS = 8
FOX_HEADS = 8
MEM_HEADS = 4
MEM_HD = 128
D_SB = SB_HEADS * HD
D_FOX = FOX_HEADS * HD
D_MEM = MEM_HEADS * MEM_HD
N_BRANCH = 3
D_FF = 4 * D_MODEL
EPS = 1e-6
NEG_INF = -1e30

LANES = 128
AUG_STRIDE = 8
ROW_TILE = 512
ATT_Q_TILE = 512
ATT_K_TILE = 256
SB_EXIT = 110.0
MEM_Q_TILE = 512
FF_CHUNK = 512
VMEM_LIMIT = 56 * 1024 * 1024

F32 = jnp.float32
BF16 = jnp.bfloat16
_NT = (((1,), (1,)), ((), ()))
_LOG2E = 1.4426950408889634


def _const_spec(shape):
    nd = len(shape)
    return pl.BlockSpec(shape, lambda *_: (0,) * nd, pipeline_mode=pl.Buffered(1))


def _params(n_axes):
    return pltpu.CompilerParams(
        dimension_semantics=("arbitrary",) * n_axes, vmem_limit_bytes=VMEM_LIMIT)


def _rms(x, g):
    ms = jnp.sum(x * x, axis=-1, keepdims=True) * (1.0 / x.shape[-1])
    return (x * lax.rsqrt(ms + EPS)) * g


def _headnorm(p, g, hd):
    rows, n = p.shape
    lane = lax.broadcasted_iota(jnp.int32, (rows, LANES), 1)
    outs = []
    for j in range(n // LANES):
        c = p[:, j * LANES:(j + 1) * LANES]
        c2 = c * c
        if hd == LANES:
            ms = jnp.sum(c2, axis=1, keepdims=True) * (1.0 / LANES)
        else:
            lo = jnp.sum(jnp.where(lane < hd, c2, 0.0), axis=1, keepdims=True)
            hi = jnp.sum(jnp.where(lane >= hd, c2, 0.0), axis=1, keepdims=True)
            ms = jnp.where(lane < hd, lo, hi) * (1.0 / hd)
        outs.append(c * lax.rsqrt(ms + EPS))
    return jnp.concatenate(outs, axis=1) * g


def _log_sigmoid(x):
    return jnp.minimum(x, 0.0) - jnp.log(1.0 + jnp.exp(-jnp.abs(x)))


def _proj_kernel(x_ref, g_ref, w_ref, bf_ref, gq_ref, gk_ref, gm_ref,
                 sbq_ref, sbk_ref, sbv_ref, fxq_ref, fxk_ref, fxv_ref, mq_ref, lf_ref):
    h = _rms(x_ref[...], g_ref[...]).astype(BF16)

    def mm(c0, n):
        return jnp.dot(h, w_ref[:, c0:c0 + n], preferred_element_type=F32)

    sbq_ref[...] = (mm(0, D_SB) * HD ** -0.5).astype(BF16)
    sbk_ref[...] = mm(D_SB, D_SB).astype(BF16)
    sbv_ref[...] = mm(2 * D_SB, D_SB).astype(BF16)
    c0 = 3 * D_SB
    fxq_ref[...] = (_headnorm(mm(c0, D_FOX), gq_ref[...], HD) * HD ** -0.5).astype(BF16)
    fxk_ref[...] = _headnorm(mm(c0 + D_FOX, D_FOX), gk_ref[...], HD).astype(BF16)
    fxv_ref[...] = mm(c0 + 2 * D_FOX, D_FOX).astype(BF16)
    c0 += 3 * D_FOX
    mq_ref[...] = _headnorm(mm(c0, D_MEM), gm_ref[...], MEM_HD).astype(BF16)
    lf_ref[...] = _log_sigmoid(mm(c0 + D_MEM, LANES) + bf_ref[...])


def _proj(x2, g, w, bf, gq, gk, gm):
    t = x2.shape[0]
    n_w = w.shape[1]
    row = lambda n: pl.BlockSpec((ROW_TILE, n), lambda i: (i, 0))
    out_shapes = [jax.ShapeDtypeStruct((t, D_SB), BF16)] * 7 + [jax.ShapeDtypeStruct((t, LANES), F32)]
    return pl.pallas_call(
        _proj_kernel,
        grid=(t // ROW_TILE,),
        in_specs=[row(D_MODEL), _const_spec((1, D_MODEL)), _const_spec((D_MODEL, n_w)),
                  _const_spec((1, LANES)), _const_spec((1, D_FOX)), _const_spec((1, D_FOX)),
                  _const_spec((1, D_MEM))],
        out_specs=[row(D_SB)] * 7 + [row(LANES)],
        out_shape=out_shapes,
        compiler_params=_params(1),
        name="proj",
    )(x2, g, w, bf, gq, gk, gm)


def _forget_kernel(lf_ref, p_ref, oq_ref, ok_ref, aq_ref, ak_ref):
    f = lf_ref[0]
    s_len = f.shape[0]
    row = lax.broadcasted_iota(jnp.int32, f.shape, 0)
    sh = 1
    while sh < s_len:
        f = f + jnp.where(row >= sh, pltpu.roll(f, sh, 0), 0.0)
        sh *= 2
    f_hi = f.astype(BF16)
    r1 = f - f_hi.astype(F32)
    f_mid = r1.astype(BF16)
    f_lo = (r1 - f_mid.astype(F32)).astype(BF16)
    pieces = jnp.concatenate([f_hi, f_mid, f_lo], axis=1)
    placed = jnp.dot(pieces, p_ref[...], preferred_element_type=F32)
    aq_ref[0] = (placed[:, :LANES] + oq_ref[...]).astype(BF16)
    ak_ref[0] = (placed[:, LANES:] + ok_ref[...]).astype(BF16)


def _forget_constants():
    p = np.zeros((3 * LANES, 2 * LANES), np.float32)
    oq = np.zeros((1, LANES), np.float32)
    ok = np.zeros((1, LANES), np.float32)
    for h in range(FOX_HEADS):
        for j in range(3):
            p[j * LANES + h, h * AUG_STRIDE + j] = 1.0
            p[j * LANES + h, LANES + h * AUG_STRIDE + 3 + j] = -1.0
            oq[0, h * AUG_STRIDE + 3 + j] = 1.0
            ok[0, h * AUG_STRIDE + j] = 1.0
    return jnp.asarray(p, BF16), jnp.asarray(oq), jnp.asarray(ok)


def _forget(lf3):
    b, s, _ = lf3.shape
    p, oq, ok = _forget_constants()
    blk = pl.BlockSpec((1, s, LANES), lambda i: (i, 0, 0))
    return pl.pallas_call(
        _forget_kernel,
        grid=(b,),
        in_specs=[blk, _const_spec(p.shape), _const_spec(oq.shape), _const_spec(ok.shape)],
        out_specs=[blk, blk],
        out_shape=[jax.ShapeDtypeStruct((b, s, LANES), BF16)] * 2,
        compiler_params=_params(1),
        name="forget",
    )(lf3, p, oq, ok)


def _stack_heads(x, lane):
    zero = jnp.zeros_like(x)
    return jnp.concatenate([jnp.where(lane < HD, x, zero), jnp.where(lane >= HD, x, zero)], axis=0)


def _unstack_heads(x, lane):
    rows = x.shape[0] // 2
    return jnp.where(lane < HD, x[:rows], x[rows:])


def _softplus(z):
    return jnp.maximum(z, 0.0) + jnp.log(1.0 + jnp.exp2(jnp.abs(z) * (-_LOG2E)))


def _split_bf16(x):
    hi = x.astype(BF16)
    return hi, (x - hi.astype(F32)).astype(BF16)


def _sb_kernel(q_ref, k_ref, v_ref, m_ref, o_ref, flag_ref, acc_ref, carry_ref):
    s_len = q_ref.shape[1]
    t = ATT_K_TILE
    nb = s_len // t
    lane = lax.broadcasted_iota(jnp.int32, (t, LANES), 1)
    tri = (lax.broadcasted_iota(jnp.int32, (2 * t, t), 1)
           < (lax.broadcasted_iota(jnp.int32, (2 * t, t), 0) & (t - 1)))
    msum = m_ref[...]
    reps = t // LANES
    order = list(reversed(range(nb)))
    z, sp, c, acc_diag, carry_diag = {}, {}, {}, {}, {}

    def qblock(j):
        return _stack_heads(q_ref[0, j * t:(j + 1) * t, :], lane)

    def scores(g):
        j = order[g]
        lhs = qblock(j) if j == nb - 1 else jnp.concatenate([qblock(j), qblock(j + 1)], axis=0)
        z[g] = lax.dot_general(lhs, k_ref[0, j * t:(j + 1) * t, :], _NT, preferred_element_type=F32)

    def suffix_sums(g):
        s = _softplus(z[g])
        top = jnp.where(tri, s[:2 * t], 0.0)
        s = top if order[g] == nb - 1 else jnp.concatenate([top, s[2 * t:]], axis=0)
        hi, lo = _split_bf16(s)
        sp[g] = s
        c[g] = (jnp.dot(hi, msum, preferred_element_type=F32)
                + jnp.dot(lo, msum, preferred_element_type=F32))

    def weights_pv(g):
        j = order[g]
        zc = z.pop(g) - c.pop(g)
        w = jnp.where(tri, jnp.exp(zc[:2 * t]), 0.0)
        if j < nb - 1:
            w_prev = jnp.exp(zc[2 * t:] - jnp.concatenate([carry_diag[j + 1]] * reps, axis=1))
            w = jnp.concatenate([w, w_prev], axis=0)
        pv = jnp.dot(w.astype(BF16), v_ref[0, j * t:(j + 1) * t, :], preferred_element_type=F32)
        rs = jnp.sum(sp.pop(g), axis=1, keepdims=True)
        if j < nb - 1:
            done = acc_diag.pop(j + 1) + pv[2 * t:]
            o_ref[0, (j + 1) * t:(j + 2) * t, :] = _unstack_heads(done, lane).astype(BF16)
            total = carry_diag.pop(j + 1) + rs[2 * t:]
            if j >= 1:
                flag_ref[j + 1] = jnp.min(total)
        if j == 0:
            o_ref[0, 0:t, :] = _unstack_heads(pv[:2 * t], lane).astype(BF16)
        else:
            acc_diag[j] = pv[:2 * t]
            carry_diag[j] = jnp.broadcast_to(rs[:2 * t], (2 * t, LANES))

    for step in range(nb + 2):
        if step < nb:
            scores(step)
        if 1 <= step <= nb:
            suffix_sums(step - 1)
        if step >= 2:
            weights_pv(step - 2)

    def redo(jb, _):
        @pl.when(flag_ref[jb] < SB_EXIT)
        def _():
            rows = pl.ds(pl.multiple_of(jb * t, t), t)
            qb = _stack_heads(q_ref[0, rows, :], lane)
            acc_ref[...] = jnp.zeros_like(acc_ref)
            carry_ref[...] = jnp.zeros_like(carry_ref)

            def tile(kb, diag):
                keys = pl.ds(pl.multiple_of(kb * t, t), t)
                zz = lax.dot_general(qb, k_ref[0, keys, :], _NT, preferred_element_type=F32)
                s = _softplus(zz)
                if diag:
                    s = jnp.where(tri, s, 0.0)
                hi, lo = _split_bf16(s)
                cc = (jnp.dot(hi, msum, preferred_element_type=F32)
                      + jnp.dot(lo, msum, preferred_element_type=F32))
                w = jnp.exp(zz - cc - jnp.concatenate([carry_ref[...]] * reps, axis=1))
                if diag:
                    w = jnp.where(tri, w, 0.0)
                acc_ref[...] += jnp.dot(w.astype(BF16), v_ref[0, keys, :], preferred_element_type=F32)
                carry_ref[...] += jnp.sum(s, axis=1, keepdims=True)

            tile(jb, True)

            def body(i, _):
                tile(jb - 1 - i, False)
                return 0

            lax.fori_loop(0, jb, body, 0)
            o_ref[0, rows, :] = _unstack_heads(acc_ref[...], lane).astype(BF16)
        return 0

    lax.fori_loop(2, nb, redo, 0)


def _sb_attention(q, k, v):
    b, s, d = q.shape
    t = ATT_K_TILE
    ii = np.arange(t)
    msum = jnp.asarray(ii[:, None] >= ii[None, :], BF16)
    spec = pl.BlockSpec((1, s, LANES), lambda bi, hp: (bi, 0, hp))
    return pl.pallas_call(
        _sb_kernel,
        grid=(b, d // LANES),
        in_specs=[spec, spec, spec, _const_spec((t, t))],
        out_specs=spec,
        out_shape=jax.ShapeDtypeStruct((b, s, d), BF16),
        scratch_shapes=[pltpu.SMEM((s // t,), F32), pltpu.VMEM((2 * t, LANES), F32),
                        pltpu.VMEM((2 * t, LANES), F32)],
        compiler_params=_params(2),
        name="sb_attn",
    )(q, k, v, msum)


def _fox_kernel(q_ref, aq_ref, k_ref, ak_ref, v_ref, o_ref):
    s_len = q_ref.shape[1]
    t = ATT_Q_TILE
    hp = pl.program_id(1)
    lane = lax.broadcasted_iota(jnp.int32, (t, LANES), 1)
    group = lane // AUG_STRIDE
    causal = (lax.broadcasted_iota(jnp.int32, (2 * t, t), 1)
              <= (lax.broadcasted_iota(jnp.int32, (2 * t, t), 0) & (t - 1)))
    ones = jnp.ones((t, LANES), BF16)
    tiles = [(qi, kb) for qi in range(s_len // t) for kb in range(qi + 1)]
    qext, kext, vext, z, m, acc = {}, {}, {}, {}, {}, {}

    def scores(n):
        qi, kb = tiles[n]
        rows = slice(qi * t, (qi + 1) * t)
        keys = slice(kb * t, (kb + 1) * t)
        if qi not in qext:
            aq = aq_ref[0, rows, :]
            zero = jnp.zeros_like(aq)
            aug = jnp.concatenate([jnp.where(group == 2 * hp, aq, zero),
                                   jnp.where(group == 2 * hp + 1, aq, zero)], axis=0)
            qext[qi] = jnp.concatenate([_stack_heads(q_ref[0, rows, :], lane), aug], axis=1)
        if kb not in kext:
            kext[kb] = jnp.concatenate([k_ref[0, keys, :], ak_ref[0, keys, :]], axis=1)
            vext[kb] = jnp.concatenate([v_ref[0, keys, :], ones], axis=1)
        s = lax.dot_general(qext[qi], kext[kb], _NT, preferred_element_type=F32)
        z[n] = jnp.where(causal, s, NEG_INF) if kb == qi else s

    def softmax_pv(n):
        qi, kb = tiles[n]
        s = z.pop(n)
        row_max = jnp.max(s, axis=1, keepdims=True)
        if qi not in m:
            m_new = jnp.broadcast_to(row_max, (2 * t, LANES))
        else:
            m_new = jnp.maximum(m[qi], row_max)
        p = jnp.exp(s - jnp.concatenate([m_new] * (t // LANES), axis=1))
        pv = jnp.dot(p.astype(BF16), vext[kb], preferred_element_type=F32)
        if qi not in m:
            acc[qi] = pv
        else:
            alpha = jnp.exp(m[qi] - m_new)
            acc[qi] = jnp.concatenate([alpha, alpha], axis=1) * acc[qi] + pv
        m[qi] = m_new
        if kb == qi:
            a = acc.pop(qi)
            o_ref[0, qi * t:(qi + 1) * t, :] = _unstack_heads(a[:, :LANES] / a[:, LANES:], lane).astype(BF16)
            m.pop(qi)
            qext.pop(qi)

    n_tiles = len(tiles)
    for step in range(n_tiles + 1):
        if step < n_tiles:
            scores(step)
        if step >= 1:
            softmax_pv(step - 1)


def _fox_attention(q, k, v, aq, ak):
    b, s, d = q.shape
    spec = pl.BlockSpec((1, s, LANES), lambda bi, hp: (bi, 0, hp))
    aspec = pl.BlockSpec((1, s, LANES), lambda bi, hp: (bi, 0, 0))
    return pl.pallas_call(
        _fox_kernel,
        grid=(b, d // LANES),
        in_specs=[spec, aspec, spec, aspec, spec],
        out_specs=spec,
        out_shape=jax.ShapeDtypeStruct((b, s, d), BF16),
        compiler_params=_params(2),
        name="fox_attn",
    )(q, aq, k, ak, v)


def _mem_kv_kernel(mem_ref, g_ref, w_ref, gk_ref, mk_ref, mv_ref):
    mh = _rms(mem_ref[0], g_ref[...]).astype(BF16)
    kv = jnp.dot(mh, w_ref[...], preferred_element_type=F32)
    mk_ref[0] = _headnorm(kv[:, :D_MEM], gk_ref[...], MEM_HD).astype(BF16)
    mv_ref[0] = kv[:, D_MEM:].astype(BF16)


def _mem_kv(mem, g, w, gk):
    b, m, _ = mem.shape
    out = pl.BlockSpec((1, m, D_MEM), lambda i: (i, 0, 0))
    return pl.pallas_call(
        _mem_kv_kernel,
        grid=(b,),
        in_specs=[pl.BlockSpec((1, m, D_MODEL), lambda i: (i, 0, 0)), _const_spec((1, D_MODEL)),
                  _const_spec((D_MODEL, 2 * D_MEM)), _const_spec((1, D_MEM))],
        out_specs=[out, out],
        out_shape=[jax.ShapeDtypeStruct((b, m, D_MEM), BF16)] * 2,
        compiler_params=_params(1),
        name="mem_kv",
    )(mem, g, w, gk)


def _mem_attn_kernel(q_ref, k_ref, v_ref, o_ref):
    for h in range(MEM_HEADS):
        sl = slice(h * MEM_HD, (h + 1) * MEM_HD)
        z = lax.dot_general(q_ref[0, :, sl], k_ref[0, :, sl], _NT,
                            preferred_element_type=F32) * MEM_HD ** -0.5
        e = jnp.exp(z - jnp.max(z, axis=1, keepdims=True))
        p = e / jnp.sum(e, axis=1, keepdims=True)
        o_ref[0, :, sl] = jnp.dot(p.astype(BF16), v_ref[0, :, sl],
                                  preferred_element_type=F32).astype(BF16)


def _mem_attention(q, k, v):
    b, s, d = q.shape
    m = k.shape[1]
    qspec = pl.BlockSpec((1, MEM_Q_TILE, d), lambda bi, qi: (bi, qi, 0))
    kvspec = pl.BlockSpec((1, m, d), lambda bi, qi: (bi, 0, 0))
    return pl.pallas_call(
        _mem_attn_kernel,
        grid=(b, s // MEM_Q_TILE),
        in_specs=[qspec, kvspec, kvspec],
        out_specs=qspec,
        out_shape=jax.ShapeDtypeStruct((b, s, d), BF16),
        compiler_params=_params(2),
        name="mem_attn",
    )(q, k, v)


def _merge_kernel(x_ref, g_ref, wg_ref, osb_ref, ofx_ref, omem_ref,
                  wsb_ref, wfx_ref, wmem_ref, wout_ref, o_ref):
    x = x_ref[...]
    h = _rms(x, g_ref[...]).astype(BF16)
    merged = None
    for i, (o_br, w_br) in enumerate(((osb_ref, wsb_ref), (ofx_ref, wfx_ref), (omem_ref, wmem_ref))):
        gate = jax.nn.sigmoid(jnp.dot(h, wg_ref[:, i * D_MODEL:(i + 1) * D_MODEL],
                                      preferred_element_type=F32))
        term = gate * jnp.dot(o_br[...], w_br[...], preferred_element_type=F32)
        merged = term if merged is None else merged + term
    o_ref[...] = x + jnp.dot(merged.astype(BF16), wout_ref[...], preferred_element_type=F32)


def _merge(x2, g, wg, osb, ofx, omem, wsb, wfx, wmem, wout):
    t = x2.shape[0]
    row = lambda n: pl.BlockSpec((ROW_TILE, n), lambda i: (i, 0))
    return pl.pallas_call(
        _merge_kernel,
        grid=(t // ROW_TILE,),
        in_specs=[row(D_MODEL), _const_spec((1, D_MODEL)), _const_spec(wg.shape),
                  row(D_SB), row(D_FOX), row(D_MEM),
                  _const_spec(wsb.shape), _const_spec(wfx.shape), _const_spec(wmem.shape),
                  _const_spec(wout.shape)],
        out_specs=row(D_MODEL),
        out_shape=jax.ShapeDtypeStruct((t, D_MODEL), F32),
        compiler_params=_params(1),
        name="merge",
    )(x2, g, wg, osb, ofx, omem, wsb, wfx, wmem, wout)


def _mlp_kernel(x_ref, g_ref, wup_ref, wdn_ref, o_ref):
    x = x_ref[...]
    h = _rms(x, g_ref[...]).astype(BF16)
    acc = x
    for c in range(D_FF // FF_CHUNK):
        sl = slice(c * FF_CHUNK, (c + 1) * FF_CHUNK)
        u = jnp.maximum(jnp.dot(h, wup_ref[:, sl], preferred_element_type=F32), 0.0)
        acc = acc + jnp.dot((u * u).astype(BF16), wdn_ref[sl, :], preferred_element_type=F32)
    o_ref[...] = acc


def _mlp(x2, g, wup, wdn):
    t = x2.shape[0]
    row = pl.BlockSpec((ROW_TILE, D_MODEL), lambda i: (i, 0))
    return pl.pallas_call(
        _mlp_kernel,
        grid=(t // ROW_TILE,),
        in_specs=[row, _const_spec((1, D_MODEL)), _const_spec(wup.shape), _const_spec(wdn.shape)],
        out_specs=row,
        out_shape=jax.ShapeDtypeStruct((t, D_MODEL), F32),
        compiler_params=_params(1),
        name="mlp",
    )(x2, g, wup, wdn)


def _layer(x, mem, g_mix, g_memn, w_in, b_forget, g_fq, g_fk, g_mq, g_mk, w_mem_kv,
           w_sb, w_fox, w_mem, w_out, g_mlp, w_up, w_dn):
    b, s, _ = x.shape
    t = b * s
    x2 = x.reshape(t, D_MODEL)
    n_qkv = 3 * D_SB + 3 * D_FOX
    w_f = jnp.pad(w_in[:, n_qkv:n_qkv + FOX_HEADS], ((0, 0), (0, LANES - FOX_HEADS)))
    c_mq = n_qkv + FOX_HEADS
    w_proj = jnp.concatenate([w_in[:, :n_qkv], w_in[:, c_mq:c_mq + D_MEM], w_f], axis=1).astype(BF16)
    w_gate = w_in[:, c_mq + D_MEM:].astype(BF16)
    bf = jnp.pad(b_forget, (0, LANES - FOX_HEADS)).reshape(1, LANES)
    row = lambda a: a.reshape(1, -1)

    sbq, sbk, sbv, fxq, fxk, fxv, mq, lf = _proj(
        x2, row(g_mix), w_proj, bf, row(jnp.tile(g_fq, FOX_HEADS)), row(jnp.tile(g_fk, FOX_HEADS)),
        row(jnp.tile(g_mq, MEM_HEADS)))
    to3 = lambda a: a.reshape(b, s, a.shape[-1])

    aq, ak = _forget(to3(lf))
    o_sb = _sb_attention(to3(sbq), to3(sbk), to3(sbv))
    o_fox = _fox_attention(to3(fxq), to3(fxk), to3(fxv), aq, ak)
    mk, mv = _mem_kv(mem, row(g_memn), w_mem_kv.astype(BF16), row(jnp.tile(g_mk, MEM_HEADS)))
    o_mem = _mem_attention(to3(mq), mk, mv)

    x1 = _merge(x2, row(g_mix), w_gate, o_sb.reshape(t, D_SB), o_fox.reshape(t, D_FOX),
                o_mem.reshape(t, D_MEM), w_sb.astype(BF16), w_fox.astype(BF16), w_mem.astype(BF16),
                w_out.astype(BF16))
    out = _mlp(x1, row(g_mlp), w_up.astype(BF16), w_dn.astype(BF16))
    return out.reshape(b, s, D_MODEL)


def kernel(x, mem, g_mix_norm, g_mem_norm, w_in, b_forget, g_fox_q, g_fox_k, g_mem_q, g_mem_k,
           w_mem_kv, w_branch_sb, w_branch_fox, w_branch_mem, w_out, g_mlp_norm, w_ff_up, w_ff_down):
    for l in range(w_in.shape[0]):
        x = _layer(x, mem, g_mix_norm[l], g_mem_norm[l], w_in[l], b_forget[l], g_fox_q[l], g_fox_k[l],
                   g_mem_q[l], g_mem_k[l], w_mem_kv[l], w_branch_sb[l], w_branch_fox[l],
                   w_branch_mem[l], w_out[l], g_mlp_norm[l], w_ff_up[l], w_ff_down[l])
    return x
```

```python
import numpy as np
import jax
import jax.numpy as jnp
from jax import lax
from jax.experimental import pallas as pl
from jax.experimental.pallas import tpu as pltpu

D_MODEL = 1024
HD = 64
SB_HEADS = 8
FOX_HEADS = 8
MEM_HEADS = 4
MEM_HD = 128
D_SB = SB_HEADS * HD
D_FOX = FOX_HEADS * HD
D_MEM = MEM_HEADS * MEM_HD
N_BRANCH = 3
D_FF = 4 * D_MODEL
EPS = 1e-6
NEG_INF = -1e30

LANES = 128
AUG_STRIDE = 8
ROW_TILE = 512
ATT_Q_TILE = 512
ATT_K_TILE = 256
SB_EXIT = 110.0
MEM_Q_TILE = 512
FF_CHUNK = 512
VMEM_LIMIT = 56 * 1024 * 1024

F32 = jnp.float32
BF16 = jnp.bfloat16
_NT = (((1,), (1,)), ((), ()))
_LOG2E = 1.4426950408889634


def _const_spec(shape):
    nd = len(shape)
    return pl.BlockSpec(shape, lambda *_: (0,) * nd, pipeline_mode=pl.Buffered(1))


def _params(n_axes):
    return pltpu.CompilerParams(
        dimension_semantics=("arbitrary",) * n_axes, vmem_limit_bytes=VMEM_LIMIT)


def _rms(x, g):
    ms = jnp.sum(x * x, axis=-1, keepdims=True) * (1.0 / x.shape[-1])
    return (x * lax.rsqrt(ms + EPS)) * g


def _headnorm(p, g, hd):
    rows, n = p.shape
    lane = lax.broadcasted_iota(jnp.int32, (rows, LANES), 1)
    outs = []
    for j in range(n // LANES):
        c = p[:, j * LANES:(j + 1) * LANES]
        c2 = c * c
        if hd == LANES:
            ms = jnp.sum(c2, axis=1, keepdims=True) * (1.0 / LANES)
        else:
            lo = jnp.sum(jnp.where(lane < hd, c2, 0.0), axis=1, keepdims=True)
            hi = jnp.sum(jnp.where(lane >= hd, c2, 0.0), axis=1, keepdims=True)
            ms = jnp.where(lane < hd, lo, hi) * (1.0 / hd)
        outs.append(c * lax.rsqrt(ms + EPS))
    return jnp.concatenate(outs, axis=1) * g


def _log_sigmoid(x):
    return jnp.minimum(x, 0.0) - jnp.log(1.0 + jnp.exp(-jnp.abs(x)))


def _mm(a, w_ref, rows=slice(None), cols=slice(None)):
    return jnp.dot(a, w_ref[rows, cols].astype(BF16), preferred_element_type=F32)


def _proj_kernel(x_ref, g_ref, w_ref, wmq_ref, wf_ref, bf_ref, gq_ref, gk_ref, gm_ref,
                 sbq_ref, sbk_ref, sbv_ref, fxq_ref, fxk_ref, fxv_ref, mq_ref, lf_ref):
    h = _rms(x_ref[...], g_ref[...]).astype(BF16)

    def mm(c0, n):
        return _mm(h, w_ref, cols=slice(c0, c0 + n))

    sbq_ref[...] = (mm(0, D_SB) * HD ** -0.5).astype(BF16)
    sbk_ref[...] = mm(D_SB, D_SB).astype(BF16)
    sbv_ref[...] = mm(2 * D_SB, D_SB).astype(BF16)
    c0 = 3 * D_SB
    fxq_ref[...] = (_headnorm(mm(c0, D_FOX), gq_ref[...], HD) * HD ** -0.5).astype(BF16)
    fxk_ref[...] = _headnorm(mm(c0 + D_FOX, D_FOX), gk_ref[...], HD).astype(BF16)
    fxv_ref[...] = mm(c0 + 2 * D_FOX, D_FOX).astype(BF16)
    mq_ref[...] = _headnorm(_mm(h, wmq_ref), gm_ref[...], MEM_HD).astype(BF16)
    lf_ref[...] = _log_sigmoid(_mm(h, wf_ref) + bf_ref[...])


def _proj(x2, g, w_in, w_mq, w_f, bf, gq, gk, gm):
    t = x2.shape[0]
    n_qkv = 3 * D_SB + 3 * D_FOX
    row = lambda n: pl.BlockSpec((ROW_TILE, n), lambda i: (i, 0))
    out_shapes = [jax.ShapeDtypeStruct((t, D_SB), BF16)] * 7 + [jax.ShapeDtypeStruct((t, LANES), F32)]
    return pl.pallas_call(
        _proj_kernel,
        grid=(t // ROW_TILE,),
        in_specs=[row(D_MODEL), _const_spec((1, D_MODEL)), _const_spec((D_MODEL, n_qkv)),
                  _const_spec(w_mq.shape), _const_spec(w_f.shape),
                  _const_spec((1, LANES)), _const_spec((1, D_FOX)), _const_spec((1, D_FOX)),
                  _const_spec((1, D_MEM))],
        out_specs=[row(D_SB)] * 7 + [row(LANES)],
        out_shape=out_shapes,
        compiler_params=_params(1),
        name="proj",
    )(x2, g, w_in, w_mq, w_f, bf, gq, gk, gm)


def _forget_kernel(lf_ref, p_ref, oq_ref, ok_ref, aq_ref, ak_ref):
    f = lf_ref[0]
    s_len = f.shape[0]
    row = lax.broadcasted_iota(jnp.int32, f.shape, 0)
    sh = 1
    while sh < s_len:
        f = f + jnp.where(row >= sh, pltpu.roll(f, sh, 0), 0.0)
        sh *= 2
    f_hi = f.astype(BF16)
    r1 = f - f_hi.astype(F32)
    f_mid = r1.astype(BF16)
    f_lo = (r1 - f_mid.astype(F32)).astype(BF16)
    pieces = jnp.concatenate([f_hi, f_mid, f_lo], axis=1)
    placed = jnp.dot(pieces, p_ref[...], preferred_element_type=F32)
    aq_ref[0] = (placed[:, :LANES] + oq_ref[...]).astype(BF16)
    ak_ref[0] = (placed[:, LANES:] + ok_ref[...]).astype(BF16)


def _forget_constants():
    p = np.zeros((3 * LANES, 2 * LANES), np.float32)
    oq = np.zeros((1, LANES), np.float32)
    ok = np.zeros((1, LANES), np.float32)
    for h in range(FOX_HEADS):
        for j in range(3):
            p[j * LANES + h, h * AUG_STRIDE + j] = 1.0
            p[j * LANES + h, LANES + h * AUG_STRIDE + 3 + j] = -1.0
            oq[0, h * AUG_STRIDE + 3 + j] = 1.0
            ok[0, h * AUG_STRIDE + j] = 1.0
    return jnp.asarray(p, BF16), jnp.asarray(oq), jnp.asarray(ok)


def _forget(lf3):
    b, s, _ = lf3.shape
    p, oq, ok = _forget_constants()
    blk = pl.BlockSpec((1, s, LANES), lambda i: (i, 0, 0))
    return pl.pallas_call(
        _forget_kernel,
        grid=(b,),
        in_specs=[blk, _const_spec(p.shape), _const_spec(oq.shape), _const_spec(ok.shape)],
        out_specs=[blk, blk],
        out_shape=[jax.ShapeDtypeStruct((b, s, LANES), BF16)] * 2,
        compiler_params=_params(1),
        name="forget",
    )(lf3, p, oq, ok)


def _stack_heads(x, lane):
    zero = jnp.zeros_like(x)
    return jnp.concatenate([jnp.where(lane < HD, x, zero), jnp.where(lane >= HD, x, zero)], axis=0)


def _unstack_heads(x, lane):
    rows = x.shape[0] // 2
    return jnp.where(lane < HD, x[:rows], x[rows:])


def _softplus(z):
    return jnp.maximum(z, 0.0) + jnp.log(1.0 + jnp.exp2(jnp.abs(z) * (-_LOG2E)))


def _split_bf16(x):
    hi = x.astype(BF16)
    return hi, (x - hi.astype(F32)).astype(BF16)


def _sb_kernel(q_ref, k_ref, v_ref, m_ref, o_ref, flag_ref, acc_ref, carry_ref):
    s_len = q_ref.shape[1]
    t = ATT_K_TILE
    nb = s_len // t
    lane = lax.broadcasted_iota(jnp.int32, (t, LANES), 1)
    tri = (lax.broadcasted_iota(jnp.int32, (2 * t, t), 1)
           < (lax.broadcasted_iota(jnp.int32, (2 * t, t), 0) & (t - 1)))
    msum = m_ref[...]
    reps = t // LANES
    order = list(reversed(range(nb)))
    z, sp, c, acc_diag, carry_diag = {}, {}, {}, {}, {}

    def qblock(j):
        return _stack_heads(q_ref[0, j * t:(j + 1) * t, :], lane)

    def scores(g):
        j = order[g]
        lhs = qblock(j) if j == nb - 1 else jnp.concatenate([qblock(j), qblock(j + 1)], axis=0)
        z[g] = lax.dot_general(lhs, k_ref[0, j * t:(j + 1) * t, :], _NT, preferred_element_type=F32)

    def suffix_sums(g):
        s = _softplus(z[g])
        top = jnp.where(tri, s[:2 * t], 0.0)
        s = top if order[g] == nb - 1 else jnp.concatenate([top, s[2 * t:]], axis=0)
        hi, lo = _split_bf16(s)
        sp[g] = s
        c[g] = (jnp.dot(hi, msum, preferred_element_type=F32)
                + jnp.dot(lo, msum, preferred_element_type=F32))

    def weights_pv(g):
        j = order[g]
        zc = z.pop(g) - c.pop(g)
        w = jnp.where(tri, jnp.exp(zc[:2 * t]), 0.0)
        if j < nb - 1:
            w_prev = jnp.exp(zc[2 * t:] - jnp.concatenate([carry_diag[j + 1]] * reps, axis=1))
            w = jnp.concatenate([w, w_prev], axis=0)
        pv = jnp.dot(w.astype(BF16), v_ref[0, j * t:(j + 1) * t, :], preferred_element_type=F32)
        rs = jnp.sum(sp.pop(g), axis=1, keepdims=True)
        if j < nb - 1:
            done = acc_diag.pop(j + 1) + pv[2 * t:]
            o_ref[0, (j + 1) * t:(j + 2) * t, :] = _unstack_heads(done, lane).astype(BF16)
            total = carry_diag.pop(j + 1) + rs[2 * t:]
            if j >= 1:
                flag_ref[j + 1] = jnp.min(total)
        if j == 0:
            o_ref[0, 0:t, :] = _unstack_heads(pv[:2 * t], lane).astype(BF16)
        else:
            acc_diag[j] = pv[:2 * t]
            carry_diag[j] = jnp.broadcast_to(rs[:2 * t], (2 * t, LANES))

    for step in range(nb + 2):
        if step < nb:
            scores(step)
        if 1 <= step <= nb:
            suffix_sums(step - 1)
        if step >= 2:
            weights_pv(step - 2)

    def redo(jb, _):
        @pl.when(flag_ref[jb] < SB_EXIT)
        def _():
            rows = pl.ds(pl.multiple_of(jb * t, t), t)
            qb = _stack_heads(q_ref[0, rows, :], lane)
            acc_ref[...] = jnp.zeros_like(acc_ref)
            carry_ref[...] = jnp.zeros_like(carry_ref)

            def tile(kb, diag):
                keys = pl.ds(pl.multiple_of(kb * t, t), t)
                zz = lax.dot_general(qb, k_ref[0, keys, :], _NT, preferred_element_type=F32)
                s = _softplus(zz)
                if diag:
                    s = jnp.where(tri, s, 0.0)
                hi, lo = _split_bf16(s)
                cc = (jnp.dot(hi, msum, preferred_element_type=F32)
                      + jnp.dot(lo, msum, preferred_element_type=F32))
                w = jnp.exp(zz - cc - jnp.concatenate([carry_ref[...]] * reps, axis=1))
                if diag:
                    w = jnp.where(tri, w, 0.0)
                acc_ref[...] += jnp.dot(w.astype(BF16), v_ref[0, keys, :], preferred_element_type=F32)
                carry_ref[...] += jnp.sum(s, axis=1, keepdims=True)

            tile(jb, True)

            def body(i, _):
                tile(jb - 1 - i, False)
                return 0

            lax.fori_loop(0, jb, body, 0)
            o_ref[0, rows, :] = _unstack_heads(acc_ref[...], lane).astype(BF16)
        return 0

    lax.fori_loop(2, nb, redo, 0)


def _sb_attention(q, k, v):
    b, s, d = q.shape
    t = ATT_K_TILE
    ii = np.arange(t)
    msum = jnp.asarray(ii[:, None] >= ii[None, :], BF16)
    spec = pl.BlockSpec((1, s, LANES), lambda bi, hp: (bi, 0, hp))
    return pl.pallas_call(
        _sb_kernel,
        grid=(b, d // LANES),
        in_specs=[spec, spec, spec, _const_spec((t, t))],
        out_specs=spec,
        out_shape=jax.ShapeDtypeStruct((b, s, d), BF16),
        scratch_shapes=[pltpu.SMEM((s // t,), F32), pltpu.VMEM((2 * t, LANES), F32),
                        pltpu.VMEM((2 * t, LANES), F32)],
        compiler_params=_params(2),
        name="sb_attn",
    )(q, k, v, msum)


def _fox_kernel(q_ref, aq_ref, k_ref, ak_ref, v_ref, o_ref):
    s_len = q_ref.shape[1]
    t = ATT_Q_TILE
    hp = pl.program_id(1)
    lane = lax.broadcasted_iota(jnp.int32, (t, LANES), 1)
    group = lane // AUG_STRIDE
    causal = (lax.broadcasted_iota(jnp.int32, (2 * t, t), 1)
              <= (lax.broadcasted_iota(jnp.int32, (2 * t, t), 0) & (t - 1)))
    ones = jnp.ones((t, LANES), BF16)
    tiles = [(qi, kb) for qi in range(s_len // t) for kb in range(qi + 1)]
    qext, kext, vext, z, m, acc = {}, {}, {}, {}, {}, {}

    def scores(n):
        qi, kb = tiles[n]
        rows = slice(qi * t, (qi + 1) * t)
        keys = slice(kb * t, (kb + 1) * t)
        if qi not in qext:
            aq = aq_ref[0, rows, :]
            zero = jnp.zeros_like(aq)
            aug = jnp.concatenate([jnp.where(group == 2 * hp, aq, zero),
                                   jnp.where(group == 2 * hp + 1, aq, zero)], axis=0)
            qext[qi] = jnp.concatenate([_stack_heads(q_ref[0, rows, :], lane), aug], axis=1)
        if kb not in kext:
            kext[kb] = jnp.concatenate([k_ref[0, keys, :], ak_ref[0, keys, :]], axis=1)
            vext[kb] = jnp.concatenate([v_ref[0, keys, :], ones], axis=1)
        s = lax.dot_general(qext[qi], kext[kb], _NT, preferred_element_type=F32)
        z[n] = jnp.where(causal, s, NEG_INF) if kb == qi else s

    def softmax_pv(n):
        qi, kb = tiles[n]
        s = z.pop(n)
        row_max = jnp.max(s, axis=1, keepdims=True)
        if qi not in m:
            m_new = jnp.broadcast_to(row_max, (2 * t, LANES))
        else:
            m_new = jnp.maximum(m[qi], row_max)
        p = jnp.exp(s - jnp.concatenate([m_new] * (t // LANES), axis=1))
        pv = jnp.dot(p.astype(BF16), vext[kb], preferred_element_type=F32)
        if qi not in m:
            acc[qi] = pv
        else:
            alpha = jnp.exp(m[qi] - m_new)
            acc[qi] = jnp.concatenate([alpha, alpha], axis=1) * acc[qi] + pv
        m[qi] = m_new
        if kb == qi:
            a = acc.pop(qi)
            o_ref[0, qi * t:(qi + 1) * t, :] = _unstack_heads(a[:, :LANES] / a[:, LANES:], lane).astype(BF16)
            m.pop(qi)
            qext.pop(qi)

    n_tiles = len(tiles)
    for step in range(n_tiles + 1):
        if step < n_tiles:
            scores(step)
        if step >= 1:
            softmax_pv(step - 1)


def _fox_attention(q, k, v, aq, ak):
    b, s, d = q.shape
    spec = pl.BlockSpec((1, s, LANES), lambda bi, hp: (bi, 0, hp))
    aspec = pl.BlockSpec((1, s, LANES), lambda bi, hp: (bi, 0, 0))
    return pl.pallas_call(
        _fox_kernel,
        grid=(b, d // LANES),
        in_specs=[spec, aspec, spec, aspec, spec],
        out_specs=spec,
        out_shape=jax.ShapeDtypeStruct((b, s, d), BF16),
        compiler_params=_params(2),
        name="fox_attn",
    )(q, aq, k, ak, v)


def _mem_kv_kernel(mem_ref, g_ref, w_ref, gk_ref, mk_ref, mv_ref):
    mh = _rms(mem_ref[0], g_ref[...]).astype(BF16)
    kv = _mm(mh, w_ref)
    mk_ref[0] = _headnorm(kv[:, :D_MEM], gk_ref[...], MEM_HD).astype(BF16)
    mv_ref[0] = kv[:, D_MEM:].astype(BF16)


def _mem_kv(mem, g, w, gk):
    b, m, _ = mem.shape
    out = pl.BlockSpec((1, m, D_MEM), lambda i: (i, 0, 0))
    return pl.pallas_call(
        _mem_kv_kernel,
        grid=(b,),
        in_specs=[pl.BlockSpec((1, m, D_MODEL), lambda i: (i, 0, 0)), _const_spec((1, D_MODEL)),
                  _const_spec((D_MODEL, 2 * D_MEM)), _const_spec((1, D_MEM))],
        out_specs=[out, out],
        out_shape=[jax.ShapeDtypeStruct((b, m, D_MEM), BF16)] * 2,
        compiler_params=_params(1),
        name="mem_kv",
    )(mem, g, w, gk)


def _mem_attn_kernel(q_ref, k_ref, v_ref, o_ref):
    z = {}

    def scores(h):
        sl = slice(h * MEM_HD, (h + 1) * MEM_HD)
        z[h] = lax.dot_general(q_ref[0, :, sl], k_ref[0, :, sl], _NT,
                               preferred_element_type=F32) * MEM_HD ** -0.5

    def softmax_pv(h):
        sl = slice(h * MEM_HD, (h + 1) * MEM_HD)
        zh = z.pop(h)
        e = jnp.exp(zh - jnp.max(zh, axis=1, keepdims=True))
        p = e / jnp.sum(e, axis=1, keepdims=True)
        o_ref[0, :, sl] = jnp.dot(p.astype(BF16), v_ref[0, :, sl],
                                  preferred_element_type=F32).astype(BF16)

    for step in range(MEM_HEADS + 1):
        if step < MEM_HEADS:
            scores(step)
        if step >= 1:
            softmax_pv(step - 1)


def _mem_attention(q, k, v):
    b, s, d = q.shape
    m = k.shape[1]
    qspec = pl.BlockSpec((1, MEM_Q_TILE, d), lambda bi, qi: (bi, qi, 0))
    kvspec = pl.BlockSpec((1, m, d), lambda bi, qi: (bi, 0, 0))
    return pl.pallas_call(
        _mem_attn_kernel,
        grid=(b, s // MEM_Q_TILE),
        in_specs=[qspec, kvspec, kvspec],
        out_specs=qspec,
        out_shape=jax.ShapeDtypeStruct((b, s, d), BF16),
        compiler_params=_params(2),
        name="mem_attn",
    )(q, k, v)


def _merge_kernel(x_ref, g_ref, wg_ref, osb_ref, ofx_ref, omem_ref,
                  wsb_ref, wfx_ref, wmem_ref, wout_ref, o_ref):
    x = x_ref[...]
    h = _rms(x, g_ref[...]).astype(BF16)
    merged = None
    for i, (o_br, w_br) in enumerate(((osb_ref, wsb_ref), (ofx_ref, wfx_ref), (omem_ref, wmem_ref))):
        gate = jax.nn.sigmoid(_mm(h, wg_ref, cols=slice(i * D_MODEL, (i + 1) * D_MODEL)))
        term = gate * _mm(o_br[...], w_br)
        merged = term if merged is None else merged + term
    o_ref[...] = x + _mm(merged.astype(BF16), wout_ref)


def _merge(x2, g, wg, osb, ofx, omem, wsb, wfx, wmem, wout):
    t = x2.shape[0]
    row = lambda n: pl.BlockSpec((ROW_TILE, n), lambda i: (i, 0))
    return pl.pallas_call(
        _merge_kernel,
        grid=(t // ROW_TILE,),
        in_specs=[row(D_MODEL), _const_spec((1, D_MODEL)), _const_spec(wg.shape),
                  row(D_SB), row(D_FOX), row(D_MEM),
                  _const_spec(wsb.shape), _const_spec(wfx.shape), _const_spec(wmem.shape),
                  _const_spec(wout.shape)],
        out_specs=row(D_MODEL),
        out_shape=jax.ShapeDtypeStruct((t, D_MODEL), F32),
        compiler_params=_params(1),
        name="merge",
    )(x2, g, wg, osb, ofx, omem, wsb, wfx, wmem, wout)


def _mlp_kernel(x_ref, g_ref, wup_ref, wdn_ref, o_ref):
    x = x_ref[...]
    h = _rms(x, g_ref[...]).astype(BF16)
    acc = x
    for c in range(D_FF // FF_CHUNK):
        sl = slice(c * FF_CHUNK, (c + 1) * FF_CHUNK)
        u = jnp.maximum(_mm(h, wup_ref, cols=sl), 0.0)
        acc = acc + _mm((u * u).astype(BF16), wdn_ref, rows=sl)
    o_ref[...] = acc


def _mlp(x2, g, wup, wdn):
    t = x2.shape[0]
    row = pl.BlockSpec((ROW_TILE, D_MODEL), lambda i: (i, 0))
    return pl.pallas_call(
        _mlp_kernel,
        grid=(t // ROW_TILE,),
        in_specs=[row, _const_spec((1, D_MODEL)), _const_spec(wup.shape), _const_spec(wdn.shape)],
        out_specs=row,
        out_shape=jax.ShapeDtypeStruct((t, D_MODEL), F32),
        compiler_params=_params(1),
        name="mlp",
    )(x2, g, wup, wdn)


def _layer(x, mem, g_mix, g_memn, w_in, b_forget, g_fq, g_fk, g_mq, g_mk, w_mem_kv,
           w_sb, w_fox, w_mem, w_out, g_mlp, w_up, w_dn):
    b, s, _ = x.shape
    t = b * s
    x2 = x.reshape(t, D_MODEL)
    n_qkv = 3 * D_SB + 3 * D_FOX
    w_f = jnp.pad(w_in[:, n_qkv:n_qkv + FOX_HEADS], ((0, 0), (0, LANES - FOX_HEADS)))
    c_mq = n_qkv + FOX_HEADS
    w_mq = w_in[:, c_mq:c_mq + D_MEM]
    w_gate = w_in[:, c_mq + D_MEM:]
    bf = jnp.pad(b_forget, (0, LANES - FOX_HEADS)).reshape(1, LANES)
    row = lambda a: a.reshape(1, -1)

    sbq, sbk, sbv, fxq, fxk, fxv, mq, lf = _proj(
        x2, row(g_mix), w_in, w_mq, w_f, bf, row(jnp.tile(g_fq, FOX_HEADS)),
        row(jnp.tile(g_fk, FOX_HEADS)), row(jnp.tile(g_mq, MEM_HEADS)))
    to3 = lambda a: a.reshape(b, s, a.shape[-1])

    aq, ak = _forget(to3(lf))
    o_sb = _sb_attention(to3(sbq), to3(sbk), to3(sbv))
    o_fox = _fox_attention(to3(fxq), to3(fxk), to3(fxv), aq, ak)
    mk, mv = _mem_kv(mem, row(g_memn), w_mem_kv, row(jnp.tile(g_mk, MEM_HEADS)))
    o_mem = _mem_attention(to3(mq), mk, mv)

    x1 = _merge(x2, row(g_mix), w_gate, o_sb.reshape(t, D_SB), o_fox.reshape(t, D_FOX),
                o_mem.reshape(t, D_MEM), w_sb, w_fox, w_mem, w_out)
    out = _mlp(x1, row(g_mlp), w_up, w_dn)
    return out.reshape(b, s, D_MODEL)


def kernel(x, mem, g_mix_norm, g_mem_norm, w_in, b_forget, g_fox_q, g_fox_k, g_mem_q, g_mem_k,
           w_mem_kv, w_branch_sb, w_branch_fox, w_branch_mem, w_out, g_mlp_norm, w_ff_up, w_ff_down):
    for l in range(w_in.shape[0]):
        x = _layer(x, mem, g_mix_norm[l], g_mem_norm[l], w_in[l], b_forget[l], g_fox_q[l], g_fox_k[l],
                   g_mem_q[l], g_mem_k[l], w_mem_kv[l], w_branch_sb[l], w_branch_fox[l],
                   w_branch_mem[l], w_out[l], g_mlp_norm[l], w_ff_up[l], w_ff_down[l])
    return x
```

```python
import numpy as np
import jax
import jax.numpy as jnp
from jax import lax
from jax.experimental import pallas as pl
from jax.experimental.pallas import tpu as pltpu

D_MODEL = 1024
HD = 64
SB_HEADS = 8
FOX_HEADS = 8
MEM_HEADS = 4
MEM_HD = 128
D_SB = SB_HEADS * HD
D_FOX = FOX_HEADS * HD
D_MEM = MEM_HEADS * MEM_HD
N_BRANCH = 3
D_FF = 4 * D_MODEL
EPS = 1e-6
NEG_INF = -1e30

LANES = 128
AUG_STRIDE = 8
ROW_TILE = 512
ATT_Q_TILE = 512
ATT_K_TILE = 256
ATT_PAIRS = 2
SB_EXIT = 110.0
MEM_Q_TILE = 512
FF_CHUNK = 512
VMEM_LIMIT = 56 * 1024 * 1024

F32 = jnp.float32
BF16 = jnp.bfloat16
_NT = (((1,), (1,)), ((), ()))
_LOG2E = 1.4426950408889634


def _const_spec(shape):
    nd = len(shape)
    return pl.BlockSpec(shape, lambda *_: (0,) * nd, pipeline_mode=pl.Buffered(1))


def _params(n_axes):
    return pltpu.CompilerParams(
        dimension_semantics=("arbitrary",) * n_axes, vmem_limit_bytes=VMEM_LIMIT)


def _rms(x, g):
    ms = jnp.sum(x * x, axis=-1, keepdims=True) * (1.0 / x.shape[-1])
    return (x * lax.rsqrt(ms + EPS)) * g


def _headnorm(p, g, hd):
    rows, n = p.shape
    lane = lax.broadcasted_iota(jnp.int32, (rows, LANES), 1)
    outs = []
    for j in range(n // LANES):
        c = p[:, j * LANES:(j + 1) * LANES]
        c2 = c * c
        if hd == LANES:
            ms = jnp.sum(c2, axis=1, keepdims=True) * (1.0 / LANES)
        else:
            lo = jnp.sum(jnp.where(lane < hd, c2, 0.0), axis=1, keepdims=True)
            hi = jnp.sum(jnp.where(lane >= hd, c2, 0.0), axis=1, keepdims=True)
            ms = jnp.where(lane < hd, lo, hi) * (1.0 / hd)
        outs.append(c * lax.rsqrt(ms + EPS))
    return jnp.concatenate(outs, axis=1) * g


def _log_sigmoid(x):
    return jnp.minimum(x, 0.0) - jnp.log(1.0 + jnp.exp(-jnp.abs(x)))


def _mm(a, w_ref, rows=slice(None), cols=slice(None)):
    return jnp.dot(a, w_ref[rows, cols].astype(BF16), preferred_element_type=F32)


def _proj_kernel(x_ref, g_ref, w_ref, wmq_ref, wf_ref, bf_ref, gq_ref, gk_ref, gm_ref,
                 sbq_ref, sbk_ref, sbv_ref, fxq_ref, fxk_ref, fxv_ref, mq_ref, lf_ref):
    h = _rms(x_ref[...], g_ref[...]).astype(BF16)

    def mm(c0, n):
        return _mm(h, w_ref, cols=slice(c0, c0 + n))

    sbq_ref[...] = (mm(0, D_SB) * HD ** -0.5).astype(BF16)
    sbk_ref[...] = mm(D_SB, D_SB).astype(BF16)
    sbv_ref[...] = mm(2 * D_SB, D_SB).astype(BF16)
    c0 = 3 * D_SB
    fxq_ref[...] = (_headnorm(mm(c0, D_FOX), gq_ref[...], HD) * HD ** -0.5).astype(BF16)
    fxk_ref[...] = _headnorm(mm(c0 + D_FOX, D_FOX), gk_ref[...], HD).astype(BF16)
    fxv_ref[...] = mm(c0 + 2 * D_FOX, D_FOX).astype(BF16)
    mq_ref[...] = _headnorm(_mm(h, wmq_ref), gm_ref[...], MEM_HD).astype(BF16)
    lf_ref[...] = _log_sigmoid(_mm(h, wf_ref) + bf_ref[...])


def _proj(x2, g, w_qkv, w_mq, w_f, bf, gq, gk, gm):
    t = x2.shape[0]
    row = lambda n: pl.BlockSpec((ROW_TILE, n), lambda i: (i, 0))
    out_shapes = [jax.ShapeDtypeStruct((t, D_SB), BF16)] * 7 + [jax.ShapeDtypeStruct((t, LANES), F32)]
    return pl.pallas_call(
        _proj_kernel,
        grid=(t // ROW_TILE,),
        in_specs=[row(D_MODEL), _const_spec((1, D_MODEL)), _const_spec(w_qkv.shape),
                  _const_spec(w_mq.shape), _const_spec(w_f.shape),
                  _const_spec((1, LANES)), _const_spec((1, D_FOX)), _const_spec((1, D_FOX)),
                  _const_spec((1, D_MEM))],
        out_specs=[row(D_SB)] * 7 + [row(LANES)],
        out_shape=out_shapes,
        compiler_params=_params(1),
        name="proj",
    )(x2, g, w_qkv, w_mq, w_f, bf, gq, gk, gm)


def _forget_kernel(lf_ref, p_ref, oq_ref, ok_ref, aq_ref, ak_ref):
    f = lf_ref[0]
    s_len = f.shape[0]
    row = lax.broadcasted_iota(jnp.int32, f.shape, 0)
    sh = 1
    while sh < s_len:
        f = f + jnp.where(row >= sh, pltpu.roll(f, sh, 0), 0.0)
        sh *= 2
    f_hi = f.astype(BF16)
    r1 = f - f_hi.astype(F32)
    f_mid = r1.astype(BF16)
    f_lo = (r1 - f_mid.astype(F32)).astype(BF16)
    pieces = jnp.concatenate([f_hi, f_mid, f_lo], axis=1)
    placed = jnp.dot(pieces, p_ref[...], preferred_element_type=F32)
    aq_ref[0] = (placed[:, :LANES] + oq_ref[...]).astype(BF16)
    ak_ref[0] = (placed[:, LANES:] + ok_ref[...]).astype(BF16)


def _forget_constants():
    p = np.zeros((3 * LANES, 2 * LANES), np.float32)
    oq = np.zeros((1, LANES), np.float32)
    ok = np.zeros((1, LANES), np.float32)
    for h in range(FOX_HEADS):
        for j in range(3):
            p[j * LANES + h, h * AUG_STRIDE + j] = 1.0
            p[j * LANES + h, LANES + h * AUG_STRIDE + 3 + j] = -1.0
            oq[0, h * AUG_STRIDE + 3 + j] = 1.0
            ok[0, h * AUG_STRIDE + j] = 1.0
    return jnp.asarray(p, BF16), jnp.asarray(oq), jnp.asarray(ok)


def _forget(lf3):
    b, s, _ = lf3.shape
    p, oq, ok = _forget_constants()
    blk = pl.BlockSpec((1, s, LANES), lambda i: (i, 0, 0))
    return pl.pallas_call(
        _forget_kernel,
        grid=(b,),
        in_specs=[blk, _const_spec(p.shape), _const_spec(oq.shape), _const_spec(ok.shape)],
        out_specs=[blk, blk],
        out_shape=[jax.ShapeDtypeStruct((b, s, LANES), BF16)] * 2,
        compiler_params=_params(1),
        name="forget",
    )(lf3, p, oq, ok)


def _stack_heads(x, lane):
    zero = jnp.zeros_like(x)
    return jnp.concatenate([jnp.where(lane < HD, x, zero), jnp.where(lane >= HD, x, zero)], axis=0)


def _unstack_heads(x, lane):
    rows = x.shape[0] // 2
    return jnp.where(lane < HD, x[:rows], x[rows:])


def _softplus(z):
    return jnp.maximum(z, 0.0) + jnp.log(1.0 + jnp.exp2(jnp.abs(z) * (-_LOG2E)))


def _split_bf16(x):
    hi = x.astype(BF16)
    return hi, (x - hi.astype(F32)).astype(BF16)


def _sb_kernel(q_ref, k_ref, v_ref, m_ref, o_ref, flag_ref, acc_ref, carry_ref):
    s_len = q_ref.shape[1]
    t = ATT_K_TILE
    nb = s_len // t
    lane = lax.broadcasted_iota(jnp.int32, (t, LANES), 1)
    tri = (lax.broadcasted_iota(jnp.int32, (2 * t, t), 1)
           < (lax.broadcasted_iota(jnp.int32, (2 * t, t), 0) & (t - 1)))
    msum = m_ref[...]
    reps = t // LANES
    order = [(pr, j) for pr in range(ATT_PAIRS) for j in reversed(range(nb))]
    z, sp, c, acc_diag, carry_diag = {}, {}, {}, {}, {}

    def qblock(pr, j):
        return _stack_heads(q_ref[0, j * t:(j + 1) * t, pr * LANES:(pr + 1) * LANES], lane)

    def scores(g):
        pr, j = order[g]
        lhs = qblock(pr, j) if j == nb - 1 else jnp.concatenate([qblock(pr, j), qblock(pr, j + 1)], axis=0)
        z[g] = lax.dot_general(lhs, k_ref[0, j * t:(j + 1) * t, pr * LANES:(pr + 1) * LANES], _NT,
                               preferred_element_type=F32)

    def suffix_sums(g):
        s = _softplus(z[g])
        top = jnp.where(tri, s[:2 * t], 0.0)
        s = top if order[g][1] == nb - 1 else jnp.concatenate([top, s[2 * t:]], axis=0)
        hi, lo = _split_bf16(s)
        sp[g] = s
        c[g] = (jnp.dot(hi, msum, preferred_element_type=F32)
                + jnp.dot(lo, msum, preferred_element_type=F32))

    def weights_pv(g):
        pr, j = order[g]
        ls = slice(pr * LANES, (pr + 1) * LANES)
        zc = z.pop(g) - c.pop(g)
        w = jnp.where(tri, jnp.exp(zc[:2 * t]), 0.0)
        if j < nb - 1:
            w_prev = jnp.exp(zc[2 * t:] - jnp.concatenate([carry_diag[(pr, j + 1)]] * reps, axis=1))
            w = jnp.concatenate([w, w_prev], axis=0)
        pv = jnp.dot(w.astype(BF16), v_ref[0, j * t:(j + 1) * t, ls], preferred_element_type=F32)
        rs = jnp.sum(sp.pop(g), axis=1, keepdims=True)
        if j < nb - 1:
            done = acc_diag.pop((pr, j + 1)) + pv[2 * t:]
            o_ref[0, (j + 1) * t:(j + 2) * t, ls] = _unstack_heads(done, lane).astype(BF16)
            total = carry_diag.pop((pr, j + 1)) + rs[2 * t:]
            if j >= 1:
                flag_ref[pr * nb + j + 1] = jnp.min(total)
        if j == 0:
            o_ref[0, 0:t, ls] = _unstack_heads(pv[:2 * t], lane).astype(BF16)
        else:
            acc_diag[(pr, j)] = pv[:2 * t]
            carry_diag[(pr, j)] = jnp.broadcast_to(rs[:2 * t], (2 * t, LANES))

    n_groups = len(order)
    for step in range(n_groups + 2):
        if step < n_groups:
            scores(step)
        if 1 <= step <= n_groups:
            suffix_sums(step - 1)
        if step >= 2:
            weights_pv(step - 2)

    def recompute_block(pr, jb):
        ls = slice(pr * LANES, (pr + 1) * LANES)
        rows = pl.ds(pl.multiple_of(jb * t, t), t)
        qb = _stack_heads(q_ref[0, rows, ls], lane)
        acc_ref[...] = jnp.zeros_like(acc_ref)
        carry_ref[...] = jnp.zeros_like(carry_ref)

        def tile(kb, diag):
            keys = pl.ds(pl.multiple_of(kb * t, t), t)
            zz = lax.dot_general(qb, k_ref[0, keys, ls], _NT, preferred_element_type=F32)
            s = _softplus(zz)
            if diag:
                s = jnp.where(tri, s, 0.0)
            hi, lo = _split_bf16(s)
            cc = (jnp.dot(hi, msum, preferred_element_type=F32)
                  + jnp.dot(lo, msum, preferred_element_type=F32))
            w = jnp.exp(zz - cc - jnp.concatenate([carry_ref[...]] * reps, axis=1))
            if diag:
                w = jnp.where(tri, w, 0.0)
            acc_ref[...] += jnp.dot(w.astype(BF16), v_ref[0, keys, ls], preferred_element_type=F32)
            carry_ref[...] += jnp.sum(s, axis=1, keepdims=True)

        tile(jb, True)

        def body(i, _):
            tile(jb - 1 - i, False)
            return 0

        lax.fori_loop(0, jb, body, 0)
        o_ref[0, rows, ls] = _unstack_heads(acc_ref[...], lane).astype(BF16)

    for pr in range(ATT_PAIRS):
        def redo(jb, _, pr=pr):
            pl.when(flag_ref[pr * nb + jb] < SB_EXIT)(lambda: recompute_block(pr, jb))
            return 0

        lax.fori_loop(2, nb, redo, 0)


def _sb_attention(q, k, v):
    b, s, d = q.shape
    t = ATT_K_TILE
    width = ATT_PAIRS * LANES
    ii = np.arange(t)
    msum = jnp.asarray(ii[:, None] >= ii[None, :], BF16)
    spec = pl.BlockSpec((1, s, width), lambda bi, g: (bi, 0, g))
    return pl.pallas_call(
        _sb_kernel,
        grid=(b, d // width),
        in_specs=[spec, spec, spec, _const_spec((t, t))],
        out_specs=spec,
        out_shape=jax.ShapeDtypeStruct((b, s, d), BF16),
        scratch_shapes=[pltpu.SMEM((ATT_PAIRS * (s // t),), F32), pltpu.VMEM((2 * t, LANES), F32),
                        pltpu.VMEM((2 * t, LANES), F32)],
        compiler_params=_params(2),
        name="sb_attn",
    )(q, k, v, msum)


def _fox_kernel(q_ref, aq_ref, k_ref, ak_ref, v_ref, o_ref):
    s_len = q_ref.shape[1]
    t = ATT_Q_TILE
    hb = t // 2
    lane = lax.broadcasted_iota(jnp.int32, (hb, LANES), 1)
    group = lane // AUG_STRIDE
    tri = (lax.broadcasted_iota(jnp.int32, (t, hb), 1)
           <= (lax.broadcasted_iota(jnp.int32, (t, hb), 0) & (hb - 1)))
    ones = jnp.ones((t, LANES), BF16)
    tiles = []
    for pr in range(ATT_PAIRS):
        for qi in range(s_len // t):
            tiles += [(pr, qi, "full", kb * t, t) for kb in range(qi)]
            tiles += [(pr, qi, "diag_all", qi * t, hb), (pr, qi, "diag_b", qi * t + hb, hb)]
    qext, kvext, z, m, acc = {}, {}, {}, {}, {}

    def stacked_q(pr, qi):
        ls = slice(pr * LANES, (pr + 1) * LANES)
        pair = ATT_PAIRS * pl.program_id(1) + pr
        parts = []
        for half in range(2):
            rows = slice(qi * t + half * hb, qi * t + (half + 1) * hb)
            aq = aq_ref[0, rows, :]
            zero = jnp.zeros_like(aq)
            aug = jnp.concatenate([jnp.where(group == 2 * pair, aq, zero),
                                   jnp.where(group == 2 * pair + 1, aq, zero)], axis=0)
            parts.append(jnp.concatenate([_stack_heads(q_ref[0, rows, ls], lane), aug], axis=1))
        return jnp.concatenate(parts, axis=0)

    def keys_values(pr, start, size):
        if (pr, start, size) not in kvext:
            ls = slice(pr * LANES, (pr + 1) * LANES)
            keys = slice(start, start + size)
            kext = jnp.concatenate([k_ref[0, keys, ls], ak_ref[0, keys, :]], axis=1)
            vext = jnp.concatenate([v_ref[0, keys, ls], ones[:size]], axis=1)
            kvext[(pr, start, size)] = (kext, vext)
        return kvext[(pr, start, size)]

    def scores(n):
        pr, qi, kind, start, size = tiles[n]
        if (pr, qi) not in qext:
            qext[(pr, qi)] = stacked_q(pr, qi)
        lhs = qext[(pr, qi)]
        kext = keys_values(pr, start, size)[0]
        if kind == "diag_b":
            s = lax.dot_general(lhs[t:], kext, _NT, preferred_element_type=F32)
            s = jnp.where(tri, s, NEG_INF)
        else:
            s = lax.dot_general(lhs, kext, _NT, preferred_element_type=F32)
            if kind == "diag_all":
                s = jnp.concatenate([jnp.where(tri, s[:t], NEG_INF), s[t:]], axis=0)
        z[n] = s

    def softmax_pv(n):
        pr, qi, kind, start, size = tiles[n]
        key = (pr, qi)
        s = z.pop(n)
        vext = keys_values(pr, start, size)[1]
        row_max = jnp.max(s, axis=1, keepdims=True)
        if key not in m:
            m_new = jnp.broadcast_to(row_max, (s.shape[0], LANES))
        else:
            m_old = m[key][t:] if kind == "diag_b" else m[key]
            m_new = jnp.maximum(m_old, row_max)
        p = jnp.exp(s - jnp.concatenate([m_new] * (size // LANES), axis=1))
        pv = jnp.dot(p.astype(BF16), vext, preferred_element_type=F32)
        if key not in m:
            acc[key], m[key] = pv, m_new
        else:
            acc_old = acc[key][t:] if kind == "diag_b" else acc[key]
            alpha = jnp.exp(m_old - m_new)
            acc_new = jnp.concatenate([alpha, alpha], axis=1) * acc_old + pv
            if kind == "diag_b":
                acc[key] = jnp.concatenate([acc[key][:t], acc_new], axis=0)
                m[key] = jnp.concatenate([m[key][:t], m_new], axis=0)
            else:
                acc[key], m[key] = acc_new, m_new
        if kind == "diag_b":
            a = acc.pop(key)
            o = a[:, :LANES] / a[:, LANES:]
            out = jnp.concatenate([_unstack_heads(o[:t], lane), _unstack_heads(o[t:], lane)], axis=0)
            o_ref[0, qi * t:(qi + 1) * t, pr * LANES:(pr + 1) * LANES] = out.astype(BF16)
            m.pop(key)
            qext.pop(key)

    n_tiles = len(tiles)
    for step in range(n_tiles + 1):
        if step < n_tiles:
            scores(step)
        if step >= 1:
            softmax_pv(step - 1)


def _fox_attention(q, k, v, aq, ak):
    b, s, d = q.shape
    width = ATT_PAIRS * LANES
    spec = pl.BlockSpec((1, s, width), lambda bi, g: (bi, 0, g))
    aspec = pl.BlockSpec((1, s, LANES), lambda bi, g: (bi, 0, 0))
    return pl.pallas_call(
        _fox_kernel,
        grid=(b, d // width),
        in_specs=[spec, aspec, spec, aspec, spec],
        out_specs=spec,
        out_shape=jax.ShapeDtypeStruct((b, s, d), BF16),
        compiler_params=_params(2),
        name="fox_attn",
    )(q, aq, k, ak, v)


def _mem_kv_kernel(mem_ref, g_ref, w_ref, gk_ref, mk_ref, mv_ref):
    mh = _rms(mem_ref[0], g_ref[...]).astype(BF16)
    kv = _mm(mh, w_ref)
    mk_ref[0] = _headnorm(kv[:, :D_MEM], gk_ref[...], MEM_HD).astype(BF16)
    mv_ref[0] = kv[:, D_MEM:].astype(BF16)


def _mem_kv(mem, g, w, gk):
    b, m, _ = mem.shape
    out = pl.BlockSpec((1, m, D_MEM), lambda i: (i, 0, 0))
    return pl.pallas_call(
        _mem_kv_kernel,
        grid=(b,),
        in_specs=[pl.BlockSpec((1, m, D_MODEL), lambda i: (i, 0, 0)), _const_spec((1, D_MODEL)),
                  _const_spec((D_MODEL, 2 * D_MEM)), _const_spec((1, D_MEM))],
        out_specs=[out, out],
        out_shape=[jax.ShapeDtypeStruct((b, m, D_MEM), BF16)] * 2,
        compiler_params=_params(1),
        name="mem_kv",
    )(mem, g, w, gk)


def _mem_attn_kernel(q_ref, k_ref, v_ref, o_ref):
    z = {}

    def scores(h):
        sl = slice(h * MEM_HD, (h + 1) * MEM_HD)
        z[h] = lax.dot_general(q_ref[0, :, sl], k_ref[0, :, sl], _NT,
                               preferred_element_type=F32) * MEM_HD ** -0.5

    def softmax_pv(h):
        sl = slice(h * MEM_HD, (h + 1) * MEM_HD)
        zh = z.pop(h)
        e = jnp.exp(zh - jnp.max(zh, axis=1, keepdims=True))
        p = e / jnp.sum(e, axis=1, keepdims=True)
        o_ref[0, :, sl] = jnp.dot(p.astype(BF16), v_ref[0, :, sl],
                                  preferred_element_type=F32).astype(BF16)

    for step in range(MEM_HEADS + 1):
        if step < MEM_HEADS:
            scores(step)
        if step >= 1:
            softmax_pv(step - 1)


def _mem_attention(q, k, v):
    b, s, d = q.shape
    m = k.shape[1]
    qspec = pl.BlockSpec((1, MEM_Q_TILE, d), lambda bi, qi: (bi, qi, 0))
    kvspec = pl.BlockSpec((1, m, d), lambda bi, qi: (bi, 0, 0))
    return pl.pallas_call(
        _mem_attn_kernel,
        grid=(b, s // MEM_Q_TILE),
        in_specs=[qspec, kvspec, kvspec],
        out_specs=qspec,
        out_shape=jax.ShapeDtypeStruct((b, s, d), BF16),
        compiler_params=_params(2),
        name="mem_attn",
    )(q, k, v)


def _merge_kernel(x_ref, g_ref, wg_ref, osb_ref, ofx_ref, omem_ref,
                  wsb_ref, wfx_ref, wmem_ref, wout_ref, o_ref):
    x = x_ref[...]
    h = _rms(x, g_ref[...]).astype(BF16)
    merged = None
    for i, (o_br, w_br) in enumerate(((osb_ref, wsb_ref), (ofx_ref, wfx_ref), (omem_ref, wmem_ref))):
        gate = jax.nn.sigmoid(_mm(h, wg_ref, cols=slice(i * D_MODEL, (i + 1) * D_MODEL)))
        term = gate * _mm(o_br[...], w_br)
        merged = term if merged is None else merged + term
    o_ref[...] = x + _mm(merged.astype(BF16), wout_ref)


def _merge(x2, g, wg, osb, ofx, omem, wsb, wfx, wmem, wout):
    t = x2.shape[0]
    row = lambda n: pl.BlockSpec((ROW_TILE, n), lambda i: (i, 0))
    return pl.pallas_call(
        _merge_kernel,
        grid=(t // ROW_TILE,),
        in_specs=[row(D_MODEL), _const_spec((1, D_MODEL)), _const_spec(wg.shape),
                  row(D_SB), row(D_FOX), row(D_MEM),
                  _const_spec(wsb.shape), _const_spec(wfx.shape), _const_spec(wmem.shape),
                  _const_spec(wout.shape)],
        out_specs=row(D_MODEL),
        out_shape=jax.ShapeDtypeStruct((t, D_MODEL), F32),
        compiler_params=_params(1),
        name="merge",
    )(x2, g, wg, osb, ofx, omem, wsb, wfx, wmem, wout)


def _mlp_kernel(x_ref, g_ref, wup_ref, wdn_ref, o_ref):
    x = x_ref[...]
    h = _rms(x, g_ref[...]).astype(BF16)
    acc = x
    for c in range(D_FF // FF_CHUNK):
        sl = slice(c * FF_CHUNK, (c + 1) * FF_CHUNK)
        u = jnp.maximum(_mm(h, wup_ref, cols=sl), 0.0)
        acc = acc + _mm((u * u).astype(BF16), wdn_ref, rows=sl)
    o_ref[...] = acc


def _mlp(x2, g, wup, wdn):
    t = x2.shape[0]
    row = pl.BlockSpec((ROW_TILE, D_MODEL), lambda i: (i, 0))
    return pl.pallas_call(
        _mlp_kernel,
        grid=(t // ROW_TILE,),
        in_specs=[row, _const_spec((1, D_MODEL)), _const_spec(wup.shape), _const_spec(wdn.shape)],
        out_specs=row,
        out_shape=jax.ShapeDtypeStruct((t, D_MODEL), F32),
        compiler_params=_params(1),
        name="mlp",
    )(x2, g, wup, wdn)


def _layer(x, mem, g_mix, g_memn, w_in, b_forget, g_fq, g_fk, g_mq, g_mk, w_mem_kv,
           w_sb, w_fox, w_mem, w_out, g_mlp, w_up, w_dn):
    b, s, _ = x.shape
    t = b * s
    x2 = x.reshape(t, D_MODEL)
    n_qkv = 3 * D_SB + 3 * D_FOX
    w_f = jnp.pad(w_in[:, n_qkv:n_qkv + FOX_HEADS], ((0, 0), (0, LANES - FOX_HEADS)))
    c_mq = n_qkv + FOX_HEADS
    w_qkv = w_in[:, :n_qkv]
    w_mq = w_in[:, c_mq:c_mq + D_MEM]
    w_gate = w_in[:, c_mq + D_MEM:]
    bf = jnp.pad(b_forget, (0, LANES - FOX_HEADS)).reshape(1, LANES)
    row = lambda a: a.reshape(1, -1)

    sbq, sbk, sbv, fxq, fxk, fxv, mq, lf = _proj(
        x2, row(g_mix), w_qkv, w_mq, w_f, bf, row(jnp.tile(g_fq, FOX_HEADS)),
        row(jnp.tile(g_fk, FOX_HEADS)), row(jnp.tile(g_mq, MEM_HEADS)))
    to3 = lambda a: a.reshape(b, s, a.shape[-1])

    aq, ak = _forget(to3(lf))
    o_sb = _sb_attention(to3(sbq), to3(sbk), to3(sbv))
    o_fox = _fox_attention(to3(fxq), to3(fxk), to3(fxv), aq, ak)
    mk, mv = _mem_kv(mem, row(g_memn), w_mem_kv, row(jnp.tile(g_mk, MEM_HEADS)))
    o_mem = _mem_attention(to3(mq), mk, mv)

    x1 = _merge(x2, row(g_mix), w_gate, o_sb.reshape(t, D_SB), o_fox.reshape(t, D_FOX),
                o_mem.reshape(t, D_MEM), w_sb, w_fox, w_mem, w_out)
    out = _mlp(x1, row(g_mlp), w_up, w_dn)
    return out.reshape(b, s, D_MODEL)


def kernel(x, mem, g_mix_norm, g_mem_norm, w_in, b_forget, g_fox_q, g_fox_k, g_mem_q, g_mem_k,
           w_mem_kv, w_branch_sb, w_branch_fox, w_branch_mem, w_out, g_mlp_norm, w_ff_up, w_ff_down):
    for l in range(w_in.shape[0]):
        x = _layer(x, mem, g_mix_norm[l], g_mem_norm[l], w_in[l], b_forget[l], g_fox_q[l], g_fox_k[l],
                   g_mem_q[l], g_mem_k[l], w_mem_kv[l], w_branch_sb[l], w_branch_fox[l],
                   w_branch_mem[l], w_out[l], g_mlp_norm[l], w_ff_up[l], w_ff_down[l])
    return x
```

```python
import numpy as np
import jax
import jax.numpy as jnp
from jax import lax
from jax.experimental import pallas as pl
from jax.experimental.pallas import tpu as pltpu

D_MODEL = 1024
HD = 64
SB_HEADS = 8
FOX_HEADS = 8
MEM_HEADS = 4
MEM_HD = 128
D_SB = SB_HEADS * HD
D_FOX = FOX_HEADS * HD
D_MEM = MEM_HEADS * MEM_HD
N_BRANCH = 3
D_FF = 4 * D_MODEL
EPS = 1e-6
NEG_INF = -1e30

LANES = 128
AUG_STRIDE = 8
ROW_TILE = 512
ATT_Q_TILE = 512
ATT_K_TILE = 256
ATT_PAIRS = 2
SB_EXIT = 110.0
MEM_Q_TILE = 512
FF_CHUNK = 512
VMEM_LIMIT = 56 * 1024 * 1024

F32 = jnp.float32
BF16 = jnp.bfloat16
_NT = (((1,), (1,)), ((), ()))
_LOG2E = 1.4426950408889634


def _const_spec(shape):
    nd = len(shape)
    return pl.BlockSpec(shape, lambda *_: (0,) * nd, pipeline_mode=pl.Buffered(1))


def _params(n_axes):
    return pltpu.CompilerParams(
        dimension_semantics=("arbitrary",) * n_axes, vmem_limit_bytes=VMEM_LIMIT)


def _rms(x, g):
    ms = jnp.sum(x * x, axis=-1, keepdims=True) * (1.0 / x.shape[-1])
    return (x * lax.rsqrt(ms + EPS)) * g


def _headnorm(p, g, hd):
    rows, n = p.shape
    lane = lax.broadcasted_iota(jnp.int32, (rows, LANES), 1)
    outs = []
    for j in range(n // LANES):
        c = p[:, j * LANES:(j + 1) * LANES]
        c2 = c * c
        if hd == LANES:
            ms = jnp.sum(c2, axis=1, keepdims=True) * (1.0 / LANES)
        else:
            lo = jnp.sum(jnp.where(lane < hd, c2, 0.0), axis=1, keepdims=True)
            hi = jnp.sum(jnp.where(lane >= hd, c2, 0.0), axis=1, keepdims=True)
            ms = jnp.where(lane < hd, lo, hi) * (1.0 / hd)
        outs.append(c * lax.rsqrt(ms + EPS))
    return jnp.concatenate(outs, axis=1) * g


def _log_sigmoid(x):
    return jnp.minimum(x, 0.0) - jnp.log(1.0 + jnp.exp(-jnp.abs(x)))


def _mm(a, w_ref, rows=slice(None), cols=slice(None)):
    return jnp.dot(a, w_ref[rows, cols].astype(BF16), preferred_element_type=F32)


def _mm_t(a, wt_ref, rows):
    return lax.dot_general(a, wt_ref[rows, :].astype(BF16), _NT, preferred_element_type=F32)


def _proj_kernel(x_ref, g_ref, wt_ref, bf_ref, gq_ref, gk_ref, gm_ref,
                 sbq_ref, sbk_ref, sbv_ref, fxq_ref, fxk_ref, fxv_ref, mq_ref, lf_ref):
    h = _rms(x_ref[...], g_ref[...]).astype(BF16)

    def mm(c0, n):
        return _mm_t(h, wt_ref, slice(c0, c0 + n))

    sbq_ref[...] = (mm(0, D_SB) * HD ** -0.5).astype(BF16)
    sbk_ref[...] = mm(D_SB, D_SB).astype(BF16)
    sbv_ref[...] = mm(2 * D_SB, D_SB).astype(BF16)
    c0 = 3 * D_SB
    fxq_ref[...] = (_headnorm(mm(c0, D_FOX), gq_ref[...], HD) * HD ** -0.5).astype(BF16)
    fxk_ref[...] = _headnorm(mm(c0 + D_FOX, D_FOX), gk_ref[...], HD).astype(BF16)
    fxv_ref[...] = mm(c0 + 2 * D_FOX, D_FOX).astype(BF16)
    c_f = c0 + 3 * D_FOX
    mq_ref[...] = _headnorm(mm(c_f + FOX_HEADS, D_MEM), gm_ref[...], MEM_HD).astype(BF16)
    lf_ref[...] = _log_sigmoid(mm(c_f, LANES) + bf_ref[...])


def _proj(x2, g, w_in_t, bf, gq, gk, gm):
    t = x2.shape[0]
    n_rows = 3 * D_SB + 3 * D_FOX + FOX_HEADS + D_MEM
    row = lambda n: pl.BlockSpec((ROW_TILE, n), lambda i: (i, 0))
    out_shapes = [jax.ShapeDtypeStruct((t, D_SB), BF16)] * 7 + [jax.ShapeDtypeStruct((t, LANES), F32)]
    return pl.pallas_call(
        _proj_kernel,
        grid=(t // ROW_TILE,),
        in_specs=[row(D_MODEL), _const_spec((1, D_MODEL)), _const_spec((n_rows, D_MODEL)),
                  _const_spec((1, LANES)), _const_spec((1, D_FOX)), _const_spec((1, D_FOX)),
                  _const_spec((1, D_MEM))],
        out_specs=[row(D_SB)] * 7 + [row(LANES)],
        out_shape=out_shapes,
        compiler_params=_params(1),
        name="proj",
    )(x2, g, w_in_t, bf, gq, gk, gm)


def _forget_kernel(lf_ref, p_ref, oq_ref, ok_ref, aq_ref, ak_ref):
    f = lf_ref[0]
    s_len = f.shape[0]
    row = lax.broadcasted_iota(jnp.int32, f.shape, 0)
    sh = 1
    while sh < s_len:
        f = f + jnp.where(row >= sh, pltpu.roll(f, sh, 0), 0.0)
        sh *= 2
    f_hi = f.astype(BF16)
    r1 = f - f_hi.astype(F32)
    f_mid = r1.astype(BF16)
    f_lo = (r1 - f_mid.astype(F32)).astype(BF16)
    pieces = jnp.concatenate([f_hi, f_mid, f_lo], axis=1)
    placed = jnp.dot(pieces, p_ref[...], preferred_element_type=F32)
    aq_ref[0] = (placed[:, :LANES] + oq_ref[...]).astype(BF16)
    ak_ref[0] = (placed[:, LANES:] + ok_ref[...]).astype(BF16)


def _forget_constants():
    p = np.zeros((3 * LANES, 2 * LANES), np.float32)
    oq = np.zeros((1, LANES), np.float32)
    ok = np.zeros((1, LANES), np.float32)
    for h in range(FOX_HEADS):
        for j in range(3):
            p[j * LANES + h, h * AUG_STRIDE + j] = 1.0
            p[j * LANES + h, LANES + h * AUG_STRIDE + 3 + j] = -1.0
            oq[0, h * AUG_STRIDE + 3 + j] = 1.0
            ok[0, h * AUG_STRIDE + j] = 1.0
    return jnp.asarray(p, BF16), jnp.asarray(oq), jnp.asarray(ok)


def _forget(lf3):
    b, s, _ = lf3.shape
    p, oq, ok = _forget_constants()
    blk = pl.BlockSpec((1, s, LANES), lambda i: (i, 0, 0))
    return pl.pallas_call(
        _forget_kernel,
        grid=(b,),
        in_specs=[blk, _const_spec(p.shape), _const_spec(oq.shape), _const_spec(ok.shape)],
        out_specs=[blk, blk],
        out_shape=[jax.ShapeDtypeStruct((b, s, LANES), BF16)] * 2,
        compiler_params=_params(1),
        name="forget",
    )(lf3, p, oq, ok)


def _stack_heads(x, lane):
    zero = jnp.zeros_like(x)
    return jnp.concatenate([jnp.where(lane < HD, x, zero), jnp.where(lane >= HD, x, zero)], axis=0)


def _unstack_heads(x, lane):
    rows = x.shape[0] // 2
    return jnp.where(lane < HD, x[:rows], x[rows:])


def _softplus(z):
    return jnp.maximum(z, 0.0) + jnp.log(1.0 + jnp.exp2(jnp.abs(z) * (-_LOG2E)))


def _split_bf16(x):
    hi = x.astype(BF16)
    return hi, (x - hi.astype(F32)).astype(BF16)


def _sb_kernel(q_ref, k_ref, v_ref, m_ref, o_ref, flag_ref, acc_ref, carry_ref):
    s_len = q_ref.shape[1]
    t = ATT_K_TILE
    nb = s_len // t
    lane = lax.broadcasted_iota(jnp.int32, (t, LANES), 1)
    tri = (lax.broadcasted_iota(jnp.int32, (2 * t, t), 1)
           < (lax.broadcasted_iota(jnp.int32, (2 * t, t), 0) & (t - 1)))
    msum = m_ref[...]
    reps = t // LANES
    order = [(pr, j) for pr in range(ATT_PAIRS) for j in reversed(range(nb))]
    z, sp, c, acc_diag, carry_diag = {}, {}, {}, {}, {}

    def qblock(pr, j):
        return _stack_heads(q_ref[0, j * t:(j + 1) * t, pr * LANES:(pr + 1) * LANES], lane)

    def scores(g):
        pr, j = order[g]
        lhs = qblock(pr, j) if j == nb - 1 else jnp.concatenate([qblock(pr, j), qblock(pr, j + 1)], axis=0)
        z[g] = lax.dot_general(lhs, k_ref[0, j * t:(j + 1) * t, pr * LANES:(pr + 1) * LANES], _NT,
                               preferred_element_type=F32)

    def suffix_sums(g):
        s = _softplus(z[g])
        top = jnp.where(tri, s[:2 * t], 0.0)
        s = top if order[g][1] == nb - 1 else jnp.concatenate([top, s[2 * t:]], axis=0)
        hi, lo = _split_bf16(s)
        sp[g] = s
        c[g] = (jnp.dot(hi, msum, preferred_element_type=F32)
                + jnp.dot(lo, msum, preferred_element_type=F32))

    def weights_pv(g):
        pr, j = order[g]
        ls = slice(pr * LANES, (pr + 1) * LANES)
        zc = z.pop(g) - c.pop(g)
        w = jnp.where(tri, jnp.exp(zc[:2 * t]), 0.0)
        if j < nb - 1:
            w_prev = jnp.exp(zc[2 * t:] - jnp.concatenate([carry_diag[(pr, j + 1)]] * reps, axis=1))
            w = jnp.concatenate([w, w_prev], axis=0)
        pv = jnp.dot(w.astype(BF16), v_ref[0, j * t:(j + 1) * t, ls], preferred_element_type=F32)
        rs = jnp.sum(sp.pop(g), axis=1, keepdims=True)
        if j < nb - 1:
            done = acc_diag.pop((pr, j + 1)) + pv[2 * t:]
            o_ref[0, (j + 1) * t:(j + 2) * t, ls] = _unstack_heads(done, lane).astype(BF16)
            total = carry_diag.pop((pr, j + 1)) + rs[2 * t:]
            if j >= 1:
                flag_ref[pr * nb + j + 1] = jnp.min(total)
        if j == 0:
            o_ref[0, 0:t, ls] = _unstack_heads(pv[:2 * t], lane).astype(BF16)
        else:
            acc_diag[(pr, j)] = pv[:2 * t]
            carry_diag[(pr, j)] = jnp.broadcast_to(rs[:2 * t], (2 * t, LANES))

    n_groups = len(order)
    for step in range(n_groups + 2):
        if step < n_groups:
            scores(step)
        if 1 <= step <= n_groups:
            suffix_sums(step - 1)
        if step >= 2:
            weights_pv(step - 2)

    def recompute_block(pr, jb):
        ls = slice(pr * LANES, (pr + 1) * LANES)
        rows = pl.ds(pl.multiple_of(jb * t, t), t)
        qb = _stack_heads(q_ref[0, rows, ls], lane)
        acc_ref[...] = jnp.zeros_like(acc_ref)
        carry_ref[...] = jnp.zeros_like(carry_ref)

        def tile(kb, diag):
            keys = pl.ds(pl.multiple_of(kb * t, t), t)
            zz = lax.dot_general(qb, k_ref[0, keys, ls], _NT, preferred_element_type=F32)
            s = _softplus(zz)
            if diag:
                s = jnp.where(tri, s, 0.0)
            hi, lo = _split_bf16(s)
            cc = (jnp.dot(hi, msum, preferred_element_type=F32)
                  + jnp.dot(lo, msum, preferred_element_type=F32))
            w = jnp.exp(zz - cc - jnp.concatenate([carry_ref[...]] * reps, axis=1))
            if diag:
                w = jnp.where(tri, w, 0.0)
            acc_ref[...] += jnp.dot(w.astype(BF16), v_ref[0, keys, ls], preferred_element_type=F32)
            carry_ref[...] += jnp.sum(s, axis=1, keepdims=True)

        tile(jb, True)

        def body(i, _):
            tile(jb - 1 - i, False)
            return 0

        lax.fori_loop(0, jb, body, 0)
        o_ref[0, rows, ls] = _unstack_heads(acc_ref[...], lane).astype(BF16)

    for pr in range(ATT_PAIRS):
        def redo(jb, _, pr=pr):
            pl.when(flag_ref[pr * nb + jb] < SB_EXIT)(lambda: recompute_block(pr, jb))
            return 0

        lax.fori_loop(2, nb, redo, 0)


def _sb_attention(q, k, v):
    b, s, d = q.shape
    t = ATT_K_TILE
    width = ATT_PAIRS * LANES
    ii = np.arange(t)
    msum = jnp.asarray(ii[:, None] >= ii[None, :], BF16)
    spec = pl.BlockSpec((1, s, width), lambda bi, g: (bi, 0, g))
    return pl.pallas_call(
        _sb_kernel,
        grid=(b, d // width),
        in_specs=[spec, spec, spec, _const_spec((t, t))],
        out_specs=spec,
        out_shape=jax.ShapeDtypeStruct((b, s, d), BF16),
        scratch_shapes=[pltpu.SMEM((ATT_PAIRS * (s // t),), F32), pltpu.VMEM((2 * t, LANES), F32),
                        pltpu.VMEM((2 * t, LANES), F32)],
        compiler_params=_params(2),
        name="sb_attn",
    )(q, k, v, msum)


def _fox_kernel(q_ref, aq_ref, k_ref, ak_ref, v_ref, o_ref):
    s_len = q_ref.shape[1]
    t = ATT_Q_TILE
    hb = t // 2
    lane = lax.broadcasted_iota(jnp.int32, (hb, LANES), 1)
    group = lane // AUG_STRIDE
    tri = (lax.broadcasted_iota(jnp.int32, (t, hb), 1)
           <= (lax.broadcasted_iota(jnp.int32, (t, hb), 0) & (hb - 1)))
    ones = jnp.ones((t, LANES), BF16)
    tiles = []
    for pr in range(ATT_PAIRS):
        for qi in range(s_len // t):
            tiles += [(pr, qi, "full", kb * t, t) for kb in range(qi)]
            tiles += [(pr, qi, "diag_all", qi * t, hb), (pr, qi, "diag_b", qi * t + hb, hb)]
    qext, kvext, z, m, acc = {}, {}, {}, {}, {}

    def stacked_q(pr, qi):
        ls = slice(pr * LANES, (pr + 1) * LANES)
        pair = ATT_PAIRS * pl.program_id(1) + pr
        parts = []
        for half in range(2):
            rows = slice(qi * t + half * hb, qi * t + (half + 1) * hb)
            aq = aq_ref[0, rows, :]
            zero = jnp.zeros_like(aq)
            aug = jnp.concatenate([jnp.where(group == 2 * pair, aq, zero),
                                   jnp.where(group == 2 * pair + 1, aq, zero)], axis=0)
            parts.append(jnp.concatenate([_stack_heads(q_ref[0, rows, ls], lane), aug], axis=1))
        return jnp.concatenate(parts, axis=0)

    def keys_values(pr, start, size):
        if (pr, start, size) not in kvext:
            ls = slice(pr * LANES, (pr + 1) * LANES)
            keys = slice(start, start + size)
            kext = jnp.concatenate([k_ref[0, keys, ls], ak_ref[0, keys, :]], axis=1)
            vext = jnp.concatenate([v_ref[0, keys, ls], ones[:size]], axis=1)
            kvext[(pr, start, size)] = (kext, vext)
        return kvext[(pr, start, size)]

    def scores(n):
        pr, qi, kind, start, size = tiles[n]
        if (pr, qi) not in qext:
            qext[(pr, qi)] = stacked_q(pr, qi)
        lhs = qext[(pr, qi)]
        kext = keys_values(pr, start, size)[0]
        if kind == "diag_b":
            s = lax.dot_general(lhs[t:], kext, _NT, preferred_element_type=F32)
            s = jnp.where(tri, s, NEG_INF)
        else:
            s = lax.dot_general(lhs, kext, _NT, preferred_element_type=F32)
            if kind == "diag_all":
                s = jnp.concatenate([jnp.where(tri, s[:t], NEG_INF), s[t:]], axis=0)
        z[n] = s

    def softmax_pv(n):
        pr, qi, kind, start, size = tiles[n]
        key = (pr, qi)
        s = z.pop(n)
        vext = keys_values(pr, start, size)[1]
        row_max = jnp.max(s, axis=1, keepdims=True)
        if key not in m:
            m_new = jnp.broadcast_to(row_max, (s.shape[0], LANES))
        else:
            m_old = m[key][t:] if kind == "diag_b" else m[key]
            m_new = jnp.maximum(m_old, row_max)
        p = jnp.exp(s - jnp.concatenate([m_new] * (size // LANES), axis=1))
        pv = jnp.dot(p.astype(BF16), vext, preferred_element_type=F32)
        if key not in m:
            acc[key], m[key] = pv, m_new
        else:
            acc_old = acc[key][t:] if kind == "diag_b" else acc[key]
            alpha = jnp.exp(m_old - m_new)
            acc_new = jnp.concatenate([alpha, alpha], axis=1) * acc_old + pv
            if kind == "diag_b":
                acc[key] = jnp.concatenate([acc[key][:t], acc_new], axis=0)
                m[key] = jnp.concatenate([m[key][:t], m_new], axis=0)
            else:
                acc[key], m[key] = acc_new, m_new
        if kind == "diag_b":
            a = acc.pop(key)
            o = a[:, :LANES] / a[:, LANES:]
            out = jnp.concatenate([_unstack_heads(o[:t], lane), _unstack_heads(o[t:], lane)], axis=0)
            o_ref[0, qi * t:(qi + 1) * t, pr * LANES:(pr + 1) * LANES] = out.astype(BF16)
            m.pop(key)
            qext.pop(key)

    n_tiles = len(tiles)
    for step in range(n_tiles + 1):
        if step < n_tiles:
            scores(step)
        if step >= 1:
            softmax_pv(step - 1)


def _fox_attention(q, k, v, aq, ak):
    b, s, d = q.shape
    width = ATT_PAIRS * LANES
    spec = pl.BlockSpec((1, s, width), lambda bi, g: (bi, 0, g))
    aspec = pl.BlockSpec((1, s, LANES), lambda bi, g: (bi, 0, 0))
    return pl.pallas_call(
        _fox_kernel,
        grid=(b, d // width),
        in_specs=[spec, aspec, spec, aspec, spec],
        out_specs=spec,
        out_shape=jax.ShapeDtypeStruct((b, s, d), BF16),
        compiler_params=_params(2),
        name="fox_attn",
    )(q, aq, k, ak, v)


def _mem_kv_kernel(mem_ref, g_ref, w_ref, gk_ref, mk_ref, mv_ref):
    mh = _rms(mem_ref[0], g_ref[...]).astype(BF16)
    kv = _mm(mh, w_ref)
    mk_ref[0] = _headnorm(kv[:, :D_MEM], gk_ref[...], MEM_HD).astype(BF16)
    mv_ref[0] = kv[:, D_MEM:].astype(BF16)


def _mem_kv(mem, g, w, gk):
    b, m, _ = mem.shape
    out = pl.BlockSpec((1, m, D_MEM), lambda i: (i, 0, 0))
    return pl.pallas_call(
        _mem_kv_kernel,
        grid=(b,),
        in_specs=[pl.BlockSpec((1, m, D_MODEL), lambda i: (i, 0, 0)), _const_spec((1, D_MODEL)),
                  _const_spec((D_MODEL, 2 * D_MEM)), _const_spec((1, D_MEM))],
        out_specs=[out, out],
        out_shape=[jax.ShapeDtypeStruct((b, m, D_MEM), BF16)] * 2,
        compiler_params=_params(1),
        name="mem_kv",
    )(mem, g, w, gk)


def _mem_attn_kernel(q_ref, k_ref, v_ref, o_ref):
    z = {}

    def scores(h):
        sl = slice(h * MEM_HD, (h + 1) * MEM_HD)
        z[h] = lax.dot_general(q_ref[0, :, sl], k_ref[0, :, sl], _NT,
                               preferred_element_type=F32) * MEM_HD ** -0.5

    def softmax_pv(h):
        sl = slice(h * MEM_HD, (h + 1) * MEM_HD)
        zh = z.pop(h)
        e = jnp.exp(zh - jnp.max(zh, axis=1, keepdims=True))
        p = e / jnp.sum(e, axis=1, keepdims=True)
        o_ref[0, :, sl] = jnp.dot(p.astype(BF16), v_ref[0, :, sl],
                                  preferred_element_type=F32).astype(BF16)

    for step in range(MEM_HEADS + 1):
        if step < MEM_HEADS:
            scores(step)
        if step >= 1:
            softmax_pv(step - 1)


def _mem_attention(q, k, v):
    b, s, d = q.shape
    m = k.shape[1]
    qspec = pl.BlockSpec((1, MEM_Q_TILE, d), lambda bi, qi: (bi, qi, 0))
    kvspec = pl.BlockSpec((1, m, d), lambda bi, qi: (bi, 0, 0))
    return pl.pallas_call(
        _mem_attn_kernel,
        grid=(b, s // MEM_Q_TILE),
        in_specs=[qspec, kvspec, kvspec],
        out_specs=qspec,
        out_shape=jax.ShapeDtypeStruct((b, s, d), BF16),
        compiler_params=_params(2),
        name="mem_attn",
    )(q, k, v)


def _merge_kernel(x_ref, g_ref, wgt_ref, osb_ref, ofx_ref, omem_ref,
                  wsb_ref, wfx_ref, wmem_ref, wout_ref, o_ref):
    x = x_ref[...]
    h = _rms(x, g_ref[...]).astype(BF16)
    merged = None
    for i, (o_br, w_br) in enumerate(((osb_ref, wsb_ref), (ofx_ref, wfx_ref), (omem_ref, wmem_ref))):
        gate = jax.nn.sigmoid(_mm_t(h, wgt_ref, slice(i * D_MODEL, (i + 1) * D_MODEL)))
        term = gate * _mm(o_br[...], w_br)
        merged = term if merged is None else merged + term
    o_ref[...] = x + _mm(merged.astype(BF16), wout_ref)


def _merge(x2, g, w_in_t, osb, ofx, omem, wsb, wfx, wmem, wout):
    t = x2.shape[0]
    n_gate = N_BRANCH * D_MODEL
    gate_rows = pl.BlockSpec((pl.Element(n_gate), pl.Element(D_MODEL)),
                             lambda i: (w_in_t.shape[0] - n_gate, 0),
                             pipeline_mode=pl.Buffered(1))
    row = lambda n: pl.BlockSpec((ROW_TILE, n), lambda i: (i, 0))
    return pl.pallas_call(
        _merge_kernel,
        grid=(t // ROW_TILE,),
        in_specs=[row(D_MODEL), _const_spec((1, D_MODEL)), gate_rows,
                  row(D_SB), row(D_FOX), row(D_MEM),
                  _const_spec(wsb.shape), _const_spec(wfx.shape), _const_spec(wmem.shape),
                  _const_spec(wout.shape)],
        out_specs=row(D_MODEL),
        out_shape=jax.ShapeDtypeStruct((t, D_MODEL), F32),
        compiler_params=_params(1),
        name="merge",
    )(x2, g, w_in_t, osb, ofx, omem, wsb, wfx, wmem, wout)


def _mlp_kernel(x_ref, g_ref, wup_ref, wdn_ref, o_ref):
    x = x_ref[...]
    h = _rms(x, g_ref[...]).astype(BF16)
    acc = x
    for c in range(D_FF // FF_CHUNK):
        sl = slice(c * FF_CHUNK, (c + 1) * FF_CHUNK)
        u = jnp.maximum(_mm(h, wup_ref, cols=sl), 0.0)
        acc = acc + _mm((u * u).astype(BF16), wdn_ref, rows=sl)
    o_ref[...] = acc


def _mlp(x2, g, wup, wdn):
    t = x2.shape[0]
    row = pl.BlockSpec((ROW_TILE, D_MODEL), lambda i: (i, 0))
    return pl.pallas_call(
        _mlp_kernel,
        grid=(t // ROW_TILE,),
        in_specs=[row, _const_spec((1, D_MODEL)), _const_spec(wup.shape), _const_spec(wdn.shape)],
        out_specs=row,
        out_shape=jax.ShapeDtypeStruct((t, D_MODEL), F32),
        compiler_params=_params(1),
        name="mlp",
    )(x2, g, wup, wdn)


def _layer(x, mem, g_mix, g_memn, w_in, b_forget, g_fq, g_fk, g_mq, g_mk, w_mem_kv,
           w_sb, w_fox, w_mem, w_out, g_mlp, w_up, w_dn):
    b, s, _ = x.shape
    t = b * s
    x2 = x.reshape(t, D_MODEL)
    n_qkv = 3 * D_SB + 3 * D_FOX
    w_in_t = w_in.T
    bf =jnp.pad(b_forget, (0, LANES - FOX_HEADS)).reshape(1, LANES)
    row = lambda a: a.reshape(1, -1)

    sbq, sbk, sbv, fxq, fxk, fxv, mq, lf = _proj(
        x2, row(g_mix), w_in_t, bf, row(jnp.tile(g_fq, FOX_HEADS)),
        row(jnp.tile(g_fk, FOX_HEADS)), row(jnp.tile(g_mq, MEM_HEADS)))
    to3 = lambda a: a.reshape(b, s, a.shape[-1])

    aq, ak = _forget(to3(lf))
    o_sb = _sb_attention(to3(sbq), to3(sbk), to3(sbv))
    o_fox = _fox_attention(to3(fxq), to3(fxk), to3(fxv), aq, ak)
    mk, mv = _mem_kv(mem, row(g_memn), w_mem_kv, row(jnp.tile(g_mk, MEM_HEADS)))
    o_mem = _mem_attention(to3(mq), mk, mv)

    x1 = _merge(x2, row(g_mix), w_in_t, o_sb.reshape(t, D_SB), o_fox.reshape(t, D_FOX),
                o_mem.reshape(t, D_MEM), w_sb, w_fox, w_mem, w_out)
    out = _mlp(x1, row(g_mlp), w_up, w_dn)
    return out.reshape(b, s, D_MODEL)


def kernel(x, mem, g_mix_norm, g_mem_norm, w_in, b_forget, g_fox_q, g_fox_k, g_mem_q, g_mem_k,
           w_mem_kv, w_branch_sb, w_branch_fox, w_branch_mem, w_out, g_mlp_norm, w_ff_up, w_ff_down):
    for l in range(w_in.shape[0]):
        x = _layer(x, mem, g_mix_norm[l], g_mem_norm[l], w_in[l], b_forget[l], g_fox_q[l], g_fox_k[l],
                   g_mem_q[l], g_mem_k[l], w_mem_kv[l], w_branch_sb[l], w_branch_fox[l],
                   w_branch_mem[l], w_out[l], g_mlp_norm[l], w_ff_up[l], w_ff_down[l])
    return x
```

```python
import numpy as np
import jax
import jax.numpy as jnp
from jax import lax
from jax.experimental import pallas as pl
from jax.experimental.pallas import tpu as pltpu

D_MODEL = 1024
HD = 64
SB_HEADS = 8
FOX_HEADS = 8
MEM_HEADS = 4
MEM_HD = 128
D_SB = SB_HEADS * HD
D_FOX = FOX_HEADS * HD
D_MEM = MEM_HEADS * MEM_HD
N_BRANCH = 3
D_FF = 4 * D_MODEL
EPS = 1e-6
NEG_INF = -1e30

LANES = 128
AUG_STRIDE = 8
ROW_TILE = 512
ATT_Q_TILE = 512
ATT_K_TILE = 256
ATT_PAIRS = 2
SB_EXIT = 110.0
MEM_Q_TILE = 512
FF_CHUNK = 512
VMEM_LIMIT = 56 * 1024 * 1024

F32 = jnp.float32
BF16 = jnp.bfloat16
_NT = (((1,), (1,)), ((), ()))
_LOG2E = 1.4426950408889634


def _const_spec(shape):
    nd = len(shape)
    return pl.BlockSpec(shape, lambda *_: (0,) * nd, pipeline_mode=pl.Buffered(1))


def _params(n_axes):
    return pltpu.CompilerParams(
        dimension_semantics=("arbitrary",) * n_axes, vmem_limit_bytes=VMEM_LIMIT)


def _rms(x, g):
    ms = jnp.sum(x * x, axis=-1, keepdims=True) * (1.0 / x.shape[-1])
    return (x * lax.rsqrt(ms + EPS)) * g


def _headnorm(p, g, hd):
    rows, n = p.shape
    lane = lax.broadcasted_iota(jnp.int32, (rows, LANES), 1)
    outs = []
    for j in range(n // LANES):
        c = p[:, j * LANES:(j + 1) * LANES]
        c2 = c * c
        if hd == LANES:
            ms = jnp.sum(c2, axis=1, keepdims=True) * (1.0 / LANES)
        else:
            lo = jnp.sum(jnp.where(lane < hd, c2, 0.0), axis=1, keepdims=True)
            hi = jnp.sum(jnp.where(lane >= hd, c2, 0.0), axis=1, keepdims=True)
            ms = jnp.where(lane < hd, lo, hi) * (1.0 / hd)
        outs.append(c * lax.rsqrt(ms + EPS))
    return jnp.concatenate(outs, axis=1) * g


def _log_sigmoid(x):
    return jnp.minimum(x, 0.0) - jnp.log(1.0 + jnp.exp(-jnp.abs(x)))


def _mm(a, w_ref, rows=slice(None), cols=slice(None)):
    return jnp.dot(a, w_ref[rows, cols].astype(BF16), preferred_element_type=F32)


def _mm_t(a, wt_ref, rows):
    return lax.dot_general(a, wt_ref[rows, :].astype(BF16), _NT, preferred_element_type=F32)


def _proj_kernel(x_ref, g_ref, wt_ref, bf_ref, gq_ref, gk_ref, gm_ref,
                 sbq_ref, sbk_ref, sbv_ref, fxq_ref, fxk_ref, fxv_ref, mq_ref, lf_ref):
    h = _rms(x_ref[...], g_ref[...]).astype(BF16)

    def mm(c0, n):
        return _mm_t(h, wt_ref, slice(c0, c0 + n))

    sbq_ref[...] = (mm(0, D_SB) * HD ** -0.5).astype(BF16)
    sbk_ref[...] = mm(D_SB, D_SB).astype(BF16)
    sbv_ref[...] = mm(2 * D_SB, D_SB).astype(BF16)
    c0 = 3 * D_SB
    fxq_ref[...] = (_headnorm(mm(c0, D_FOX), gq_ref[...], HD) * HD ** -0.5).astype(BF16)
    fxk_ref[...] = _headnorm(mm(c0 + D_FOX, D_FOX), gk_ref[...], HD).astype(BF16)
    fxv_ref[...] = mm(c0 + 2 * D_FOX, D_FOX).astype(BF16)
    c_f = c0 + 3 * D_FOX
    mq_ref[...] = _headnorm(mm(c_f + FOX_HEADS, D_MEM), gm_ref[...], MEM_HD).astype(BF16)
    lf_ref[...] = _log_sigmoid(mm(c_f, LANES) + bf_ref[...])


def _proj(x2, g, w_in_t, bf, gq, gk, gm):
    t = x2.shape[0]
    n_rows = 3 * D_SB + 3 * D_FOX + FOX_HEADS + D_MEM
    row = lambda n: pl.BlockSpec((ROW_TILE, n), lambda i: (i, 0))
    out_shapes = [jax.ShapeDtypeStruct((t, D_SB), BF16)] * 7 + [jax.ShapeDtypeStruct((t, LANES), F32)]
    return pl.pallas_call(
        _proj_kernel,
        grid=(t // ROW_TILE,),
        in_specs=[row(D_MODEL), _const_spec((1, D_MODEL)), _const_spec((n_rows, D_MODEL)),
                  _const_spec((1, LANES)), _const_spec((1, D_FOX)), _const_spec((1, D_FOX)),
                  _const_spec((1, D_MEM))],
        out_specs=[row(D_SB)] * 7 + [row(LANES)],
        out_shape=out_shapes,
        compiler_params=_params(1),
        name="proj",
    )(x2, g, w_in_t, bf, gq, gk, gm)


def _forget_kernel(lf_ref, p_ref, oq_ref, ok_ref, aq_ref, ak_ref):
    f = lf_ref[0]
    s_len = f.shape[0]
    row = lax.broadcasted_iota(jnp.int32, f.shape, 0)
    sh = 1
    while sh < s_len:
        f = f + jnp.where(row >= sh, pltpu.roll(f, sh, 0), 0.0)
        sh *= 2
    f_hi = f.astype(BF16)
    r1 = f - f_hi.astype(F32)
    f_mid = r1.astype(BF16)
    f_lo = (r1 - f_mid.astype(F32)).astype(BF16)
    pieces = jnp.concatenate([f_hi, f_mid, f_lo], axis=1)
    placed = jnp.dot(pieces, p_ref[...], preferred_element_type=F32)
    aq_ref[0] = (placed[:, :LANES] + oq_ref[...]).astype(BF16)
    ak_ref[0] = (placed[:, LANES:] + ok_ref[...]).astype(BF16)


def _forget_constants():
    p = np.zeros((3 * LANES, 2 * LANES), np.float32)
    oq = np.zeros((1, LANES), np.float32)
    ok = np.zeros((1, LANES), np.float32)
    for h in range(FOX_HEADS):
        for j in range(3):
            p[j * LANES + h, h * AUG_STRIDE + j] = 1.0
            p[j * LANES + h, LANES + h * AUG_STRIDE + 3 + j] = -1.0
            oq[0, h * AUG_STRIDE + 3 + j] = 1.0
            ok[0, h * AUG_STRIDE + j] = 1.0
    return jnp.asarray(p, BF16), jnp.asarray(oq), jnp.asarray(ok)


def _forget(lf3):
    b, s, _ = lf3.shape
    p, oq, ok = _forget_constants()
    blk = pl.BlockSpec((1, s, LANES), lambda i: (i, 0, 0))
    return pl.pallas_call(
        _forget_kernel,
        grid=(b,),
        in_specs=[blk, _const_spec(p.shape), _const_spec(oq.shape), _const_spec(ok.shape)],
        out_specs=[blk, blk],
        out_shape=[jax.ShapeDtypeStruct((b, s, LANES), BF16)] * 2,
        compiler_params=_params(1),
        name="forget",
    )(lf3, p, oq, ok)


def _stack_heads(x, lane):
    zero = jnp.zeros_like(x)
    return jnp.concatenate([jnp.where(lane < HD, x, zero), jnp.where(lane >= HD, x, zero)], axis=0)


def _unstack_heads(x, lane):
    rows = x.shape[0] // 2
    return jnp.where(lane < HD, x[:rows], x[rows:])


def _softplus(z):
    return jnp.maximum(z, 0.0) + jnp.log(1.0 + jnp.exp2(jnp.abs(z) * (-_LOG2E)))


def _sb_kernel(q_ref, k_ref, v_ref, m_ref, o_ref, flag_ref, acc_ref, carry_ref):
    s_len = q_ref.shape[1]
    t = ATT_K_TILE
    nb = s_len // t
    lane = lax.broadcasted_iota(jnp.int32, (t, LANES), 1)
    tri = (lax.broadcasted_iota(jnp.int32, (2 * t, t), 1)
           < (lax.broadcasted_iota(jnp.int32, (2 * t, t), 0) & (t - 1)))
    msum = m_ref[...]
    reps = t // LANES
    order = [(pr, j) for pr in range(ATT_PAIRS) for j in reversed(range(nb))]
    z, sp, c, acc_diag, carry_diag = {}, {}, {}, {}, {}

    def qblock(pr, j):
        return _stack_heads(q_ref[0, j * t:(j + 1) * t, pr * LANES:(pr + 1) * LANES], lane)

    def scores(g):
        pr, j = order[g]
        lhs = qblock(pr, j) if j == nb - 1 else jnp.concatenate([qblock(pr, j), qblock(pr, j + 1)], axis=0)
        z[g] = lax.dot_general(lhs, k_ref[0, j * t:(j + 1) * t, pr * LANES:(pr + 1) * LANES], _NT,
                               preferred_element_type=F32)

    def suffix_sums(g):
        s = _softplus(z[g])
        top = jnp.where(tri, s[:2 * t], 0.0)
        s = top if order[g][1] == nb - 1 else jnp.concatenate([top, s[2 * t:]], axis=0)
        sp[g] = s
        z[g] = z[g] - s
        c[g] = jnp.dot(s.astype(BF16), msum, preferred_element_type=F32)

    def weights_pv(g):
        pr, j = order[g]
        ls = slice(pr * LANES, (pr + 1) * LANES)
        zc = z.pop(g) - c.pop(g)
        w = jnp.where(tri, jnp.exp(zc[:2 * t]), 0.0)
        if j < nb - 1:
            w_prev = jnp.exp(zc[2 * t:] - jnp.concatenate([carry_diag[(pr, j + 1)]] * reps, axis=1))
            w = jnp.concatenate([w, w_prev], axis=0)
        pv = jnp.dot(w.astype(BF16), v_ref[0, j * t:(j + 1) * t, ls], preferred_element_type=F32)
        rs = jnp.sum(sp.pop(g), axis=1, keepdims=True)
        if j < nb - 1:
            done = acc_diag.pop((pr, j + 1)) + pv[2 * t:]
            o_ref[0, (j + 1) * t:(j + 2) * t, ls] = _unstack_heads(done, lane).astype(BF16)
            total = carry_diag.pop((pr, j + 1)) + rs[2 * t:]
            if j >= 1:
                flag_ref[pr * nb + j + 1] = jnp.min(total)
        if j == 0:
            o_ref[0, 0:t, ls] = _unstack_heads(pv[:2 * t], lane).astype(BF16)
        else:
            acc_diag[(pr, j)] = pv[:2 * t]
            carry_diag[(pr, j)] = jnp.broadcast_to(rs[:2 * t], (2 * t, LANES))

    n_groups = len(order)
    for step in range(n_groups + 2):
        if step < n_groups:
            scores(step)
        if 1 <= step <= n_groups:
            suffix_sums(step - 1)
        if step >= 2:
            weights_pv(step - 2)

    def recompute_block(pr, jb):
        ls = slice(pr * LANES, (pr + 1) * LANES)
        rows = pl.ds(pl.multiple_of(jb * t, t), t)
        qb = _stack_heads(q_ref[0, rows, ls], lane)
        acc_ref[...] = jnp.zeros_like(acc_ref)
        carry_ref[...] = jnp.zeros_like(carry_ref)

        def tile(kb, diag):
            keys = pl.ds(pl.multiple_of(kb * t, t), t)
            zz = lax.dot_general(qb, k_ref[0, keys, ls], _NT, preferred_element_type=F32)
            s = _softplus(zz)
            if diag:
                s = jnp.where(tri, s, 0.0)
            cc = jnp.dot(s.astype(BF16), msum, preferred_element_type=F32)
            w = jnp.exp(zz - s - cc - jnp.concatenate([carry_ref[...]] * reps, axis=1))
            if diag:
                w = jnp.where(tri, w, 0.0)
            acc_ref[...] += jnp.dot(w.astype(BF16), v_ref[0, keys, ls], preferred_element_type=F32)
            carry_ref[...] += jnp.sum(s, axis=1, keepdims=True)

        tile(jb, True)

        def body(i, _):
            tile(jb - 1 - i, False)
            return 0

        lax.fori_loop(0, jb, body, 0)
        o_ref[0, rows, ls] = _unstack_heads(acc_ref[...], lane).astype(BF16)

    for pr in range(ATT_PAIRS):
        def redo(jb, _, pr=pr):
            pl.when(flag_ref[pr * nb + jb] < SB_EXIT)(lambda: recompute_block(pr, jb))
            return 0

        lax.fori_loop(2, nb, redo, 0)


def _sb_attention(q, k, v):
    b, s, d = q.shape
    t = ATT_K_TILE
    width = ATT_PAIRS * LANES
    ii = np.arange(t)
    msum = jnp.asarray(ii[:, None] > ii[None, :], BF16)
    spec = pl.BlockSpec((1, s, width), lambda bi, g: (bi, 0, g))
    return pl.pallas_call(
        _sb_kernel,
        grid=(b, d // width),
        in_specs=[spec, spec, spec, _const_spec((t, t))],
        out_specs=spec,
        out_shape=jax.ShapeDtypeStruct((b, s, d), BF16),
        scratch_shapes=[pltpu.SMEM((ATT_PAIRS * (s // t),), F32), pltpu.VMEM((2 * t, LANES), F32),
                        pltpu.VMEM((2 * t, LANES), F32)],
        compiler_params=_params(2),
        name="sb_attn",
    )(q, k, v, msum)


def _fox_kernel(q_ref, aq_ref, k_ref, ak_ref, v_ref, o_ref):
    s_len = q_ref.shape[1]
    t = ATT_Q_TILE
    hb = t // 2
    lane = lax.broadcasted_iota(jnp.int32, (hb, LANES), 1)
    group = lane // AUG_STRIDE
    tri = (lax.broadcasted_iota(jnp.int32, (t, hb), 1)
           <= (lax.broadcasted_iota(jnp.int32, (t, hb), 0) & (hb - 1)))
    ones = jnp.ones((t, LANES), BF16)
    tiles = []
    for pr in range(ATT_PAIRS):
        for qi in range(s_len // t):
            tiles += [(pr, qi, "full", kb * t, t) for kb in range(qi)]
            tiles += [(pr, qi, "diag_all", qi * t, hb), (pr, qi, "diag_b", qi * t + hb, hb)]
    qext, kvext, z, m, acc = {}, {}, {}, {}, {}

    def stacked_q(pr, qi):
        ls = slice(pr * LANES, (pr + 1) * LANES)
        pair = ATT_PAIRS * pl.program_id(1) + pr
        parts = []
        for half in range(2):
            rows = slice(qi * t + half * hb, qi * t + (half + 1) * hb)
            aq = aq_ref[0, rows, :]
            zero = jnp.zeros_like(aq)
            aug = jnp.concatenate([jnp.where(group == 2 * pair, aq, zero),
                                   jnp.where(group == 2 * pair + 1, aq, zero)], axis=0)
            parts.append(jnp.concatenate([_stack_heads(q_ref[0, rows, ls], lane), aug], axis=1))
        return jnp.concatenate(parts, axis=0)

    def keys_values(pr, start, size):
        if (pr, start, size) not in kvext:
            ls = slice(pr * LANES, (pr + 1) * LANES)
            keys = slice(start, start + size)
            kext = jnp.concatenate([k_ref[0, keys, ls], ak_ref[0, keys, :]], axis=1)
            vext = jnp.concatenate([v_ref[0, keys, ls], ones[:size]], axis=1)
            kvext[(pr, start, size)] = (kext, vext)
        return kvext[(pr, start, size)]

    def scores(n):
        pr, qi, kind, start, size = tiles[n]
        if (pr, qi) not in qext:
            qext[(pr, qi)] = stacked_q(pr, qi)
        lhs = qext[(pr, qi)]
        kext = keys_values(pr, start, size)[0]
        if kind == "diag_b":
            s = lax.dot_general(lhs[t:], kext, _NT, preferred_element_type=F32)
            s = jnp.where(tri, s, NEG_INF)
        else:
            s = lax.dot_general(lhs, kext, _NT, preferred_element_type=F32)
            if kind == "diag_all":
                s = jnp.concatenate([jnp.where(tri, s[:t], NEG_INF), s[t:]], axis=0)
        z[n] = s

    def softmax_pv(n):
        pr, qi, kind, start, size = tiles[n]
        key = (pr, qi)
        s = z.pop(n)
        vext = keys_values(pr, start, size)[1]
        row_max = jnp.max(s, axis=1, keepdims=True)
        if key not in m:
            m_new = jnp.broadcast_to(row_max, (s.shape[0], LANES))
        else:
            m_old = m[key][t:] if kind == "diag_b" else m[key]
            m_new = jnp.maximum(m_old, row_max)
        p = jnp.exp(s - jnp.concatenate([m_new] * (size // LANES), axis=1))
        pv = jnp.dot(p.astype(BF16), vext, preferred_element_type=F32)
        if key not in m:
            acc[key], m[key] = pv, m_new
        else:
            acc_old = acc[key][t:] if kind == "diag_b" else acc[key]
            alpha = jnp.exp(m_old - m_new)
            acc_new = jnp.concatenate([alpha, alpha], axis=1) * acc_old + pv
            if kind == "diag_b":
                acc[key] = jnp.concatenate([acc[key][:t], acc_new], axis=0)
                m[key] = jnp.concatenate([m[key][:t], m_new], axis=0)
            else:
                acc[key], m[key] = acc_new, m_new
        if kind == "diag_b":
            a = acc.pop(key)
            o = a[:, :LANES] / a[:, LANES:]
            out = jnp.concatenate([_unstack_heads(o[:t], lane), _unstack_heads(o[t:], lane)], axis=0)
            o_ref[0, qi * t:(qi + 1) * t, pr * LANES:(pr + 1) * LANES] = out.astype(BF16)
            m.pop(key)
            qext.pop(key)

    n_tiles = len(tiles)
    for step in range(n_tiles + 1):
        if step < n_tiles:
            scores(step)
        if step >= 1:
            softmax_pv(step - 1)


def _fox_attention(q, k, v, aq, ak):
    b, s, d = q.shape
    width = ATT_PAIRS * LANES
    spec = pl.BlockSpec((1, s, width), lambda bi, g: (bi, 0, g))
    aspec = pl.BlockSpec((1, s, LANES), lambda bi, g: (bi, 0, 0))
    return pl.pallas_call(
        _fox_kernel,
        grid=(b, d // width),
        in_specs=[spec, aspec, spec, aspec, spec],
        out_specs=spec,
        out_shape=jax.ShapeDtypeStruct((b, s, d), BF16),
        compiler_params=_params(2),
        name="fox_attn",
    )(q, aq, k, ak, v)


def _mem_kv_kernel(mem_ref, g_ref, w_ref, gk_ref, mk_ref, mv_ref):
    mh = _rms(mem_ref[0], g_ref[...]).astype(BF16)
    kv = _mm(mh, w_ref)
    mk_ref[0] = _headnorm(kv[:, :D_MEM], gk_ref[...], MEM_HD).astype(BF16)
    mv_ref[0] = kv[:, D_MEM:].astype(BF16)


def _mem_kv(mem, g, w, gk):
    b, m, _ = mem.shape
    out = pl.BlockSpec((1, m, D_MEM), lambda i: (i, 0, 0))
    return pl.pallas_call(
        _mem_kv_kernel,
        grid=(b,),
        in_specs=[pl.BlockSpec((1, m, D_MODEL), lambda i: (i, 0, 0)), _const_spec((1, D_MODEL)),
                  _const_spec((D_MODEL, 2 * D_MEM)), _const_spec((1, D_MEM))],
        out_specs=[out, out],
        out_shape=[jax.ShapeDtypeStruct((b, m, D_MEM), BF16)] * 2,
        compiler_params=_params(1),
        name="mem_kv",
    )(mem, g, w, gk)


def _mem_attn_kernel(q_ref, k_ref, v_ref, o_ref):
    z = {}

    def scores(h):
        sl = slice(h * MEM_HD, (h + 1) * MEM_HD)
        z[h] = lax.dot_general(q_ref[0, :, sl], k_ref[0, :, sl], _NT,
                               preferred_element_type=F32) * MEM_HD ** -0.5

    def softmax_pv(h):
        sl = slice(h * MEM_HD, (h + 1) * MEM_HD)
        zh = z.pop(h)
        e = jnp.exp(zh - jnp.max(zh, axis=1, keepdims=True))
        p = e / jnp.sum(e, axis=1, keepdims=True)
        o_ref[0, :, sl] = jnp.dot(p.astype(BF16), v_ref[0, :, sl],
                                  preferred_element_type=F32).astype(BF16)

    for step in range(MEM_HEADS + 1):
        if step < MEM_HEADS:
            scores(step)
        if step >= 1:
            softmax_pv(step - 1)


def _mem_attention(q, k, v):
    b, s, d = q.shape
    m = k.shape[1]
    qspec = pl.BlockSpec((1, MEM_Q_TILE, d), lambda bi, qi: (bi, qi, 0))
    kvspec = pl.BlockSpec((1, m, d), lambda bi, qi: (bi, 0, 0))
    return pl.pallas_call(
        _mem_attn_kernel,
        grid=(b, s // MEM_Q_TILE),
        in_specs=[qspec, kvspec, kvspec],
        out_specs=qspec,
        out_shape=jax.ShapeDtypeStruct((b, s, d), BF16),
        compiler_params=_params(2),
        name="mem_attn",
    )(q, k, v)


def _merge_kernel(x_ref, g_ref, wgt_ref, osb_ref, ofx_ref, omem_ref,
                  wsb_ref, wfx_ref, wmem_ref, wout_ref, o_ref):
    x = x_ref[...]
    h = _rms(x, g_ref[...]).astype(BF16)
    merged = None
    for i, (o_br, w_br) in enumerate(((osb_ref, wsb_ref), (ofx_ref, wfx_ref), (omem_ref, wmem_ref))):
        gate = jax.nn.sigmoid(_mm_t(h, wgt_ref, slice(i * D_MODEL, (i + 1) * D_MODEL)))
        term = gate * _mm(o_br[...], w_br)
        merged = term if merged is None else merged + term
    o_ref[...] = x + _mm(merged.astype(BF16), wout_ref)


def _merge(x2, g, w_in_t, osb, ofx, omem, wsb, wfx, wmem, wout):
    t = x2.shape[0]
    n_gate = N_BRANCH * D_MODEL
    gate_rows = pl.BlockSpec((pl.Element(n_gate), pl.Element(D_MODEL)),
                             lambda i: (w_in_t.shape[0] - n_gate, 0),
                             pipeline_mode=pl.Buffered(1))
    row = lambda n: pl.BlockSpec((ROW_TILE, n), lambda i: (i, 0))
    return pl.pallas_call(
        _merge_kernel,
        grid=(t // ROW_TILE,),
        in_specs=[row(D_MODEL), _const_spec((1, D_MODEL)), gate_rows,
                  row(D_SB), row(D_FOX), row(D_MEM),
                  _const_spec(wsb.shape), _const_spec(wfx.shape), _const_spec(wmem.shape),
                  _const_spec(wout.shape)],
        out_specs=row(D_MODEL),
        out_shape=jax.ShapeDtypeStruct((t, D_MODEL), F32),
        compiler_params=_params(1),
        name="merge",
    )(x2, g, w_in_t, osb, ofx, omem, wsb, wfx, wmem, wout)


def _mlp_kernel(x_ref, g_ref, wup_ref, wdn_ref, o_ref):
    x = x_ref[...]
    h = _rms(x, g_ref[...]).astype(BF16)
    acc = x
    for c in range(D_FF // FF_CHUNK):
        sl = slice(c * FF_CHUNK, (c + 1) * FF_CHUNK)
        u = jnp.maximum(_mm(h, wup_ref, cols=sl), 0.0)
        acc = acc + _mm((u * u).astype(BF16), wdn_ref, rows=sl)
    o_ref[...] = acc


def _mlp(x2, g, wup, wdn):
    t = x2.shape[0]
    row = pl.BlockSpec((ROW_TILE, D_MODEL), lambda i: (i, 0))
    return pl.pallas_call(
        _mlp_kernel,
        grid=(t // ROW_TILE,),
        in_specs=[row, _const_spec((1, D_MODEL)), _const_spec(wup.shape), _const_spec(wdn.shape)],
        out_specs=row,
        out_shape=jax.ShapeDtypeStruct((t, D_MODEL), F32),
        compiler_params=_params(1),
        name="mlp",
    )(x2, g, wup, wdn)


def _layer(x, mem, g_mix, g_memn, w_in, b_forget, g_fq, g_fk, g_mq, g_mk, w_mem_kv,
           w_sb, w_fox, w_mem, w_out, g_mlp, w_up, w_dn):
    b, s, _ = x.shape
    t = b * s
    x2 = x.reshape(t, D_MODEL)
    n_qkv = 3 * D_SB + 3 * D_FOX
    w_in_t = w_in.T
    bf =jnp.pad(b_forget, (0, LANES - FOX_HEADS)).reshape(1, LANES)
    row = lambda a: a.reshape(1, -1)

    sbq, sbk, sbv, fxq, fxk, fxv, mq, lf = _proj(
        x2, row(g_mix), w_in_t, bf, row(jnp.tile(g_fq, FOX_HEADS)),
        row(jnp.tile(g_fk, FOX_HEADS)), row(jnp.tile(g_mq, MEM_HEADS)))
    to3 = lambda a: a.reshape(b, s, a.shape[-1])

    aq, ak = _forget(to3(lf))
    o_sb = _sb_attention(to3(sbq), to3(sbk), to3(sbv))
    o_fox = _fox_attention(to3(fxq), to3(fxk), to3(fxv), aq, ak)
    mk, mv = _mem_kv(mem, row(g_memn), w_mem_kv, row(jnp.tile(g_mk, MEM_HEADS)))
    o_mem = _mem_attention(to3(mq), mk, mv)

    x1 = _merge(x2, row(g_mix), w_in_t, o_sb.reshape(t, D_SB), o_fox.reshape(t, D_FOX),
                o_mem.reshape(t, D_MEM), w_sb, w_fox, w_mem, w_out)
    out = _mlp(x1, row(g_mlp), w_up, w_dn)
    return out.reshape(b, s, D_MODEL)


def kernel(x, mem, g_mix_norm, g_mem_norm, w_in, b_forget, g_fox_q, g_fox_k, g_mem_q, g_mem_k,
           w_mem_kv, w_branch_sb, w_branch_fox, w_branch_mem, w_out, g_mlp_norm, w_ff_up, w_ff_down):
    for l in range(w_in.shape[0]):
        x = _layer(x, mem, g_mix_norm[l], g_mem_norm[l], w_in[l], b_forget[l], g_fox_q[l], g_fox_k[l],
                   g_mem_q[l], g_mem_k[l], w_mem_kv[l], w_branch_sb[l], w_branch_fox[l],
                   w_branch_mem[l], w_out[l], g_mlp_norm[l], w_ff_up[l], w_ff_down[l])
    return x
```

```python
import functools

import numpy as np
import jax
import jax.numpy as jnp
from jax import lax
from jax.experimental import pallas as pl
from jax.experimental.pallas import tpu as pltpu

D_MODEL = 1024
HD = 64
SB_HEADS = 8
FOX_HEADS = 8
MEM_HEADS = 4
MEM_HD = 128
D_SB = SB_HEADS * HD
D_FOX = FOX_HEADS * HD
D_MEM = MEM_HEADS * MEM_HD
N_BRANCH = 3
D_FF = 4 * D_MODEL
EPS = 1e-6
NEG_INF = -1e30

LANES = 128
AUG_STRIDE = 8
ROW_TILE = 512
ATT_Q_TILE = 512
ATT_K_TILE = 256
ATT_PAIRS = 2
SB_EXIT = 110.0
FF_CHUNK = 512
VMEM_LIMIT = 56 * 1024 * 1024

F32 = jnp.float32
BF16 = jnp.bfloat16
_NT = (((1,), (1,)), ((), ()))
_LOG2E = 1.4426950408889634


def _const_spec(shape):
    nd = len(shape)
    return pl.BlockSpec(shape, lambda *_: (0,) * nd, pipeline_mode=pl.Buffered(1))


def _params(n_axes):
    return pltpu.CompilerParams(
        dimension_semantics=("arbitrary",) * n_axes, vmem_limit_bytes=VMEM_LIMIT)


def _rms(x, g):
    ms = jnp.sum(x * x, axis=-1, keepdims=True) * (1.0 / x.shape[-1])
    return (x * lax.rsqrt(ms + EPS)) * g


def _headnorm(p, g, hd):
    rows, n = p.shape
    lane = lax.broadcasted_iota(jnp.int32, (rows, LANES), 1)
    outs = []
    for j in range(n // LANES):
        c = p[:, j * LANES:(j + 1) * LANES]
        c2 = c * c
        if hd == LANES:
            ms = jnp.sum(c2, axis=1, keepdims=True) * (1.0 / LANES)
        else:
            lo = jnp.sum(jnp.where(lane < hd, c2, 0.0), axis=1, keepdims=True)
            hi = jnp.sum(jnp.where(lane >= hd, c2, 0.0), axis=1, keepdims=True)
            ms = jnp.where(lane < hd, lo, hi) * (1.0 / hd)
        outs.append(c * lax.rsqrt(ms + EPS))
    return jnp.concatenate(outs, axis=1) * g


def _log_sigmoid(x):
    return jnp.minimum(x, 0.0) - jnp.log(1.0 + jnp.exp(-jnp.abs(x)))


def _mm(a, w_ref, rows=slice(None), cols=slice(None)):
    return jnp.dot(a, w_ref[rows, cols].astype(BF16), preferred_element_type=F32)


def _mm_t(a, wt_ref, rows):
    return lax.dot_general(a, wt_ref[rows, :].astype(BF16), _NT, preferred_element_type=F32)


def _proj_kernel(x_ref, g_ref, wt_ref, bf_ref, gq_ref, gk_ref, gm_ref,
                 sbq_ref, sbk_ref, sbv_ref, fxq_ref, fxk_ref, fxv_ref, mq_ref, lf_ref):
    h = _rms(x_ref[...], g_ref[...]).astype(BF16)

    def mm(c0, n):
        return _mm_t(h, wt_ref, slice(c0, c0 + n))

    sbq_ref[...] = (mm(0, D_SB) * HD ** -0.5).astype(BF16)
    sbk_ref[...] = mm(D_SB, D_SB).astype(BF16)
    sbv_ref[...] = mm(2 * D_SB, D_SB).astype(BF16)
    c0 = 3 * D_SB
    fxq_ref[...] = (_headnorm(mm(c0, D_FOX), gq_ref[...], HD) * HD ** -0.5).astype(BF16)
    fxk_ref[...] = _headnorm(mm(c0 + D_FOX, D_FOX), gk_ref[...], HD).astype(BF16)
    fxv_ref[...] = mm(c0 + 2 * D_FOX, D_FOX).astype(BF16)
    c_f = c0 + 3 * D_FOX
    mq_ref[...] = _headnorm(mm(c_f + FOX_HEADS, D_MEM), gm_ref[...], MEM_HD).astype(BF16)
    lf_ref[...] = _log_sigmoid(mm(c_f, LANES) + bf_ref[...])


def _proj(x2, g, w_in_t, bf, gq, gk, gm):
    t = x2.shape[0]
    n_rows = 3 * D_SB + 3 * D_FOX + FOX_HEADS + D_MEM
    row = lambda n: pl.BlockSpec((ROW_TILE, n), lambda i: (i, 0))
    out_shapes = [jax.ShapeDtypeStruct((t, D_SB), BF16)] * 7 + [jax.ShapeDtypeStruct((t, LANES), F32)]
    return pl.pallas_call(
        _proj_kernel,
        grid=(t // ROW_TILE,),
        in_specs=[row(D_MODEL), _const_spec((1, D_MODEL)), _const_spec((n_rows, D_MODEL)),
                  _const_spec((1, LANES)), _const_spec((1, D_FOX)), _const_spec((1, D_FOX)),
                  _const_spec((1, D_MEM))],
        out_specs=[row(D_SB)] * 7 + [row(LANES)],
        out_shape=out_shapes,
        compiler_params=_params(1),
        name="proj",
    )(x2, g, w_in_t, bf, gq, gk, gm)


def _forget_kernel(lf_ref, p_ref, oq_ref, ok_ref, aq_ref, ak_ref):
    f = lf_ref[0]
    s_len = f.shape[0]
    row = lax.broadcasted_iota(jnp.int32, f.shape, 0)
    sh = 1
    while sh < s_len:
        f = f + jnp.where(row >= sh, pltpu.roll(f, sh, 0), 0.0)
        sh *= 2
    f_hi = f.astype(BF16)
    r1 = f - f_hi.astype(F32)
    f_mid = r1.astype(BF16)
    f_lo = (r1 - f_mid.astype(F32)).astype(BF16)
    pieces = jnp.concatenate([f_hi, f_mid, f_lo], axis=1)
    placed = jnp.dot(pieces, p_ref[...], preferred_element_type=F32)
    aq_ref[0] = (placed[:, :LANES] + oq_ref[...]).astype(BF16)
    ak_ref[0] = (placed[:, LANES:] + ok_ref[...]).astype(BF16)


def _forget_constants():
    p = np.zeros((3 * LANES, 2 * LANES), np.float32)
    oq = np.zeros((1, LANES), np.float32)
    ok = np.zeros((1, LANES), np.float32)
    for h in range(FOX_HEADS):
        for j in range(3):
            p[j * LANES + h, h * AUG_STRIDE + j] = 1.0
            p[j * LANES + h, LANES + h * AUG_STRIDE + 3 + j] = -1.0
            oq[0, h * AUG_STRIDE + 3 + j] = 1.0
            ok[0, h * AUG_STRIDE + j] = 1.0
    return jnp.asarray(p, BF16), jnp.asarray(oq), jnp.asarray(ok)


def _forget(lf3):
    b, s, _ = lf3.shape
    p, oq, ok = _forget_constants()
    blk = pl.BlockSpec((1, s, LANES), lambda i: (i, 0, 0))
    return pl.pallas_call(
        _forget_kernel,
        grid=(b,),
        in_specs=[blk, _const_spec(p.shape), _const_spec(oq.shape), _const_spec(ok.shape)],
        out_specs=[blk, blk],
        out_shape=[jax.ShapeDtypeStruct((b, s, LANES), BF16)] * 2,
        compiler_params=_params(1),
        name="forget",
    )(lf3, p, oq, ok)


def _stack_heads(x, lane):
    zero = jnp.zeros_like(x)
    return jnp.concatenate([jnp.where(lane < HD, x, zero), jnp.where(lane >= HD, x, zero)], axis=0)


def _unstack_heads(x, lane):
    rows = x.shape[0] // 2
    return jnp.where(lane < HD, x[:rows], x[rows:])


def _softplus(z):
    return jnp.maximum(z, 0.0) + jnp.log(1.0 + jnp.exp2(jnp.abs(z) * (-_LOG2E)))


def _sb_kernel(q_ref, k_ref, v_ref, m_ref, o_ref, flag_ref, acc_ref, carry_ref):
    s_len = q_ref.shape[1]
    t = ATT_K_TILE
    nb = s_len // t
    lane = lax.broadcasted_iota(jnp.int32, (t, LANES), 1)
    tri = (lax.broadcasted_iota(jnp.int32, (2 * t, t), 1)
           < (lax.broadcasted_iota(jnp.int32, (2 * t, t), 0) & (t - 1)))
    msum = m_ref[...]
    reps = t // LANES
    order = [(pr, j) for pr in range(ATT_PAIRS) for j in reversed(range(nb))]
    z, sp, c, acc_diag, carry_diag = {}, {}, {}, {}, {}

    def qblock(pr, j):
        return _stack_heads(q_ref[0, j * t:(j + 1) * t, pr * LANES:(pr + 1) * LANES], lane)

    def scores(g):
        pr, j = order[g]
        lhs = qblock(pr, j) if j == nb - 1 else jnp.concatenate([qblock(pr, j), qblock(pr, j + 1)], axis=0)
        z[g] = lax.dot_general(lhs, k_ref[0, j * t:(j + 1) * t, pr * LANES:(pr + 1) * LANES], _NT,
                               preferred_element_type=F32)

    def suffix_sums(g):
        s = _softplus(z[g])
        top = jnp.where(tri, s[:2 * t], 0.0)
        s = top if order[g][1] == nb - 1 else jnp.concatenate([top, s[2 * t:]], axis=0)
        sp[g] = s
        z[g] = z[g] - s
        c[g] = jnp.dot(s.astype(BF16), msum, preferred_element_type=F32)

    def weights_pv(g):
        pr, j = order[g]
        ls = slice(pr * LANES, (pr + 1) * LANES)
        zc = z.pop(g) - c.pop(g)
        w = jnp.where(tri, jnp.exp(zc[:2 * t]), 0.0)
        if j < nb - 1:
            w_prev = jnp.exp(zc[2 * t:] - jnp.concatenate([carry_diag[(pr, j + 1)]] * reps, axis=1))
            w = jnp.concatenate([w, w_prev], axis=0)
        pv = jnp.dot(w.astype(BF16), v_ref[0, j * t:(j + 1) * t, ls], preferred_element_type=F32)
        rs = jnp.sum(sp.pop(g), axis=1, keepdims=True)
        if j < nb - 1:
            done = acc_diag.pop((pr, j + 1)) + pv[2 * t:]
            o_ref[0, (j + 1) * t:(j + 2) * t, ls] = _unstack_heads(done, lane).astype(BF16)
            total = carry_diag.pop((pr, j + 1)) + rs[2 * t:]
            if j >= 1:
                flag_ref[pr * nb + j + 1] = jnp.min(total)
        if j == 0:
            o_ref[0, 0:t, ls] = _unstack_heads(pv[:2 * t], lane).astype(BF16)
        else:
            acc_diag[(pr, j)] = pv[:2 * t]
            carry_diag[(pr, j)] = jnp.broadcast_to(rs[:2 * t], (2 * t, LANES))

    n_groups = len(order)
    for step in range(n_groups + 2):
        if step < n_groups:
            scores(step)
        if 1 <= step <= n_groups:
            suffix_sums(step - 1)
        if step >= 2:
            weights_pv(step - 2)

    def recompute_block(pr, jb):
        ls = slice(pr * LANES, (pr + 1) * LANES)
        rows = pl.ds(pl.multiple_of(jb * t, t), t)
        qb = _stack_heads(q_ref[0, rows, ls], lane)
        acc_ref[...] = jnp.zeros_like(acc_ref)
        carry_ref[...] = jnp.zeros_like(carry_ref)

        def tile(kb, diag):
            keys = pl.ds(pl.multiple_of(kb * t, t), t)
            zz = lax.dot_general(qb, k_ref[0, keys, ls], _NT, preferred_element_type=F32)
            s = _softplus(zz)
            if diag:
                s = jnp.where(tri, s, 0.0)
            cc = jnp.dot(s.astype(BF16), msum, preferred_element_type=F32)
            w = jnp.exp(zz - s - cc - jnp.concatenate([carry_ref[...]] * reps, axis=1))
            if diag:
                w = jnp.where(tri, w, 0.0)
            acc_ref[...] += jnp.dot(w.astype(BF16), v_ref[0, keys, ls], preferred_element_type=F32)
            carry_ref[...] += jnp.sum(s, axis=1, keepdims=True)

        tile(jb, True)

        def body(i, _):
            tile(jb - 1 - i, False)
            return 0

        lax.fori_loop(0, jb, body, 0)
        o_ref[0, rows, ls] = _unstack_heads(acc_ref[...], lane).astype(BF16)

    for pr in range(ATT_PAIRS):
        def redo(jb, _, pr=pr):
            pl.when(flag_ref[pr * nb + jb] < SB_EXIT)(lambda: recompute_block(pr, jb))
            return 0

        lax.fori_loop(2, nb, redo, 0)


def _sb_attention(q, k, v):
    b, s, d = q.shape
    t = ATT_K_TILE
    width = ATT_PAIRS * LANES
    ii = np.arange(t)
    msum = jnp.asarray(ii[:, None] > ii[None, :], BF16)
    spec = pl.BlockSpec((1, s, width), lambda bi, g: (bi, 0, g))
    return pl.pallas_call(
        _sb_kernel,
        grid=(b, d // width),
        in_specs=[spec, spec, spec, _const_spec((t, t))],
        out_specs=spec,
        out_shape=jax.ShapeDtypeStruct((b, s, d), BF16),
        scratch_shapes=[pltpu.SMEM((ATT_PAIRS * (s // t),), F32), pltpu.VMEM((2 * t, LANES), F32),
                        pltpu.VMEM((2 * t, LANES), F32)],
        compiler_params=_params(2),
        name="sb_attn",
    )(q, k, v, msum)


def _fox_kernel(q_ref, aq_ref, k_ref, ak_ref, v_ref, o_ref):
    s_len = q_ref.shape[1]
    t = ATT_Q_TILE
    hb = t // 2
    lane = lax.broadcasted_iota(jnp.int32, (hb, LANES), 1)
    group = lane // AUG_STRIDE
    tri = (lax.broadcasted_iota(jnp.int32, (t, hb), 1)
           <= (lax.broadcasted_iota(jnp.int32, (t, hb), 0) & (hb - 1)))
    ones = jnp.ones((t, LANES), BF16)
    tiles = []
    for pr in range(ATT_PAIRS):
        for qi in range(s_len // t):
            tiles += [(pr, qi, "full", kb * t, t) for kb in range(qi)]
            tiles += [(pr, qi, "diag_all", qi * t, hb), (pr, qi, "diag_b", qi * t + hb, hb)]
    qext, kvext, z, m, acc = {}, {}, {}, {}, {}

    def stacked_q(pr, qi):
        ls = slice(pr * LANES, (pr + 1) * LANES)
        pair = ATT_PAIRS * pl.program_id(1) + pr
        parts = []
        for half in range(2):
            rows = slice(qi * t + half * hb, qi * t + (half + 1) * hb)
            aq = aq_ref[0, rows, :]
            zero = jnp.zeros_like(aq)
            aug = jnp.concatenate([jnp.where(group == 2 * pair, aq, zero),
                                   jnp.where(group == 2 * pair + 1, aq, zero)], axis=0)
            parts.append(jnp.concatenate([_stack_heads(q_ref[0, rows, ls], lane), aug], axis=1))
        return jnp.concatenate(parts, axis=0)

    def keys_values(pr, start, size):
        if (pr, start, size) not in kvext:
            ls = slice(pr * LANES, (pr + 1) * LANES)
            keys = slice(start, start + size)
            kext = jnp.concatenate([k_ref[0, keys, ls], ak_ref[0, keys, :]], axis=1)
            vext = jnp.concatenate([v_ref[0, keys, ls], ones[:size]], axis=1)
            kvext[(pr, start, size)] = (kext, vext)
        return kvext[(pr, start, size)]

    def scores(n):
        pr, qi, kind, start, size = tiles[n]
        if (pr, qi) not in qext:
            qext[(pr, qi)] = stacked_q(pr, qi)
        lhs = qext[(pr, qi)]
        kext = keys_values(pr, start, size)[0]
        if kind == "diag_b":
            s = lax.dot_general(lhs[t:], kext, _NT, preferred_element_type=F32)
            s = jnp.where(tri, s, NEG_INF)
        else:
            s = lax.dot_general(lhs, kext, _NT, preferred_element_type=F32)
            if kind == "diag_all":
                s = jnp.concatenate([jnp.where(tri, s[:t], NEG_INF), s[t:]], axis=0)
        z[n] = s

    def softmax_pv(n):
        pr, qi, kind, start, size = tiles[n]
        key = (pr, qi)
        s = z.pop(n)
        vext = keys_values(pr, start, size)[1]
        row_max = jnp.max(s, axis=1, keepdims=True)
        if key not in m:
            m_new = jnp.broadcast_to(row_max, (s.shape[0], LANES))
        else:
            m_old = m[key][t:] if kind == "diag_b" else m[key]
            m_new = jnp.maximum(m_old, row_max)
        p = jnp.exp(s - jnp.concatenate([m_new] * (size // LANES), axis=1))
        pv = jnp.dot(p.astype(BF16), vext, preferred_element_type=F32)
        if key not in m:
            acc[key], m[key] = pv, m_new
        else:
            acc_old = acc[key][t:] if kind == "diag_b" else acc[key]
            alpha = jnp.exp(m_old - m_new)
            acc_new = jnp.concatenate([alpha, alpha], axis=1) * acc_old + pv
            if kind == "diag_b":
                acc[key] = jnp.concatenate([acc[key][:t], acc_new], axis=0)
                m[key] = jnp.concatenate([m[key][:t], m_new], axis=0)
            else:
                acc[key], m[key] = acc_new, m_new
        if kind == "diag_b":
            a = acc.pop(key)
            o = a[:, :LANES] / a[:, LANES:]
            out = jnp.concatenate([_unstack_heads(o[:t], lane), _unstack_heads(o[t:], lane)], axis=0)
            o_ref[0, qi * t:(qi + 1) * t, pr * LANES:(pr + 1) * LANES] = out.astype(BF16)
            m.pop(key)
            qext.pop(key)

    n_tiles = len(tiles)
    for step in range(n_tiles + 1):
        if step < n_tiles:
            scores(step)
        if step >= 1:
            softmax_pv(step - 1)


def _fox_attention(q, k, v, aq, ak):
    b, s, d = q.shape
    width = ATT_PAIRS * LANES
    spec = pl.BlockSpec((1, s, width), lambda bi, g: (bi, 0, g))
    aspec = pl.BlockSpec((1, s, LANES), lambda bi, g: (bi, 0, 0))
    return pl.pallas_call(
        _fox_kernel,
        grid=(b, d // width),
        in_specs=[spec, aspec, spec, aspec, spec],
        out_specs=spec,
        out_shape=jax.ShapeDtypeStruct((b, s, d), BF16),
        compiler_params=_params(2),
        name="fox_attn",
    )(q, aq, k, ak, v)


def _memory_attention(q_ref, mk_ref, mv_ref):
    z, out = {}, []

    def scores(h):
        sl = slice(h * MEM_HD, (h + 1) * MEM_HD)
        z[h] = lax.dot_general(q_ref[:, sl], mk_ref[:, sl], _NT,
                               preferred_element_type=F32) * MEM_HD ** -0.5

    def softmax_pv(h):
        sl = slice(h * MEM_HD, (h + 1) * MEM_HD)
        zh = z.pop(h)
        e = jnp.exp(zh - jnp.max(zh, axis=1, keepdims=True))
        p = e / jnp.sum(e, axis=1, keepdims=True)
        out.append(jnp.dot(p.astype(BF16), mv_ref[:, sl], preferred_element_type=F32).astype(BF16))

    for step in range(MEM_HEADS + 1):
        if step < MEM_HEADS:
            scores(step)
        if step >= 1:
            softmax_pv(step - 1)
    return jnp.concatenate(out, axis=1)


def _merge_kernel(steps_per_batch, x_ref, g_ref, wgt_ref, osb_ref, ofx_ref, mq_ref,
                  mem_ref, gmem_ref, wkv_ref, gk_ref,
                  wsb_ref, wfx_ref, wmem_ref, wout_ref, o_ref, mk_ref, mv_ref):
    @pl.when(pl.program_id(0) % steps_per_batch == 0)
    def _():
        mh = _rms(mem_ref[0], gmem_ref[...]).astype(BF16)
        kv = _mm(mh, wkv_ref)
        mk_ref[...] = _headnorm(kv[:, :D_MEM], gk_ref[...], MEM_HD).astype(BF16)
        mv_ref[...] = kv[:, D_MEM:].astype(BF16)

    x = x_ref[...]
    h = _rms(x, g_ref[...]).astype(BF16)
    branches = ((osb_ref[...], wsb_ref), (ofx_ref[...], wfx_ref),
                (_memory_attention(mq_ref, mk_ref, mv_ref), wmem_ref))
    merged = None
    for i, (o_br, w_br) in enumerate(branches):
        gate = jax.nn.sigmoid(_mm_t(h, wgt_ref, slice(i * D_MODEL, (i + 1) * D_MODEL)))
        term = gate * _mm(o_br, w_br)
        merged = term if merged is None else merged + term
    o_ref[...] = x + _mm(merged.astype(BF16), wout_ref)


def _merge(x2, g, w_in_t, osb, ofx, mq, mem, gmem, wkv, gk, wsb, wfx, wmem, wout):
    t = x2.shape[0]
    b, m, _ = mem.shape
    steps_per_batch = t // b // ROW_TILE
    n_gate = N_BRANCH * D_MODEL
    gate_rows = pl.BlockSpec((pl.Element(n_gate), pl.Element(D_MODEL)),
                             lambda i: (w_in_t.shape[0] - n_gate, 0),
                             pipeline_mode=pl.Buffered(1))
    row = lambda n: pl.BlockSpec((ROW_TILE, n), lambda i: (i, 0))
    return pl.pallas_call(
        functools.partial(_merge_kernel, steps_per_batch),
        grid=(t // ROW_TILE,),
        in_specs=[row(D_MODEL), _const_spec((1, D_MODEL)), gate_rows,
                  row(D_SB), row(D_FOX), row(D_MEM),
                  pl.BlockSpec((1, m, D_MODEL), lambda i: (i // steps_per_batch, 0, 0)),
                  _const_spec((1, D_MODEL)), _const_spec(wkv.shape), _const_spec((1, D_MEM)),
                  _const_spec(wsb.shape), _const_spec(wfx.shape), _const_spec(wmem.shape),
                  _const_spec(wout.shape)],
        out_specs=row(D_MODEL),
        out_shape=jax.ShapeDtypeStruct((t, D_MODEL), F32),
        scratch_shapes=[pltpu.VMEM((m, D_MEM), BF16), pltpu.VMEM((m, D_MEM), BF16)],
        compiler_params=_params(1),
        name="merge",
    )(x2, g, w_in_t, osb, ofx, mq, mem, gmem, wkv, gk, wsb, wfx, wmem, wout)


def _mlp_kernel(x_ref, g_ref, wup_ref, wdn_ref, o_ref):
    x = x_ref[...]
    h = _rms(x, g_ref[...]).astype(BF16)
    acc = x
    for c in range(D_FF // FF_CHUNK):
        sl = slice(c * FF_CHUNK, (c + 1) * FF_CHUNK)
        u = jnp.maximum(_mm(h, wup_ref, cols=sl), 0.0)
        acc = acc + _mm((u * u).astype(BF16), wdn_ref, rows=sl)
    o_ref[...] = acc


def _mlp(x2, g, wup, wdn):
    t = x2.shape[0]
    row = pl.BlockSpec((ROW_TILE, D_MODEL), lambda i: (i, 0))
    return pl.pallas_call(
        _mlp_kernel,
        grid=(t // ROW_TILE,),
        in_specs=[row, _const_spec((1, D_MODEL)), _const_spec(wup.shape), _const_spec(wdn.shape)],
        out_specs=row,
        out_shape=jax.ShapeDtypeStruct((t, D_MODEL), F32),
        compiler_params=_params(1),
        name="mlp",
    )(x2, g, wup, wdn)


def _layer(x, mem, g_mix, g_memn, w_in, b_forget, g_fq, g_fk, g_mq, g_mk, w_mem_kv,
           w_sb, w_fox, w_mem, w_out, g_mlp, w_up, w_dn):
    b, s, _ = x.shape
    t = b * s
    x2 = x.reshape(t, D_MODEL)
    n_qkv = 3 * D_SB + 3 * D_FOX
    w_in_t = w_in.T
    bf =jnp.pad(b_forget, (0, LANES - FOX_HEADS)).reshape(1, LANES)
    row = lambda a: a.reshape(1, -1)

    sbq, sbk, sbv, fxq, fxk, fxv, mq, lf = _proj(
        x2, row(g_mix), w_in_t, bf, row(jnp.tile(g_fq, FOX_HEADS)),
        row(jnp.tile(g_fk, FOX_HEADS)), row(jnp.tile(g_mq, MEM_HEADS)))
    to3 = lambda a: a.reshape(b, s, a.shape[-1])

    aq, ak = _forget(to3(lf))
    o_sb = _sb_attention(to3(sbq), to3(sbk), to3(sbv))
    o_fox = _fox_attention(to3(fxq), to3(fxk), to3(fxv), aq, ak)
    x1 = _merge(x2, row(g_mix), w_in_t, o_sb.reshape(t, D_SB), o_fox.reshape(t, D_FOX), mq,
                mem, row(g_memn), w_mem_kv, row(jnp.tile(g_mk, MEM_HEADS)), w_sb, w_fox, w_mem, w_out)
    out = _mlp(x1, row(g_mlp), w_up, w_dn)
    return out.reshape(b, s, D_MODEL)


def kernel(x, mem, g_mix_norm, g_mem_norm, w_in, b_forget, g_fox_q, g_fox_k, g_mem_q, g_mem_k,
           w_mem_kv, w_branch_sb, w_branch_fox, w_branch_mem, w_out, g_mlp_norm, w_ff_up, w_ff_down):
    for l in range(w_in.shape[0]):
        x = _layer(x, mem, g_mix_norm[l], g_mem_norm[l], w_in[l], b_forget[l], g_fox_q[l], g_fox_k[l],
                   g_mem_q[l], g_mem_k[l], w_mem_kv[l], w_branch_sb[l], w_branch_fox[l],
                   w_branch_mem[l], w_out[l], g_mlp_norm[l], w_ff_up[l], w_ff_down[l])
    return x
```

```python
import functools

import numpy as np
import jax
import jax.numpy as jnp
from jax import lax
from jax.experimental import pallas as pl
from jax.experimental.pallas import tpu as pltpu

D_MODEL = 1024
HD = 64
SB_HEADS = 8
FOX_HEADS = 8
MEM_HEADS = 4
MEM_HD = 128
D_SB = SB_HEADS * HD
D_FOX = FOX_HEADS * HD
D_MEM = MEM_HEADS * MEM_HD
N_BRANCH = 3
D_FF = 4 * D_MODEL
EPS = 1e-6
NEG_INF = -1e30

LANES = 128
AUG_STRIDE = 8
ROW_TILE = 512
PROJ_ROW_TILE = 1024
ATT_Q_TILE = 512
ATT_K_TILE = 256
ATT_PAIRS = 2
SB_EXIT = 110.0
FF_CHUNK = 512
VMEM_LIMIT = 56 * 1024 * 1024

F32 = jnp.float32
BF16 = jnp.bfloat16
_NT = (((1,), (1,)), ((), ()))
_LOG2E = 1.4426950408889634


def _const_spec(shape):
    nd = len(shape)
    return pl.BlockSpec(shape, lambda *_: (0,) * nd, pipeline_mode=pl.Buffered(1))


def _params(n_axes):
    return pltpu.CompilerParams(
        dimension_semantics=("arbitrary",) * n_axes, vmem_limit_bytes=VMEM_LIMIT)


def _rms(x, g):
    ms = jnp.sum(x * x, axis=-1, keepdims=True) * (1.0 / x.shape[-1])
    return (x * lax.rsqrt(ms + EPS)) * g


def _headnorm(p, g, hd):
    rows, n = p.shape
    lane = lax.broadcasted_iota(jnp.int32, (rows, LANES), 1)
    outs = []
    for j in range(n // LANES):
        c = p[:, j * LANES:(j + 1) * LANES]
        c2 = c * c
        if hd == LANES:
            ms = jnp.sum(c2, axis=1, keepdims=True) * (1.0 / LANES)
        else:
            lo = jnp.sum(jnp.where(lane < hd, c2, 0.0), axis=1, keepdims=True)
            hi = jnp.sum(jnp.where(lane >= hd, c2, 0.0), axis=1, keepdims=True)
            ms = jnp.where(lane < hd, lo, hi) * (1.0 / hd)
        outs.append(c * lax.rsqrt(ms + EPS))
    return jnp.concatenate(outs, axis=1) * g


def _log_sigmoid(x):
    return jnp.minimum(x, 0.0) - jnp.log(1.0 + jnp.exp(-jnp.abs(x)))


def _mm(a, w_ref, rows=slice(None), cols=slice(None)):
    return jnp.dot(a, w_ref[rows, cols].astype(BF16), preferred_element_type=F32)


def _mm_t(a, wt_ref, rows):
    return lax.dot_general(a, wt_ref[rows, :].astype(BF16), _NT, preferred_element_type=F32)


def _decay_columns(lf, carry_ref, p_ref, oq_ref, ok_ref):
    rows = lf.shape[0]
    row = lax.broadcasted_iota(jnp.int32, lf.shape, 0)
    lane = lax.broadcasted_iota(jnp.int32, lf.shape, 1)
    f = lf
    sh = 1
    while sh < rows:
        f = f + jnp.where(row >= sh, pltpu.roll(f, sh, 0), 0.0)
        sh *= 2
    f = f + carry_ref[...]
    carry_ref[...] = f[rows - 1:rows, :]
    f_hi = f.astype(BF16).astype(F32)
    r1 = f - f_hi
    f_mid = r1.astype(BF16).astype(F32)
    f_lo = (r1 - f_mid).astype(BF16).astype(F32)
    keep = lane < FOX_HEADS
    zero = jnp.zeros_like(f)
    pieces = (jnp.where(keep, f_hi, zero) + pltpu.roll(jnp.where(keep, f_mid, zero), FOX_HEADS, 1)
              + pltpu.roll(jnp.where(keep, f_lo, zero), 2 * FOX_HEADS, 1))
    placed = jnp.dot(pieces.astype(BF16), p_ref[...], preferred_element_type=F32)
    return ((placed[:, :LANES] + oq_ref[...]).astype(BF16), (placed[:, LANES:] + ok_ref[...]).astype(BF16))


def _decay_constants():
    p = np.zeros((LANES, 2 * LANES), np.float32)
    oq = np.zeros((1, LANES), np.float32)
    ok = np.zeros((1, LANES), np.float32)
    for h in range(FOX_HEADS):
        for j in range(3):
            p[j * FOX_HEADS + h, h * AUG_STRIDE + j] = 1.0
            p[j * FOX_HEADS + h, LANES + h * AUG_STRIDE + 3 + j] = -1.0
            oq[0, h * AUG_STRIDE + 3 + j] = 1.0
            ok[0, h * AUG_STRIDE + j] = 1.0
    return jnp.asarray(p, BF16), jnp.asarray(oq), jnp.asarray(ok)


def _proj_kernel(steps_per_batch, x_ref, g_ref, wt_ref, bf_ref, gq_ref, gk_ref, gm_ref, p_ref, oq_ref, ok_ref,
                 sbq_ref, sbk_ref, sbv_ref, fxq_ref, fxk_ref, fxv_ref, mq_ref, aq_ref, ak_ref,
                 carry_ref, wbf_ref):
    @pl.when(pl.program_id(0) == 0)
    def _():
        wbf_ref[...] = wt_ref[...].astype(BF16)

    @pl.when(pl.program_id(0) % steps_per_batch == 0)
    def _():
        carry_ref[...] = jnp.zeros_like(carry_ref)

    h = _rms(x_ref[...], g_ref[...]).astype(BF16)

    def mm(c0, n):
        return lax.dot_general(h, wbf_ref[c0:c0 + n, :], _NT, preferred_element_type=F32)

    c0 = 3 * D_SB
    c_f = c0 + 3 * D_FOX
    log_f = _log_sigmoid(mm(c_f, LANES) + bf_ref[...])
    mq_ref[...] = _headnorm(mm(c_f + FOX_HEADS, D_MEM), gm_ref[...], MEM_HD).astype(BF16)
    fxq_ref[...] = (_headnorm(mm(c0, D_FOX), gq_ref[...], HD) * HD ** -0.5).astype(BF16)
    fxk_ref[...] = _headnorm(mm(c0 + D_FOX, D_FOX), gk_ref[...], HD).astype(BF16)
    aq_ref[...], ak_ref[...] = _decay_columns(log_f, carry_ref, p_ref, oq_ref, ok_ref)
    fxv_ref[...] = mm(c0 + 2 * D_FOX, D_FOX).astype(BF16)
    sbq_ref[...] = (mm(0, D_SB) * HD ** -0.5).astype(BF16)
    sbk_ref[...] = mm(D_SB, D_SB).astype(BF16)
    sbv_ref[...] = mm(2 * D_SB, D_SB).astype(BF16)


def _proj(x2, batch, g, w_in_t, bf, gq, gk, gm):
    t = x2.shape[0]
    n_rows = 3 * D_SB + 3 * D_FOX + FOX_HEADS + D_MEM
    p, oq, ok = _decay_constants()
    row = lambda n: pl.BlockSpec((PROJ_ROW_TILE, n), lambda i: (i, 0))
    out_shapes = [jax.ShapeDtypeStruct((t, D_SB), BF16)] * 7 + [jax.ShapeDtypeStruct((t, LANES), BF16)] * 2
    return pl.pallas_call(
        functools.partial(_proj_kernel, t // batch // PROJ_ROW_TILE),
        grid=(t // PROJ_ROW_TILE,),
        in_specs=[row(D_MODEL), _const_spec((1, D_MODEL)), _const_spec((n_rows, D_MODEL)),
                  _const_spec((1, LANES)), _const_spec((1, D_FOX)), _const_spec((1, D_FOX)),
                  _const_spec((1, D_MEM)), _const_spec(p.shape), _const_spec(oq.shape), _const_spec(ok.shape)],
        out_specs=[row(D_SB)] * 7 + [row(LANES)] * 2,
        out_shape=out_shapes,
        scratch_shapes=[pltpu.VMEM((1, LANES), F32), pltpu.VMEM((n_rows, D_MODEL), BF16)],
        compiler_params=_params(1),
        name="proj",
    )(x2, g, w_in_t, bf, gq, gk, gm, p, oq, ok)


def _stack_heads(x, lane):
    zero = jnp.zeros_like(x)
    return jnp.concatenate([jnp.where(lane < HD, x, zero), jnp.where(lane >= HD, x, zero)], axis=0)


def _unstack_heads(x, lane):
    rows = x.shape[0] // 2
    return jnp.where(lane < HD, x[:rows], x[rows:])


def _softplus(z):
    return jnp.maximum(z, 0.0) + jnp.log(1.0 + jnp.exp2(jnp.abs(z) * (-_LOG2E)))


def _sb_kernel(q_ref, k_ref, v_ref, m_ref, o_ref, flag_ref, acc_ref, carry_ref):
    s_len = q_ref.shape[1]
    t = ATT_K_TILE
    nb = s_len // t
    lane = lax.broadcasted_iota(jnp.int32, (t, LANES), 1)
    tri = (lax.broadcasted_iota(jnp.int32, (2 * t, t), 1)
           < (lax.broadcasted_iota(jnp.int32, (2 * t, t), 0) & (t - 1)))
    msum = m_ref[...]
    reps = t // LANES
    order = [(pr, j) for pr in range(ATT_PAIRS) for j in reversed(range(nb))]
    z, sp, c, acc_diag, carry_diag = {}, {}, {}, {}, {}

    def qblock(pr, j):
        return _stack_heads(q_ref[0, j * t:(j + 1) * t, pr * LANES:(pr + 1) * LANES], lane)

    def scores(g):
        pr, j = order[g]
        lhs = qblock(pr, j) if j == nb - 1 else jnp.concatenate([qblock(pr, j), qblock(pr, j + 1)], axis=0)
        z[g] = lax.dot_general(lhs, k_ref[0, j * t:(j + 1) * t, pr * LANES:(pr + 1) * LANES], _NT,
                               preferred_element_type=F32)

    def suffix_sums(g):
        s = _softplus(z[g])
        top = jnp.where(tri, s[:2 * t], 0.0)
        s = top if order[g][1] == nb - 1 else jnp.concatenate([top, s[2 * t:]], axis=0)
        sp[g] = s
        z[g] = z[g] - s
        c[g] = jnp.dot(s.astype(BF16), msum, preferred_element_type=F32)

    def weights_pv(g):
        pr, j = order[g]
        ls = slice(pr * LANES, (pr + 1) * LANES)
        zc = z.pop(g) - c.pop(g)
        w = jnp.where(tri, jnp.exp(zc[:2 * t]), 0.0)
        if j < nb - 1:
            w_prev = jnp.exp(zc[2 * t:] - jnp.concatenate([carry_diag[(pr, j + 1)]] * reps, axis=1))
            w = jnp.concatenate([w, w_prev], axis=0)
        pv = jnp.dot(w.astype(BF16), v_ref[0, j * t:(j + 1) * t, ls], preferred_element_type=F32)
        rs = jnp.sum(sp.pop(g), axis=1, keepdims=True)
        if j < nb - 1:
            done = acc_diag.pop((pr, j + 1)) + pv[2 * t:]
            o_ref[0, (j + 1) * t:(j + 2) * t, ls] = _unstack_heads(done, lane).astype(BF16)
            total = carry_diag.pop((pr, j + 1)) + rs[2 * t:]
            if j >= 1:
                flag_ref[pr * nb + j + 1] = jnp.min(total)
        if j == 0:
            o_ref[0, 0:t, ls] = _unstack_heads(pv[:2 * t], lane).astype(BF16)
        else:
            acc_diag[(pr, j)] = pv[:2 * t]
            carry_diag[(pr, j)] = jnp.broadcast_to(rs[:2 * t], (2 * t, LANES))

    n_groups = len(order)
    for step in range(n_groups + 2):
        if step < n_groups:
            scores(step)
        if 1 <= step <= n_groups:
            suffix_sums(step - 1)
        if step >= 2:
            weights_pv(step - 2)

    def recompute_block(pr, jb):
        ls = slice(pr * LANES, (pr + 1) * LANES)
        rows = pl.ds(pl.multiple_of(jb * t, t), t)
        qb = _stack_heads(q_ref[0, rows, ls], lane)
        acc_ref[...] = jnp.zeros_like(acc_ref)
        carry_ref[...] = jnp.zeros_like(carry_ref)

        def tile(kb, diag):
            keys = pl.ds(pl.multiple_of(kb * t, t), t)
            zz = lax.dot_general(qb, k_ref[0, keys, ls], _NT, preferred_element_type=F32)
            s = _softplus(zz)
            if diag:
                s = jnp.where(tri, s, 0.0)
            cc = jnp.dot(s.astype(BF16), msum, preferred_element_type=F32)
            w = jnp.exp(zz - s - cc - jnp.concatenate([carry_ref[...]] * reps, axis=1))
            if diag:
                w = jnp.where(tri, w, 0.0)
            acc_ref[...] += jnp.dot(w.astype(BF16), v_ref[0, keys, ls], preferred_element_type=F32)
            carry_ref[...] += jnp.sum(s, axis=1, keepdims=True)

        tile(jb, True)

        def body(i, _):
            tile(jb - 1 - i, False)
            return 0

        lax.fori_loop(0, jb, body, 0)
        o_ref[0, rows, ls] = _unstack_heads(acc_ref[...], lane).astype(BF16)

    for pr in range(ATT_PAIRS):
        def redo(jb, _, pr=pr):
            pl.when(flag_ref[pr * nb + jb] < SB_EXIT)(lambda: recompute_block(pr, jb))
            return 0

        lax.fori_loop(2, nb, redo, 0)


def _sb_attention(q, k, v):
    b, s, d = q.shape
    t = ATT_K_TILE
    width = ATT_PAIRS * LANES
    ii = np.arange(t)
    msum = jnp.asarray(ii[:, None] > ii[None, :], BF16)
    spec = pl.BlockSpec((1, s, width), lambda bi, g: (bi, 0, g))
    return pl.pallas_call(
        _sb_kernel,
        grid=(b, d // width),
        in_specs=[spec, spec, spec, _const_spec((t, t))],
        out_specs=spec,
        out_shape=jax.ShapeDtypeStruct((b, s, d), BF16),
        scratch_shapes=[pltpu.SMEM((ATT_PAIRS * (s // t),), F32), pltpu.VMEM((2 * t, LANES), F32),
                        pltpu.VMEM((2 * t, LANES), F32)],
        compiler_params=_params(2),
        name="sb_attn",
    )(q, k, v, msum)


def _fox_kernel(q_ref, aq_ref, k_ref, ak_ref, v_ref, o_ref):
    s_len = q_ref.shape[1]
    t = ATT_Q_TILE
    hb = t // 2
    lane = lax.broadcasted_iota(jnp.int32, (hb, LANES), 1)
    group = lane // AUG_STRIDE
    tri = (lax.broadcasted_iota(jnp.int32, (t, hb), 1)
           <= (lax.broadcasted_iota(jnp.int32, (t, hb), 0) & (hb - 1)))
    ones = jnp.ones((t, LANES), BF16)
    tiles = []
    for pr in range(ATT_PAIRS):
        for qi in range(s_len // t):
            tiles += [(pr, qi, "full", kb * t, t) for kb in range(qi)]
            tiles += [(pr, qi, "diag_all", qi * t, hb), (pr, qi, "diag_b", qi * t + hb, hb)]
    qext, kvext, z, m, acc = {}, {}, {}, {}, {}

    def stacked_q(pr, qi):
        ls = slice(pr * LANES, (pr + 1) * LANES)
        pair = ATT_PAIRS * pl.program_id(1) + pr
        parts = []
        for half in range(2):
            rows = slice(qi * t + half * hb, qi * t + (half + 1) * hb)
            aq = aq_ref[0, rows, :]
            zero = jnp.zeros_like(aq)
            aug = jnp.concatenate([jnp.where(group == 2 * pair, aq, zero),
                                   jnp.where(group == 2 * pair + 1, aq, zero)], axis=0)
            parts.append(jnp.concatenate([_stack_heads(q_ref[0, rows, ls], lane), aug], axis=1))
        return jnp.concatenate(parts, axis=0)

    def keys_values(pr, start, size):
        if (pr, start, size) not in kvext:
            ls = slice(pr * LANES, (pr + 1) * LANES)
            keys = slice(start, start + size)
            kext = jnp.concatenate([k_ref[0, keys, ls], ak_ref[0, keys, :]], axis=1)
            vext = jnp.concatenate([v_ref[0, keys, ls], ones[:size]], axis=1)
            kvext[(pr, start, size)] = (kext, vext)
        return kvext[(pr, start, size)]

    def scores(n):
        pr, qi, kind, start, size = tiles[n]
        if (pr, qi) not in qext:
            qext[(pr, qi)] = stacked_q(pr, qi)
        lhs = qext[(pr, qi)]
        kext = keys_values(pr, start, size)[0]
        if kind == "diag_b":
            s = lax.dot_general(lhs[t:], kext, _NT, preferred_element_type=F32)
            s = jnp.where(tri, s, NEG_INF)
        else:
            s = lax.dot_general(lhs, kext, _NT, preferred_element_type=F32)
            if kind == "diag_all":
                s = jnp.concatenate([jnp.where(tri, s[:t], NEG_INF), s[t:]], axis=0)
        z[n] = s

    def softmax_pv(n):
        pr, qi, kind, start, size = tiles[n]
        key = (pr, qi)
        s = z.pop(n)
        vext = keys_values(pr, start, size)[1]
        row_max = jnp.max(s, axis=1, keepdims=True)
        if key not in m:
            m_new = jnp.broadcast_to(row_max, (s.shape[0], LANES))
        else:
            m_old = m[key][t:] if kind == "diag_b" else m[key]
            m_new = jnp.maximum(m_old, row_max)
        p = jnp.exp(s - jnp.concatenate([m_new] * (size // LANES), axis=1))
        pv = jnp.dot(p.astype(BF16), vext, preferred_element_type=F32)
        if key not in m:
            acc[key], m[key] = pv, m_new
        else:
            acc_old = acc[key][t:] if kind == "diag_b" else acc[key]
            alpha = jnp.exp(m_old - m_new)
            acc_new = jnp.concatenate([alpha, alpha], axis=1) * acc_old + pv
            if kind == "diag_b":
                acc[key] = jnp.concatenate([acc[key][:t], acc_new], axis=0)
                m[key] = jnp.concatenate([m[key][:t], m_new], axis=0)
            else:
                acc[key], m[key] = acc_new, m_new
        if kind == "diag_b":
            a = acc.pop(key)
            o = a[:, :LANES] / a[:, LANES:]
            out = jnp.concatenate([_unstack_heads(o[:t], lane), _unstack_heads(o[t:], lane)], axis=0)
            o_ref[0, qi * t:(qi + 1) * t, pr * LANES:(pr + 1) * LANES] = out.astype(BF16)
            m.pop(key)
            qext.pop(key)

    n_tiles = len(tiles)
    for step in range(n_tiles + 1):
        if step < n_tiles:
            scores(step)
        if step >= 1:
            softmax_pv(step - 1)


def _fox_attention(q, k, v, aq, ak):
    b, s, d = q.shape
    width = ATT_PAIRS * LANES
    spec = pl.BlockSpec((1, s, width), lambda bi, g: (bi, 0, g))
    aspec = pl.BlockSpec((1, s, LANES), lambda bi, g: (bi, 0, 0))
    return pl.pallas_call(
        _fox_kernel,
        grid=(b, d // width),
        in_specs=[spec, aspec, spec, aspec, spec],
        out_specs=spec,
        out_shape=jax.ShapeDtypeStruct((b, s, d), BF16),
        compiler_params=_params(2),
        name="fox_attn",
    )(q, aq, k, ak, v)


def _memory_attention(q_ref, mk_ref, mv_ref):
    z, out = {}, []

    def scores(h):
        sl = slice(h * MEM_HD, (h + 1) * MEM_HD)
        z[h] = lax.dot_general(q_ref[:, sl], mk_ref[:, sl], _NT,
                               preferred_element_type=F32) * MEM_HD ** -0.5

    def softmax_pv(h):
        sl = slice(h * MEM_HD, (h + 1) * MEM_HD)
        zh = z.pop(h)
        e = jnp.exp(zh - jnp.max(zh, axis=1, keepdims=True))
        p = e / jnp.sum(e, axis=1, keepdims=True)
        out.append(jnp.dot(p.astype(BF16), mv_ref[:, sl], preferred_element_type=F32).astype(BF16))

    for step in range(MEM_HEADS + 1):
        if step < MEM_HEADS:
            scores(step)
        if step >= 1:
            softmax_pv(step - 1)
    return jnp.concatenate(out, axis=1)


def _merge_kernel(steps_per_batch, x_ref, g_ref, wgt_ref, osb_ref, ofx_ref, mq_ref,
                  mem_ref, gmem_ref, wkv_ref, gk_ref,
                  wsb_ref, wfx_ref, wmem_ref, wout_ref, o_ref, mk_ref, mv_ref):
    @pl.when(pl.program_id(0) % steps_per_batch == 0)
    def _():
        mh = _rms(mem_ref[0], gmem_ref[...]).astype(BF16)
        kv = _mm(mh, wkv_ref)
        mk_ref[...] = _headnorm(kv[:, :D_MEM], gk_ref[...], MEM_HD).astype(BF16)
        mv_ref[...] = kv[:, D_MEM:].astype(BF16)

    x = x_ref[...]
    h = _rms(x, g_ref[...]).astype(BF16)
    branches = ((osb_ref[...], wsb_ref), (ofx_ref[...], wfx_ref),
                (_memory_attention(mq_ref, mk_ref, mv_ref), wmem_ref))
    merged = None
    for i, (o_br, w_br) in enumerate(branches):
        gate = jax.nn.sigmoid(_mm_t(h, wgt_ref, slice(i * D_MODEL, (i + 1) * D_MODEL)))
        term = gate * _mm(o_br, w_br)
        merged = term if merged is None else merged + term
    o_ref[...] = x + _mm(merged.astype(BF16), wout_ref)


def _merge(x2, g, w_in_t, osb, ofx, mq, mem, gmem, wkv, gk, wsb, wfx, wmem, wout):
    t = x2.shape[0]
    b, m, _ = mem.shape
    steps_per_batch = t // b // ROW_TILE
    n_gate = N_BRANCH * D_MODEL
    gate_rows = pl.BlockSpec((pl.Element(n_gate), pl.Element(D_MODEL)),
                             lambda i: (w_in_t.shape[0] - n_gate, 0),
                             pipeline_mode=pl.Buffered(1))
    row = lambda n: pl.BlockSpec((ROW_TILE, n), lambda i: (i, 0))
    return pl.pallas_call(
        functools.partial(_merge_kernel, steps_per_batch),
        grid=(t // ROW_TILE,),
        in_specs=[row(D_MODEL), _const_spec((1, D_MODEL)), gate_rows,
                  row(D_SB), row(D_FOX), row(D_MEM),
                  pl.BlockSpec((1, m, D_MODEL), lambda i: (i // steps_per_batch, 0, 0)),
                  _const_spec((1, D_MODEL)), _const_spec(wkv.shape), _const_spec((1, D_MEM)),
                  _const_spec(wsb.shape), _const_spec(wfx.shape), _const_spec(wmem.shape),
                  _const_spec(wout.shape)],
        out_specs=row(D_MODEL),
        out_shape=jax.ShapeDtypeStruct((t, D_MODEL), F32),
        scratch_shapes=[pltpu.VMEM((m, D_MEM), BF16), pltpu.VMEM((m, D_MEM), BF16)],
        compiler_params=_params(1),
        name="merge",
    )(x2, g, w_in_t, osb, ofx, mq, mem, gmem, wkv, gk, wsb, wfx, wmem, wout)


def _mlp_kernel(x_ref, g_ref, wup_ref, wdn_ref, o_ref):
    x = x_ref[...]
    h = _rms(x, g_ref[...]).astype(BF16)
    acc = x
    for c in range(D_FF // FF_CHUNK):
        sl = slice(c * FF_CHUNK, (c + 1) * FF_CHUNK)
        u = jnp.maximum(_mm(h, wup_ref, cols=sl), 0.0)
        acc = acc + _mm((u * u).astype(BF16), wdn_ref, rows=sl)
    o_ref[...] = acc


def _mlp(x2, g, wup, wdn):
    t = x2.shape[0]
    row = pl.BlockSpec((ROW_TILE, D_MODEL), lambda i: (i, 0))
    return pl.pallas_call(
        _mlp_kernel,
        grid=(t // ROW_TILE,),
        in_specs=[row, _const_spec((1, D_MODEL)), _const_spec(wup.shape), _const_spec(wdn.shape)],
        out_specs=row,
        out_shape=jax.ShapeDtypeStruct((t, D_MODEL), F32),
        compiler_params=_params(1),
        name="mlp",
    )(x2, g, wup, wdn)


def _layer(x, mem, g_mix, g_memn, w_in, b_forget, g_fq, g_fk, g_mq, g_mk, w_mem_kv,
           w_sb, w_fox, w_mem, w_out, g_mlp, w_up, w_dn):
    b, s, _ = x.shape
    t = b * s
    x2 = x.reshape(t, D_MODEL)
    n_qkv = 3 * D_SB + 3 * D_FOX
    w_in_t = w_in.T
    bf = jnp.pad(b_forget, (0, LANES - FOX_HEADS)).reshape(1, LANES)
    row = lambda a: a.reshape(1, -1)

    sbq, sbk, sbv, fxq, fxk, fxv, mq, aq, ak = _proj(
        x2, b, row(g_mix), w_in_t, bf, row(jnp.tile(g_fq, FOX_HEADS)),
        row(jnp.tile(g_fk, FOX_HEADS)), row(jnp.tile(g_mq, MEM_HEADS)))
    to3 = lambda a: a.reshape(b, s, a.shape[-1])

    o_sb = _sb_attention(to3(sbq), to3(sbk), to3(sbv))
    o_fox = _fox_attention(to3(fxq), to3(fxk), to3(fxv), to3(aq), to3(ak))
    x1 = _merge(x2, row(g_mix), w_in_t, o_sb.reshape(t, D_SB), o_fox.reshape(t, D_FOX), mq,
                mem, row(g_memn), w_mem_kv, row(jnp.tile(g_mk, MEM_HEADS)), w_sb, w_fox, w_mem, w_out)
    out = _mlp(x1, row(g_mlp), w_up, w_dn)
    return out.reshape(b, s, D_MODEL)


def kernel(x, mem, g_mix_norm, g_mem_norm, w_in, b_forget, g_fox_q, g_fox_k, g_mem_q, g_mem_k,
           w_mem_kv, w_branch_sb, w_branch_fox, w_branch_mem, w_out, g_mlp_norm, w_ff_up, w_ff_down):
    for l in range(w_in.shape[0]):
        x = _layer(x, mem, g_mix_norm[l], g_mem_norm[l], w_in[l], b_forget[l], g_fox_q[l], g_fox_k[l],
                   g_mem_q[l], g_mem_k[l], w_mem_kv[l], w_branch_sb[l], w_branch_fox[l],
                   w_branch_mem[l], w_out[l], g_mlp_norm[l], w_ff_up[l], w_ff_down[l])
    return x
```

```python
import functools

import numpy as np
import jax
import jax.numpy as jnp
from jax import lax
from jax.experimental import pallas as pl
from jax.experimental.pallas import tpu as pltpu

D_MODEL = 1024
HD = 64
SB_HEADS = 8
FOX_HEADS = 8
MEM_HEADS = 4
MEM_HD = 128
D_SB = SB_HEADS * HD
D_FOX = FOX_HEADS * HD
D_MEM = MEM_HEADS * MEM_HD
N_BRANCH = 3
D_FF = 4 * D_MODEL
EPS = 1e-6
NEG_INF = -1e30

LANES = 128
AUG_STRIDE = 8
ROW_TILE = 512
PROJ_ROW_TILE = 1024
MLP_ROW_TILE = 512
ATT_Q_TILE = 512
ATT_K_TILE = 256
ATT_PAIRS = 2
SB_EXIT = 110.0
FF_CHUNK = 512
VMEM_LIMIT = 56 * 1024 * 1024

F32 = jnp.float32
BF16 = jnp.bfloat16
_NT = (((1,), (1,)), ((), ()))
_LOG2E = 1.4426950408889634


def _const_spec(shape):
    nd = len(shape)
    return pl.BlockSpec(shape, lambda *_: (0,) * nd, pipeline_mode=pl.Buffered(1))


def _params(n_axes):
    return pltpu.CompilerParams(
        dimension_semantics=("arbitrary",) * n_axes, vmem_limit_bytes=VMEM_LIMIT)


def _rms(x, g):
    ms = jnp.sum(x * x, axis=-1, keepdims=True) * (1.0 / x.shape[-1])
    return (x * lax.rsqrt(ms + EPS)) * g


def _headnorm(p, g, hd):
    rows, n = p.shape
    lane = lax.broadcasted_iota(jnp.int32, (rows, LANES), 1)
    outs = []
    for j in range(n // LANES):
        c = p[:, j * LANES:(j + 1) * LANES]
        c2 = c * c
        if hd == LANES:
            ms = jnp.sum(c2, axis=1, keepdims=True) * (1.0 / LANES)
        else:
            lo = jnp.sum(jnp.where(lane < hd, c2, 0.0), axis=1, keepdims=True)
            hi = jnp.sum(jnp.where(lane >= hd, c2, 0.0), axis=1, keepdims=True)
            ms = jnp.where(lane < hd, lo, hi) * (1.0 / hd)
        outs.append(c * lax.rsqrt(ms + EPS))
    return jnp.concatenate(outs, axis=1) * g


def _log_sigmoid(x):
    return jnp.minimum(x, 0.0) - jnp.log(1.0 + jnp.exp(-jnp.abs(x)))


def _mm(a, w_ref, rows=slice(None), cols=slice(None)):
    return jnp.dot(a, w_ref[rows, cols].astype(BF16), preferred_element_type=F32)


def _mm_t(a, wt_ref, rows):
    return lax.dot_general(a, wt_ref[rows, :].astype(BF16), _NT, preferred_element_type=F32)


def _decay_columns(lf, carry_ref, p_ref, oq_ref, ok_ref):
    rows = lf.shape[0]
    row = lax.broadcasted_iota(jnp.int32, lf.shape, 0)
    lane = lax.broadcasted_iota(jnp.int32, lf.shape, 1)
    f = lf
    sh = 1
    while sh < rows:
        f = f + jnp.where(row >= sh, pltpu.roll(f, sh, 0), 0.0)
        sh *= 2
    f = f + carry_ref[...]
    carry_ref[...] = f[rows - 1:rows, :]
    f_hi = f.astype(BF16).astype(F32)
    r1 = f - f_hi
    f_mid = r1.astype(BF16).astype(F32)
    f_lo = (r1 - f_mid).astype(BF16).astype(F32)
    keep = lane < FOX_HEADS
    zero = jnp.zeros_like(f)
    pieces = (jnp.where(keep, f_hi, zero) + pltpu.roll(jnp.where(keep, f_mid, zero), FOX_HEADS, 1)
              + pltpu.roll(jnp.where(keep, f_lo, zero), 2 * FOX_HEADS, 1))
    placed = jnp.dot(pieces.astype(BF16), p_ref[...], preferred_element_type=F32)
    return ((placed[:, :LANES] + oq_ref[...]).astype(BF16), (placed[:, LANES:] + ok_ref[...]).astype(BF16))


def _decay_constants():
    p = np.zeros((LANES, 2 * LANES), np.float32)
    oq = np.zeros((1, LANES), np.float32)
    ok = np.zeros((1, LANES), np.float32)
    for h in range(FOX_HEADS):
        for j in range(3):
            p[j * FOX_HEADS + h, h * AUG_STRIDE + j] = 1.0
            p[j * FOX_HEADS + h, LANES + h * AUG_STRIDE + 3 + j] = -1.0
            oq[0, h * AUG_STRIDE + 3 + j] = 1.0
            ok[0, h * AUG_STRIDE + j] = 1.0
    return jnp.asarray(p, BF16), jnp.asarray(oq), jnp.asarray(ok)


def _proj_kernel(steps_per_batch, x_ref, g_ref, wt_ref, bf_ref, gq_ref, gk_ref, gm_ref, p_ref, oq_ref, ok_ref,
                 sbq_ref, sbk_ref, sbv_ref, fxq_ref, fxk_ref, fxv_ref, mq_ref, aq_ref, ak_ref,
                 carry_ref, wbf_ref):
    @pl.when(pl.program_id(0) == 0)
    def _():
        wbf_ref[...] = wt_ref[...].astype(BF16)

    @pl.when(pl.program_id(0) % steps_per_batch == 0)
    def _():
        carry_ref[...] = jnp.zeros_like(carry_ref)

    h = _rms(x_ref[...], g_ref[...]).astype(BF16)

    def mm(c0, n):
        return lax.dot_general(h, wbf_ref[c0:c0 + n, :], _NT, preferred_element_type=F32)

    c0 = 3 * D_SB
    c_f = c0 + 3 * D_FOX
    log_f = _log_sigmoid(mm(c_f, LANES) + bf_ref[...])
    mq_ref[...] = _headnorm(mm(c_f + FOX_HEADS, D_MEM), gm_ref[...], MEM_HD).astype(BF16)
    fxq_ref[...] = (_headnorm(mm(c0, D_FOX), gq_ref[...], HD) * HD ** -0.5).astype(BF16)
    fxk_ref[...] = _headnorm(mm(c0 + D_FOX, D_FOX), gk_ref[...], HD).astype(BF16)
    aq_ref[...], ak_ref[...] = _decay_columns(log_f, carry_ref, p_ref, oq_ref, ok_ref)
    fxv_ref[...] = mm(c0 + 2 * D_FOX, D_FOX).astype(BF16)
    sbq_ref[...] = (mm(0, D_SB) * HD ** -0.5).astype(BF16)
    sbk_ref[...] = mm(D_SB, D_SB).astype(BF16)
    sbv_ref[...] = mm(2 * D_SB, D_SB).astype(BF16)


def _proj(x2, batch, g, w_in_t, bf, gq, gk, gm):
    t = x2.shape[0]
    n_rows = 3 * D_SB + 3 * D_FOX + FOX_HEADS + D_MEM
    p, oq, ok = _decay_constants()
    row = lambda n: pl.BlockSpec((PROJ_ROW_TILE, n), lambda i: (i, 0))
    out_shapes = [jax.ShapeDtypeStruct((t, D_SB), BF16)] * 7 + [jax.ShapeDtypeStruct((t, LANES), BF16)] * 2
    return pl.pallas_call(
        functools.partial(_proj_kernel, t // batch // PROJ_ROW_TILE),
        grid=(t // PROJ_ROW_TILE,),
        in_specs=[row(D_MODEL), _const_spec((1, D_MODEL)), _const_spec((n_rows, D_MODEL)),
                  _const_spec((1, LANES)), _const_spec((1, D_FOX)), _const_spec((1, D_FOX)),
                  _const_spec((1, D_MEM)), _const_spec(p.shape), _const_spec(oq.shape), _const_spec(ok.shape)],
        out_specs=[row(D_SB)] * 7 + [row(LANES)] * 2,
        out_shape=out_shapes,
        scratch_shapes=[pltpu.VMEM((1, LANES), F32), pltpu.VMEM((n_rows, D_MODEL), BF16)],
        compiler_params=_params(1),
        name="proj",
    )(x2, g, w_in_t, bf, gq, gk, gm, p, oq, ok)


def _stack_heads(x, lane):
    zero = jnp.zeros_like(x)
    return jnp.concatenate([jnp.where(lane < HD, x, zero), jnp.where(lane >= HD, x, zero)], axis=0)


def _unstack_heads(x, lane):
    rows = x.shape[0] // 2
    return jnp.where(lane < HD, x[:rows], x[rows:])


def _softplus(z):
    return jnp.maximum(z, 0.0) + jnp.log(1.0 + jnp.exp2(jnp.abs(z) * (-_LOG2E)))


def _sb_kernel(q_ref, k_ref, v_ref, m_ref, o_ref, flag_ref, acc_ref, carry_ref):
    s_len = q_ref.shape[1]
    t = ATT_K_TILE
    nb = s_len // t
    lane = lax.broadcasted_iota(jnp.int32, (t, LANES), 1)
    tri = (lax.broadcasted_iota(jnp.int32, (2 * t, t), 1)
           < (lax.broadcasted_iota(jnp.int32, (2 * t, t), 0) & (t - 1)))
    msum = m_ref[...]
    reps = t // LANES
    order = [(pr, j) for pr in range(ATT_PAIRS) for j in reversed(range(nb))]
    z, sp, c, acc_diag, carry_diag = {}, {}, {}, {}, {}

    def qblock(pr, j):
        return _stack_heads(q_ref[0, j * t:(j + 1) * t, pr * LANES:(pr + 1) * LANES], lane)

    def scores(g):
        pr, j = order[g]
        lhs = qblock(pr, j) if j == nb - 1 else jnp.concatenate([qblock(pr, j), qblock(pr, j + 1)], axis=0)
        z[g] = lax.dot_general(lhs, k_ref[0, j * t:(j + 1) * t, pr * LANES:(pr + 1) * LANES], _NT,
                               preferred_element_type=F32)

    def suffix_sums(g):
        s = _softplus(z[g])
        top = jnp.where(tri, s[:2 * t], 0.0)
        s = top if order[g][1] == nb - 1 else jnp.concatenate([top, s[2 * t:]], axis=0)
        sp[g] = s
        z[g] = z[g] - s
        c[g] = jnp.dot(s.astype(BF16), msum, preferred_element_type=F32)

    def weights_pv(g):
        pr, j = order[g]
        ls = slice(pr * LANES, (pr + 1) * LANES)
        zc = z.pop(g) - c.pop(g)
        w = jnp.where(tri, jnp.exp(zc[:2 * t]), 0.0)
        if j < nb - 1:
            w_prev = jnp.exp(zc[2 * t:] - jnp.concatenate([carry_diag[(pr, j + 1)]] * reps, axis=1))
            w = jnp.concatenate([w, w_prev], axis=0)
        pv = jnp.dot(w.astype(BF16), v_ref[0, j * t:(j + 1) * t, ls], preferred_element_type=F32)
        rs = jnp.sum(sp.pop(g), axis=1, keepdims=True)
        if j < nb - 1:
            done = acc_diag.pop((pr, j + 1)) + pv[2 * t:]
            o_ref[0, (j + 1) * t:(j + 2) * t, ls] = _unstack_heads(done, lane).astype(BF16)
            total = carry_diag.pop((pr, j + 1)) + rs[2 * t:]
            if j >= 1:
                flag_ref[pr * nb + j + 1] = jnp.min(total)
        if j == 0:
            o_ref[0, 0:t, ls] = _unstack_heads(pv[:2 * t], lane).astype(BF16)
        else:
            acc_diag[(pr, j)] = pv[:2 * t]
            carry_diag[(pr, j)] = jnp.broadcast_to(rs[:2 * t], (2 * t, LANES))

    n_groups = len(order)
    for step in range(n_groups + 2):
        if step < n_groups:
            scores(step)
        if 1 <= step <= n_groups:
            suffix_sums(step - 1)
        if step >= 2:
            weights_pv(step - 2)

    def recompute_block(pr, jb):
        ls = slice(pr * LANES, (pr + 1) * LANES)
        rows = pl.ds(pl.multiple_of(jb * t, t), t)
        qb = _stack_heads(q_ref[0, rows, ls], lane)
        acc_ref[...] = jnp.zeros_like(acc_ref)
        carry_ref[...] = jnp.zeros_like(carry_ref)

        def tile(kb, diag):
            keys = pl.ds(pl.multiple_of(kb * t, t), t)
            zz = lax.dot_general(qb, k_ref[0, keys, ls], _NT, preferred_element_type=F32)
            s = _softplus(zz)
            if diag:
                s = jnp.where(tri, s, 0.0)
            cc = jnp.dot(s.astype(BF16), msum, preferred_element_type=F32)
            w = jnp.exp(zz - s - cc - jnp.concatenate([carry_ref[...]] * reps, axis=1))
            if diag:
                w = jnp.where(tri, w, 0.0)
            acc_ref[...] += jnp.dot(w.astype(BF16), v_ref[0, keys, ls], preferred_element_type=F32)
            carry_ref[...] += jnp.sum(s, axis=1, keepdims=True)

        tile(jb, True)

        def body(i, _):
            tile(jb - 1 - i, False)
            return 0

        lax.fori_loop(0, jb, body, 0)
        o_ref[0, rows, ls] = _unstack_heads(acc_ref[...], lane).astype(BF16)

    for pr in range(ATT_PAIRS):
        def redo(jb, _, pr=pr):
            pl.when(flag_ref[pr * nb + jb] < SB_EXIT)(lambda: recompute_block(pr, jb))
            return 0

        lax.fori_loop(2, nb, redo, 0)


def _sb_attention(q, k, v):
    b, s, d = q.shape
    t = ATT_K_TILE
    width = ATT_PAIRS * LANES
    ii = np.arange(t)
    msum = jnp.asarray(ii[:, None] > ii[None, :], BF16)
    spec = pl.BlockSpec((1, s, width), lambda bi, g: (bi, 0, g))
    return pl.pallas_call(
        _sb_kernel,
        grid=(b, d // width),
        in_specs=[spec, spec, spec, _const_spec((t, t))],
        out_specs=spec,
        out_shape=jax.ShapeDtypeStruct((b, s, d), BF16),
        scratch_shapes=[pltpu.SMEM((ATT_PAIRS * (s // t),), F32), pltpu.VMEM((2 * t, LANES), F32),
                        pltpu.VMEM((2 * t, LANES), F32)],
        compiler_params=_params(2),
        name="sb_attn",
    )(q, k, v, msum)


def _fox_kernel(q_ref, aq_ref, k_ref, ak_ref, v_ref, o_ref):
    s_len = q_ref.shape[1]
    t = ATT_Q_TILE
    hb = t // 2
    lane = lax.broadcasted_iota(jnp.int32, (hb, LANES), 1)
    group = lane // AUG_STRIDE
    tri = (lax.broadcasted_iota(jnp.int32, (t, hb), 1)
           <= (lax.broadcasted_iota(jnp.int32, (t, hb), 0) & (hb - 1)))
    ones = jnp.ones((t, LANES), BF16)
    tiles = []
    for pr in range(ATT_PAIRS):
        for qi in range(s_len // t):
            tiles += [(pr, qi, "full", kb * t, t) for kb in range(qi)]
            tiles += [(pr, qi, "diag_all", qi * t, hb), (pr, qi, "diag_b", qi * t + hb, hb)]
    qext, kvext, z, m, acc = {}, {}, {}, {}, {}

    def stacked_q(pr, qi):
        ls = slice(pr * LANES, (pr + 1) * LANES)
        pair = ATT_PAIRS * pl.program_id(1) + pr
        parts = []
        for half in range(2):
            rows = slice(qi * t + half * hb, qi * t + (half + 1) * hb)
            aq = aq_ref[0, rows, :]
            zero = jnp.zeros_like(aq)
            aug = jnp.concatenate([jnp.where(group == 2 * pair, aq, zero),
                                   jnp.where(group == 2 * pair + 1, aq, zero)], axis=0)
            parts.append(jnp.concatenate([_stack_heads(q_ref[0, rows, ls], lane), aug], axis=1))
        return jnp.concatenate(parts, axis=0)

    def keys_values(pr, start, size):
        if (pr, start, size) not in kvext:
            ls = slice(pr * LANES, (pr + 1) * LANES)
            keys = slice(start, start + size)
            kext = jnp.concatenate([k_ref[0, keys, ls], ak_ref[0, keys, :]], axis=1)
            vext = jnp.concatenate([v_ref[0, keys, ls], ones[:size]], axis=1)
            kvext[(pr, start, size)] = (kext, vext)
        return kvext[(pr, start, size)]

    def scores(n):
        pr, qi, kind, start, size = tiles[n]
        if (pr, qi) not in qext:
            qext[(pr, qi)] = stacked_q(pr, qi)
        lhs = qext[(pr, qi)]
        kext = keys_values(pr, start, size)[0]
        if kind == "diag_b":
            s = lax.dot_general(lhs[t:], kext, _NT, preferred_element_type=F32)
            s = jnp.where(tri, s, NEG_INF)
        else:
            s = lax.dot_general(lhs, kext, _NT, preferred_element_type=F32)
            if kind == "diag_all":
                s = jnp.concatenate([jnp.where(tri, s[:t], NEG_INF), s[t:]], axis=0)
        z[n] = s

    def softmax_pv(n):
        pr, qi, kind, start, size = tiles[n]
        key = (pr, qi)
        s = z.pop(n)
        vext = keys_values(pr, start, size)[1]
        row_max = jnp.max(s, axis=1, keepdims=True)
        if key not in m:
            m_new = jnp.broadcast_to(row_max, (s.shape[0], LANES))
        else:
            m_old = m[key][t:] if kind == "diag_b" else m[key]
            m_new = jnp.maximum(m_old, row_max)
        p = jnp.exp(s - jnp.concatenate([m_new] * (size // LANES), axis=1))
        pv = jnp.dot(p.astype(BF16), vext, preferred_element_type=F32)
        if key not in m:
            acc[key], m[key] = pv, m_new
        else:
            acc_old = acc[key][t:] if kind == "diag_b" else acc[key]
            alpha = jnp.exp(m_old - m_new)
            acc_new = jnp.concatenate([alpha, alpha], axis=1) * acc_old + pv
            if kind == "diag_b":
                acc[key] = jnp.concatenate([acc[key][:t], acc_new], axis=0)
                m[key] = jnp.concatenate([m[key][:t], m_new], axis=0)
            else:
                acc[key], m[key] = acc_new, m_new
        if kind == "diag_b":
            a = acc.pop(key)
            o = a[:, :LANES] / a[:, LANES:]
            out = jnp.concatenate([_unstack_heads(o[:t], lane), _unstack_heads(o[t:], lane)], axis=0)
            o_ref[0, qi * t:(qi + 1) * t, pr * LANES:(pr + 1) * LANES] = out.astype(BF16)
            m.pop(key)
            qext.pop(key)

    n_tiles = len(tiles)
    for step in range(n_tiles + 1):
        if step < n_tiles:
            scores(step)
        if step >= 1:
            softmax_pv(step - 1)


def _fox_attention(q, k, v, aq, ak):
    b, s, d = q.shape
    width = ATT_PAIRS * LANES
    spec = pl.BlockSpec((1, s, width), lambda bi, g: (bi, 0, g))
    aspec = pl.BlockSpec((1, s, LANES), lambda bi, g: (bi, 0, 0))
    return pl.pallas_call(
        _fox_kernel,
        grid=(b, d // width),
        in_specs=[spec, aspec, spec, aspec, spec],
        out_specs=spec,
        out_shape=jax.ShapeDtypeStruct((b, s, d), BF16),
        compiler_params=_params(2),
        name="fox_attn",
    )(q, aq, k, ak, v)


def _memory_attention(q_ref, mk_ref, mv_ref):
    z, out = {}, []

    def scores(h):
        sl = slice(h * MEM_HD, (h + 1) * MEM_HD)
        z[h] = lax.dot_general(q_ref[:, sl], mk_ref[:, sl], _NT,
                               preferred_element_type=F32) * MEM_HD ** -0.5

    def softmax_pv(h):
        sl = slice(h * MEM_HD, (h + 1) * MEM_HD)
        zh = z.pop(h)
        e = jnp.exp(zh - jnp.max(zh, axis=1, keepdims=True))
        p = e / jnp.sum(e, axis=1, keepdims=True)
        out.append(jnp.dot(p.astype(BF16), mv_ref[:, sl], preferred_element_type=F32).astype(BF16))

    for step in range(MEM_HEADS + 1):
        if step < MEM_HEADS:
            scores(step)
        if step >= 1:
            softmax_pv(step - 1)
    return jnp.concatenate(out, axis=1)


def _merge_kernel(steps_per_batch, x_ref, g_ref, wgt_ref, osb_ref, ofx_ref, mq_ref,
                  mem_ref, gmem_ref, wkv_ref, gk_ref,
                  wsb_ref, wfx_ref, wmem_ref, wout_ref, o_ref, mk_ref, mv_ref):
    @pl.when(pl.program_id(0) % steps_per_batch == 0)
    def _():
        mh = _rms(mem_ref[0], gmem_ref[...]).astype(BF16)
        kv = _mm(mh, wkv_ref)
        mk_ref[...] = _headnorm(kv[:, :D_MEM], gk_ref[...], MEM_HD).astype(BF16)
        mv_ref[...] = kv[:, D_MEM:].astype(BF16)

    x = x_ref[...]
    h = _rms(x, g_ref[...]).astype(BF16)
    branches = ((osb_ref[...], wsb_ref), (ofx_ref[...], wfx_ref),
                (_memory_attention(mq_ref, mk_ref, mv_ref), wmem_ref))
    merged = None
    for i, (o_br, w_br) in enumerate(branches):
        gate = jax.nn.sigmoid(_mm_t(h, wgt_ref, slice(i * D_MODEL, (i + 1) * D_MODEL)))
        term = gate * _mm(o_br, w_br)
        merged = term if merged is None else merged + term
    o_ref[...] = x + _mm(merged.astype(BF16), wout_ref)


def _merge(x2, g, w_in_t, osb, ofx, mq, mem, gmem, wkv, gk, wsb, wfx, wmem, wout):
    t = x2.shape[0]
    b, m, _ = mem.shape
    steps_per_batch = t // b // ROW_TILE
    n_gate = N_BRANCH * D_MODEL
    gate_rows = pl.BlockSpec((pl.Element(n_gate), pl.Element(D_MODEL)),
                             lambda i: (w_in_t.shape[0] - n_gate, 0),
                             pipeline_mode=pl.Buffered(1))
    row = lambda n: pl.BlockSpec((ROW_TILE, n), lambda i: (i, 0))
    return pl.pallas_call(
        functools.partial(_merge_kernel, steps_per_batch),
        grid=(t // ROW_TILE,),
        in_specs=[row(D_MODEL), _const_spec((1, D_MODEL)), gate_rows,
                  row(D_SB), row(D_FOX), row(D_MEM),
                  pl.BlockSpec((1, m, D_MODEL), lambda i: (i // steps_per_batch, 0, 0)),
                  _const_spec((1, D_MODEL)), _const_spec(wkv.shape), _const_spec((1, D_MEM)),
                  _const_spec(wsb.shape), _const_spec(wfx.shape), _const_spec(wmem.shape),
                  _const_spec(wout.shape)],
        out_specs=row(D_MODEL),
        out_shape=jax.ShapeDtypeStruct((t, D_MODEL), F32),
        scratch_shapes=[pltpu.VMEM((m, D_MEM), BF16), pltpu.VMEM((m, D_MEM), BF16)],
        compiler_params=_params(1),
        name="merge",
    )(x2, g, w_in_t, osb, ofx, mq, mem, gmem, wkv, gk, wsb, wfx, wmem, wout)


def _mlp_kernel(x_ref, g_ref, wup_hbm, wdn_hbm, o_ref, wup_ref, wdn_ref, up_stage, dn_stage, sem):
    n_chunks = D_FF // FF_CHUNK

    def chunk_copies(c, slot):
        sl = pl.ds(c * FF_CHUNK, FF_CHUNK)
        return (pltpu.make_async_copy(wup_hbm.at[:, sl], up_stage.at[slot], sem.at[0, slot]),
                pltpu.make_async_copy(wdn_hbm.at[sl, :], dn_stage.at[slot], sem.at[1, slot]))

    def body(stream_weights):
        if stream_weights:
            for cp in chunk_copies(0, 0):
                cp.start()
        x = x_ref[...]
        h = _rms(x, g_ref[...]).astype(BF16)
        acc = x
        for c in range(n_chunks):
            sl = slice(c * FF_CHUNK, (c + 1) * FF_CHUNK)
            if stream_weights:
                slot = c % 2
                if c + 1 < n_chunks:
                    for cp in chunk_copies(c + 1, 1 - slot):
                        cp.start()
                for cp in chunk_copies(c, slot):
                    cp.wait()
                wup_ref[:, sl] = up_stage[slot].astype(BF16)
                wdn_ref[sl, :] = dn_stage[slot].astype(BF16)
            u = jnp.maximum(jnp.dot(h, wup_ref[:, sl], preferred_element_type=F32), 0.0)
            acc = acc + jnp.dot((u * u).astype(BF16), wdn_ref[sl, :], preferred_element_type=F32)
        o_ref[...] = acc

    pl.when(pl.program_id(0) == 0)(lambda: body(True))
    pl.when(pl.program_id(0) > 0)(lambda: body(False))


def _mlp(x2, g, wup, wdn):
    t = x2.shape[0]
    row = pl.BlockSpec((MLP_ROW_TILE, D_MODEL), lambda i: (i, 0))
    hbm = pl.BlockSpec(memory_space=pl.ANY)
    return pl.pallas_call(
        _mlp_kernel,
        grid=(t // MLP_ROW_TILE,),
        in_specs=[row, _const_spec((1, D_MODEL)), hbm, hbm],
        out_specs=row,
        out_shape=jax.ShapeDtypeStruct((t, D_MODEL), F32),
        scratch_shapes=[pltpu.VMEM((D_MODEL, D_FF), BF16), pltpu.VMEM((D_FF, D_MODEL), BF16),
                        pltpu.VMEM((2, D_MODEL, FF_CHUNK), F32), pltpu.VMEM((2, FF_CHUNK, D_MODEL), F32),
                        pltpu.SemaphoreType.DMA((2, 2))],
        compiler_params=_params(1),
        name="mlp",
    )(x2, g, wup, wdn)


def _layer(x, mem, g_mix, g_memn, w_in, b_forget, g_fq, g_fk, g_mq, g_mk, w_mem_kv,
           w_sb, w_fox, w_mem, w_out, g_mlp, w_up, w_dn):
    b, s, _ = x.shape
    t = b * s
    x2 = x.reshape(t, D_MODEL)
    n_qkv = 3 * D_SB + 3 * D_FOX
    w_in_t = w_in.T
    bf = jnp.pad(b_forget, (0, LANES - FOX_HEADS)).reshape(1, LANES)
    row = lambda a: a.reshape(1, -1)

    sbq, sbk, sbv, fxq, fxk, fxv, mq, aq, ak = _proj(
        x2, b, row(g_mix), w_in_t, bf, row(jnp.tile(g_fq, FOX_HEADS)),
        row(jnp.tile(g_fk, FOX_HEADS)), row(jnp.tile(g_mq, MEM_HEADS)))
    to3 = lambda a: a.reshape(b, s, a.shape[-1])

    o_sb = _sb_attention(to3(sbq), to3(sbk), to3(sbv))
    o_fox = _fox_attention(to3(fxq), to3(fxk), to3(fxv), to3(aq), to3(ak))
    x1 = _merge(x2, row(g_mix), w_in_t, o_sb.reshape(t, D_SB), o_fox.reshape(t, D_FOX), mq,
                mem, row(g_memn), w_mem_kv, row(jnp.tile(g_mk, MEM_HEADS)), w_sb, w_fox, w_mem, w_out)
    out = _mlp(x1, row(g_mlp), w_up, w_dn)
    return out.reshape(b, s, D_MODEL)


def kernel(x, mem, g_mix_norm, g_mem_norm, w_in, b_forget, g_fox_q, g_fox_k, g_mem_q, g_mem_k,
           w_mem_kv, w_branch_sb, w_branch_fox, w_branch_mem, w_out, g_mlp_norm, w_ff_up, w_ff_down):
    for l in range(w_in.shape[0]):
        x = _layer(x, mem, g_mix_norm[l], g_mem_norm[l], w_in[l], b_forget[l], g_fox_q[l], g_fox_k[l],
                   g_mem_q[l], g_mem_k[l], w_mem_kv[l], w_branch_sb[l], w_branch_fox[l],
                   w_branch_mem[l], w_out[l], g_mlp_norm[l], w_ff_up[l], w_ff_down[l])
    return x
```

```python
import functools

import numpy as np
import jax
import jax.numpy as jnp
from jax import lax
from jax.experimental import pallas as pl
from jax.experimental.pallas import tpu as pltpu

D_MODEL = 1024
HD = 64
SB_HEADS = 8
FOX_HEADS = 8
MEM_HEADS = 4
MEM_HD = 128
D_SB = SB_HEADS * HD
D_FOX = FOX_HEADS * HD
D_MEM = MEM_HEADS * MEM_HD
N_BRANCH = 3
D_FF = 4 * D_MODEL
EPS = 1e-6
NEG_INF = -1e30

LANES = 128
AUG_STRIDE = 8
ROW_TILE = 512
PROJ_ROW_TILE = 1024
MLP_ROW_TILE = 512
ATT_Q_TILE = 512
ATT_K_TILE = 256
ATT_PAIRS = 2
SB_EXIT = 110.0
FF_CHUNK = 512
VMEM_LIMIT = 56 * 1024 * 1024

F32 = jnp.float32
BF16 = jnp.bfloat16
_NT = (((1,), (1,)), ((), ()))
_LOG2E = 1.4426950408889634


def _const_spec(shape):
    nd = len(shape)
    return pl.BlockSpec(shape, lambda *_: (0,) * nd, pipeline_mode=pl.Buffered(1))


def _params(n_axes):
    return pltpu.CompilerParams(
        dimension_semantics=("arbitrary",) * n_axes, vmem_limit_bytes=VMEM_LIMIT)


def _rms(x, g):
    ms = jnp.sum(x * x, axis=-1, keepdims=True) * (1.0 / x.shape[-1])
    return (x * lax.rsqrt(ms + EPS)) * g


def _headnorm(p, g, hd):
    rows, n = p.shape
    lane = lax.broadcasted_iota(jnp.int32, (rows, LANES), 1)
    outs = []
    for j in range(n // LANES):
        c = p[:, j * LANES:(j + 1) * LANES]
        c2 = c * c
        if hd == LANES:
            ms = jnp.sum(c2, axis=1, keepdims=True) * (1.0 / LANES)
        else:
            lo = jnp.sum(jnp.where(lane < hd, c2, 0.0), axis=1, keepdims=True)
            hi = jnp.sum(jnp.where(lane >= hd, c2, 0.0), axis=1, keepdims=True)
            ms = jnp.where(lane < hd, lo, hi) * (1.0 / hd)
        outs.append(c * lax.rsqrt(ms + EPS))
    return jnp.concatenate(outs, axis=1) * g


def _log_sigmoid(x):
    return jnp.minimum(x, 0.0) - jnp.log(1.0 + jnp.exp(-jnp.abs(x)))


def _decay_columns(lf, carry_ref, p_ref, oq_ref, ok_ref):
    rows = lf.shape[0]
    row = lax.broadcasted_iota(jnp.int32, lf.shape, 0)
    lane = lax.broadcasted_iota(jnp.int32, lf.shape, 1)
    f = lf
    sh = 1
    while sh < rows:
        f = f + jnp.where(row >= sh, pltpu.roll(f, sh, 0), 0.0)
        sh *= 2
    f = f + carry_ref[...]
    carry_ref[...] = f[rows - 1:rows, :]
    f_hi = f.astype(BF16).astype(F32)
    r1 = f - f_hi
    f_mid = r1.astype(BF16).astype(F32)
    f_lo = (r1 - f_mid).astype(BF16).astype(F32)
    keep = lane < FOX_HEADS
    zero = jnp.zeros_like(f)
    pieces = (jnp.where(keep, f_hi, zero) + pltpu.roll(jnp.where(keep, f_mid, zero), FOX_HEADS, 1)
              + pltpu.roll(jnp.where(keep, f_lo, zero), 2 * FOX_HEADS, 1))
    placed = jnp.dot(pieces.astype(BF16), p_ref[...], preferred_element_type=F32)
    return ((placed[:, :LANES] + oq_ref[...]).astype(BF16), (placed[:, LANES:] + ok_ref[...]).astype(BF16))


def _decay_constants():
    p = np.zeros((LANES, 2 * LANES), np.float32)
    oq = np.zeros((1, LANES), np.float32)
    ok = np.zeros((1, LANES), np.float32)
    for h in range(FOX_HEADS):
        for j in range(3):
            p[j * FOX_HEADS + h, h * AUG_STRIDE + j] = 1.0
            p[j * FOX_HEADS + h, LANES + h * AUG_STRIDE + 3 + j] = -1.0
            oq[0, h * AUG_STRIDE + 3 + j] = 1.0
            ok[0, h * AUG_STRIDE + j] = 1.0
    return jnp.asarray(p, BF16), jnp.asarray(oq), jnp.asarray(ok)


def _proj_kernel(steps_per_batch, x_ref, g_ref, wt_ref, bf_ref, gq_ref, gk_ref, gm_ref, p_ref, oq_ref, ok_ref,
                 sbq_ref, sbk_ref, sbv_ref, fxq_ref, fxk_ref, fxv_ref, mq_ref, aq_ref, ak_ref,
                 carry_ref, wbf_ref):
    @pl.when(pl.program_id(0) == 0)
    def _():
        wbf_ref[...] = wt_ref[...].astype(BF16)

    @pl.when(pl.program_id(0) % steps_per_batch == 0)
    def _():
        carry_ref[...] = jnp.zeros_like(carry_ref)

    h = _rms(x_ref[...], g_ref[...]).astype(BF16)

    def mm(c0, n):
        return lax.dot_general(h, wbf_ref[c0:c0 + n, :], _NT, preferred_element_type=F32)

    c0 = 3 * D_SB
    c_f = c0 + 3 * D_FOX
    log_f = _log_sigmoid(mm(c_f, LANES) + bf_ref[...])
    mq_ref[...] = _headnorm(mm(c_f + FOX_HEADS, D_MEM), gm_ref[...], MEM_HD).astype(BF16)
    fxq_ref[...] = (_headnorm(mm(c0, D_FOX), gq_ref[...], HD) * HD ** -0.5).astype(BF16)
    fxk_ref[...] = _headnorm(mm(c0 + D_FOX, D_FOX), gk_ref[...], HD).astype(BF16)
    aq_ref[...], ak_ref[...] = _decay_columns(log_f, carry_ref, p_ref, oq_ref, ok_ref)
    fxv_ref[...] = mm(c0 + 2 * D_FOX, D_FOX).astype(BF16)
    sbq_ref[...] = (mm(0, D_SB) * HD ** -0.5).astype(BF16)
    sbk_ref[...] = mm(D_SB, D_SB).astype(BF16)
    sbv_ref[...] = mm(2 * D_SB, D_SB).astype(BF16)


def _proj(x2, batch, g, w_in_t, bf, gq, gk, gm):
    t = x2.shape[0]
    n_rows = 3 * D_SB + 3 * D_FOX + FOX_HEADS + D_MEM
    p, oq, ok = _decay_constants()
    row = lambda n: pl.BlockSpec((PROJ_ROW_TILE, n), lambda i: (i, 0))
    out_shapes = [jax.ShapeDtypeStruct((t, D_SB), BF16)] * 7 + [jax.ShapeDtypeStruct((t, LANES), BF16)] * 2
    return pl.pallas_call(
        functools.partial(_proj_kernel, t // batch // PROJ_ROW_TILE),
        grid=(t // PROJ_ROW_TILE,),
        in_specs=[row(D_MODEL), _const_spec((1, D_MODEL)), _const_spec((n_rows, D_MODEL)),
                  _const_spec((1, LANES)), _const_spec((1, D_FOX)), _const_spec((1, D_FOX)),
                  _const_spec((1, D_MEM)), _const_spec(p.shape), _const_spec(oq.shape), _const_spec(ok.shape)],
        out_specs=[row(D_SB)] * 7 + [row(LANES)] * 2,
        out_shape=out_shapes,
        scratch_shapes=[pltpu.VMEM((1, LANES), F32), pltpu.VMEM((n_rows, D_MODEL), BF16)],
        compiler_params=_params(1),
        name="proj",
    )(x2, g, w_in_t, bf, gq, gk, gm, p, oq, ok)


def _stack_heads(x, lane):
    zero = jnp.zeros_like(x)
    return jnp.concatenate([jnp.where(lane < HD, x, zero), jnp.where(lane >= HD, x, zero)], axis=0)


def _unstack_heads(x, lane):
    rows = x.shape[0] // 2
    return jnp.where(lane < HD, x[:rows], x[rows:])


def _softplus(z):
    return jnp.maximum(z, 0.0) + jnp.log(1.0 + jnp.exp2(jnp.abs(z) * (-_LOG2E)))


def _sb_kernel(q_ref, k_ref, v_ref, m_ref, o_ref, flag_ref, acc_ref, carry_ref):
    s_len = q_ref.shape[1]
    t = ATT_K_TILE
    nb = s_len // t
    lane = lax.broadcasted_iota(jnp.int32, (t, LANES), 1)
    tri = (lax.broadcasted_iota(jnp.int32, (2 * t, t), 1)
           < (lax.broadcasted_iota(jnp.int32, (2 * t, t), 0) & (t - 1)))
    msum = m_ref[...]
    reps = t // LANES
    order = [(pr, j) for pr in range(ATT_PAIRS) for j in reversed(range(nb))]
    z, sp, c, acc_diag, carry_diag = {}, {}, {}, {}, {}

    def qblock(pr, j):
        return _stack_heads(q_ref[0, j * t:(j + 1) * t, pr * LANES:(pr + 1) * LANES], lane)

    def scores(g):
        pr, j = order[g]
        lhs = qblock(pr, j) if j == nb - 1 else jnp.concatenate([qblock(pr, j), qblock(pr, j + 1)], axis=0)
        z[g] = lax.dot_general(lhs, k_ref[0, j * t:(j + 1) * t, pr * LANES:(pr + 1) * LANES], _NT,
                               preferred_element_type=F32)

    def suffix_sums(g):
        s = _softplus(z[g])
        top = jnp.where(tri, s[:2 * t], 0.0)
        s = top if order[g][1] == nb - 1 else jnp.concatenate([top, s[2 * t:]], axis=0)
        sp[g] = s
        z[g] = z[g] - s
        c[g] = jnp.dot(s.astype(BF16), msum, preferred_element_type=F32)

    def weights_pv(g):
        pr, j = order[g]
        ls = slice(pr * LANES, (pr + 1) * LANES)
        zc = z.pop(g) - c.pop(g)
        w = jnp.where(tri, jnp.exp(zc[:2 * t]), 0.0)
        if j < nb - 1:
            w_prev = jnp.exp(zc[2 * t:] - jnp.concatenate([carry_diag[(pr, j + 1)]] * reps, axis=1))
            w = jnp.concatenate([w, w_prev], axis=0)
        pv = jnp.dot(w.astype(BF16), v_ref[0, j * t:(j + 1) * t, ls], preferred_element_type=F32)
        rs = jnp.sum(sp.pop(g), axis=1, keepdims=True)
        if j < nb - 1:
            done = acc_diag.pop((pr, j + 1)) + pv[2 * t:]
            o_ref[0, (j + 1) * t:(j + 2) * t, ls] = _unstack_heads(done, lane).astype(BF16)
            total = carry_diag.pop((pr, j + 1)) + rs[2 * t:]
            if j >= 1:
                flag_ref[pr * nb + j + 1] = jnp.min(total)
        if j == 0:
            o_ref[0, 0:t, ls] = _unstack_heads(pv[:2 * t], lane).astype(BF16)
        else:
            acc_diag[(pr, j)] = pv[:2 * t]
            carry_diag[(pr, j)] = jnp.broadcast_to(rs[:2 * t], (2 * t, LANES))

    n_groups = len(order)
    for step in range(n_groups + 2):
        if step < n_groups:
            scores(step)
        if 1 <= step <= n_groups:
            suffix_sums(step - 1)
        if step >= 2:
            weights_pv(step - 2)

    def recompute_block(pr, jb):
        ls = slice(pr * LANES, (pr + 1) * LANES)
        rows = pl.ds(pl.multiple_of(jb * t, t), t)
        qb = _stack_heads(q_ref[0, rows, ls], lane)
        acc_ref[...] = jnp.zeros_like(acc_ref)
        carry_ref[...] = jnp.zeros_like(carry_ref)

        def tile(kb, diag):
            keys = pl.ds(pl.multiple_of(kb * t, t), t)
            zz = lax.dot_general(qb, k_ref[0, keys, ls], _NT, preferred_element_type=F32)
            s = _softplus(zz)
            if diag:
                s = jnp.where(tri, s, 0.0)
            cc = jnp.dot(s.astype(BF16), msum, preferred_element_type=F32)
            w = jnp.exp(zz - s - cc - jnp.concatenate([carry_ref[...]] * reps, axis=1))
            if diag:
                w = jnp.where(tri, w, 0.0)
            acc_ref[...] += jnp.dot(w.astype(BF16), v_ref[0, keys, ls], preferred_element_type=F32)
            carry_ref[...] += jnp.sum(s, axis=1, keepdims=True)

        tile(jb, True)

        def body(i, _):
            tile(jb - 1 - i, False)
            return 0

        lax.fori_loop(0, jb, body, 0)
        o_ref[0, rows, ls] = _unstack_heads(acc_ref[...], lane).astype(BF16)

    for pr in range(ATT_PAIRS):
        def redo(jb, _, pr=pr):
            pl.when(flag_ref[pr * nb + jb] < SB_EXIT)(lambda: recompute_block(pr, jb))
            return 0

        lax.fori_loop(2, nb, redo, 0)


def _sb_attention(q, k, v):
    b, s, d = q.shape
    t = ATT_K_TILE
    width = ATT_PAIRS * LANES
    ii = np.arange(t)
    msum = jnp.asarray(ii[:, None] > ii[None, :], BF16)
    spec = pl.BlockSpec((1, s, width), lambda bi, g: (bi, 0, g))
    return pl.pallas_call(
        _sb_kernel,
        grid=(b, d // width),
        in_specs=[spec, spec, spec, _const_spec((t, t))],
        out_specs=spec,
        out_shape=jax.ShapeDtypeStruct((b, s, d), BF16),
        scratch_shapes=[pltpu.SMEM((ATT_PAIRS * (s // t),), F32), pltpu.VMEM((2 * t, LANES), F32),
                        pltpu.VMEM((2 * t, LANES), F32)],
        compiler_params=_params(2),
        name="sb_attn",
    )(q, k, v, msum)


def _fox_kernel(q_ref, aq_ref, k_ref, ak_ref, v_ref, o_ref):
    s_len = q_ref.shape[1]
    t = ATT_Q_TILE
    hb = t // 2
    lane = lax.broadcasted_iota(jnp.int32, (hb, LANES), 1)
    group = lane // AUG_STRIDE
    tri = (lax.broadcasted_iota(jnp.int32, (t, hb), 1)
           <= (lax.broadcasted_iota(jnp.int32, (t, hb), 0) & (hb - 1)))
    ones = jnp.ones((t, LANES), BF16)
    tiles = []
    for pr in range(ATT_PAIRS):
        for qi in range(s_len // t):
            tiles += [(pr, qi, "full", kb * t, t) for kb in range(qi)]
            tiles += [(pr, qi, "diag_all", qi * t, hb), (pr, qi, "diag_b", qi * t + hb, hb)]
    qext, kvext, z, m, acc = {}, {}, {}, {}, {}

    def stacked_q(pr, qi):
        ls = slice(pr * LANES, (pr + 1) * LANES)
        pair = ATT_PAIRS * pl.program_id(1) + pr
        parts = []
        for half in range(2):
            rows = slice(qi * t + half * hb, qi * t + (half + 1) * hb)
            aq = aq_ref[0, rows, :]
            zero = jnp.zeros_like(aq)
            aug = jnp.concatenate([jnp.where(group == 2 * pair, aq, zero),
                                   jnp.where(group == 2 * pair + 1, aq, zero)], axis=0)
            parts.append(jnp.concatenate([_stack_heads(q_ref[0, rows, ls], lane), aug], axis=1))
        return jnp.concatenate(parts, axis=0)

    def keys_values(pr, start, size):
        if (pr, start, size) not in kvext:
            ls = slice(pr * LANES, (pr + 1) * LANES)
            keys = slice(start, start + size)
            kext = jnp.concatenate([k_ref[0, keys, ls], ak_ref[0, keys, :]], axis=1)
            vext = jnp.concatenate([v_ref[0, keys, ls], ones[:size]], axis=1)
            kvext[(pr, start, size)] = (kext, vext)
        return kvext[(pr, start, size)]

    def scores(n):
        pr, qi, kind, start, size = tiles[n]
        if (pr, qi) not in qext:
            qext[(pr, qi)] = stacked_q(pr, qi)
        lhs = qext[(pr, qi)]
        kext = keys_values(pr, start, size)[0]
        if kind == "diag_b":
            s = lax.dot_general(lhs[t:], kext, _NT, preferred_element_type=F32)
            s = jnp.where(tri, s, NEG_INF)
        else:
            s = lax.dot_general(lhs, kext, _NT, preferred_element_type=F32)
            if kind == "diag_all":
                s = jnp.concatenate([jnp.where(tri, s[:t], NEG_INF), s[t:]], axis=0)
        z[n] = s

    def softmax_pv(n):
        pr, qi, kind, start, size = tiles[n]
        key = (pr, qi)
        s = z.pop(n)
        vext = keys_values(pr, start, size)[1]
        row_max = jnp.max(s, axis=1, keepdims=True)
        if key not in m:
            m_new = jnp.broadcast_to(row_max, (s.shape[0], LANES))
        else:
            m_old = m[key][t:] if kind == "diag_b" else m[key]
            m_new = jnp.maximum(m_old, row_max)
        p = jnp.exp(s - jnp.concatenate([m_new] * (size // LANES), axis=1))
        pv = jnp.dot(p.astype(BF16), vext, preferred_element_type=F32)
        if key not in m:
            acc[key], m[key] = pv, m_new
        else:
            acc_old = acc[key][t:] if kind == "diag_b" else acc[key]
            alpha = jnp.exp(m_old - m_new)
            acc_new = jnp.concatenate([alpha, alpha], axis=1) * acc_old + pv
            if kind == "diag_b":
                acc[key] = jnp.concatenate([acc[key][:t], acc_new], axis=0)
                m[key] = jnp.concatenate([m[key][:t], m_new], axis=0)
            else:
                acc[key], m[key] = acc_new, m_new
        if kind == "diag_b":
            a = acc.pop(key)
            o = a[:, :LANES] / a[:, LANES:]
            out = jnp.concatenate([_unstack_heads(o[:t], lane), _unstack_heads(o[t:], lane)], axis=0)
            o_ref[0, qi * t:(qi + 1) * t, pr * LANES:(pr + 1) * LANES] = out.astype(BF16)
            m.pop(key)
            qext.pop(key)

    n_tiles = len(tiles)
    for step in range(n_tiles + 1):
        if step < n_tiles:
            scores(step)
        if step >= 1:
            softmax_pv(step - 1)


def _fox_attention(q, k, v, aq, ak):
    b, s, d = q.shape
    width = ATT_PAIRS * LANES
    spec = pl.BlockSpec((1, s, width), lambda bi, g: (bi, 0, g))
    aspec = pl.BlockSpec((1, s, LANES), lambda bi, g: (bi, 0, 0))
    return pl.pallas_call(
        _fox_kernel,
        grid=(b, d // width),
        in_specs=[spec, aspec, spec, aspec, spec],
        out_specs=spec,
        out_shape=jax.ShapeDtypeStruct((b, s, d), BF16),
        compiler_params=_params(2),
        name="fox_attn",
    )(q, aq, k, ak, v)


def _memory_attention(q_ref, mk_ref, mv_ref):
    z, out = {}, []

    def scores(h):
        sl = slice(h * MEM_HD, (h + 1) * MEM_HD)
        z[h] = lax.dot_general(q_ref[:, sl], mk_ref[:, sl], _NT,
                               preferred_element_type=F32) * MEM_HD ** -0.5

    def softmax_pv(h):
        sl = slice(h * MEM_HD, (h + 1) * MEM_HD)
        zh = z.pop(h)
        e = jnp.exp(zh - jnp.max(zh, axis=1, keepdims=True))
        p = e / jnp.sum(e, axis=1, keepdims=True)
        out.append(jnp.dot(p.astype(BF16), mv_ref[:, sl], preferred_element_type=F32).astype(BF16))

    for step in range(MEM_HEADS + 1):
        if step < MEM_HEADS:
            scores(step)
        if step >= 1:
            softmax_pv(step - 1)
    return jnp.concatenate(out, axis=1)


def _merge_kernel(steps_per_batch, x_ref, g_ref, wint_hbm, osb_ref, ofx_ref, mq_ref,
                  mem_ref, gmem_ref, wkv_hbm, gk_ref, wsb_hbm, wfx_hbm, wmem_hbm, wout_hbm, o_ref,
                  mk_ref, mv_ref, wkv_ref, wgt_ref, wsb_ref, wfx_ref, wmem_ref, wout_ref, stage, sem):
    gate0 = wint_hbm.shape[0] - N_BRANCH * D_MODEL
    weights = [(wkv_hbm, wkv_ref)]
    for i, (src, dst) in enumerate(((wsb_hbm, wsb_ref), (wfx_hbm, wfx_ref), (wmem_hbm, wmem_ref))):
        rows = pl.ds(i * D_MODEL, D_MODEL)
        weights += [(wint_hbm.at[pl.ds(gate0 + i * D_MODEL, D_MODEL), :], wgt_ref.at[rows, :]), (src, dst)]
    weights.append((wout_hbm, wout_ref))

    def copy(k, slot):
        src = weights[k][0]
        return pltpu.make_async_copy(src, stage.at[slot, pl.ds(0, src.shape[0]), :], sem.at[slot])

    def body(stream_weights):
        fetched = [0]

        def fetch():
            if not stream_weights:
                return
            k = fetched[0]
            fetched[0] += 1
            slot = k % 2
            if k + 1 < len(weights):
                copy(k + 1, 1 - slot).start()
            copy(k, slot).wait()
            src, dst = weights[k]
            dst[...] = stage[slot, :src.shape[0], :].astype(BF16)

        def memory_kv():
            mh = _rms(mem_ref[0], gmem_ref[...]).astype(BF16)
            kv = jnp.dot(mh, wkv_ref[...], preferred_element_type=F32)
            mk_ref[...] = _headnorm(kv[:, :D_MEM], gk_ref[...], MEM_HD).astype(BF16)
            mv_ref[...] = kv[:, D_MEM:].astype(BF16)

        if stream_weights:
            copy(0, 0).start()
            fetch()
            memory_kv()
        else:
            pl.when(pl.program_id(0) % steps_per_batch == 0)(memory_kv)

        x = x_ref[...]
        h = _rms(x, g_ref[...]).astype(BF16)
        branches = ((osb_ref, wsb_ref), (ofx_ref, wfx_ref), (None, wmem_ref))
        merged = None
        for i, (o_br, w_br) in enumerate(branches):
            fetch()
            gate = jax.nn.sigmoid(lax.dot_general(h, wgt_ref[i * D_MODEL:(i + 1) * D_MODEL, :], _NT,
                                                  preferred_element_type=F32))
            o = _memory_attention(mq_ref, mk_ref, mv_ref) if o_br is None else o_br[...]
            fetch()
            term = gate * jnp.dot(o, w_br[...], preferred_element_type=F32)
            merged = term if merged is None else merged + term
        fetch()
        o_ref[...] = x + jnp.dot(merged.astype(BF16), wout_ref[...], preferred_element_type=F32)

    pl.when(pl.program_id(0) == 0)(lambda: body(True))
    pl.when(pl.program_id(0) > 0)(lambda: body(False))


def _merge(x2, g, w_in_t, osb, ofx, mq, mem, gmem, wkv, gk, wsb, wfx, wmem, wout):
    t = x2.shape[0]
    b, m, _ = mem.shape
    steps_per_batch = t // b // ROW_TILE
    row = lambda n: pl.BlockSpec((ROW_TILE, n), lambda i: (i, 0))
    hbm = pl.BlockSpec(memory_space=pl.ANY)
    bf16_copy = lambda w: pltpu.VMEM(w.shape, BF16)
    return pl.pallas_call(
        functools.partial(_merge_kernel, steps_per_batch),
        grid=(t // ROW_TILE,),
        in_specs=[row(D_MODEL), _const_spec((1, D_MODEL)), hbm,
                  row(D_SB), row(D_FOX), row(D_MEM),
                  pl.BlockSpec((1, m, D_MODEL), lambda i: (i // steps_per_batch, 0, 0)),
                  _const_spec((1, D_MODEL)), hbm, _const_spec((1, D_MEM)),
                  hbm, hbm, hbm, hbm],
        out_specs=row(D_MODEL),
        out_shape=jax.ShapeDtypeStruct((t, D_MODEL), F32),
        scratch_shapes=[pltpu.VMEM((m, D_MEM), BF16), pltpu.VMEM((m, D_MEM), BF16),
                        bf16_copy(wkv), pltpu.VMEM((N_BRANCH * D_MODEL, D_MODEL), BF16),
                        bf16_copy(wsb), bf16_copy(wfx), bf16_copy(wmem), bf16_copy(wout),
                        pltpu.VMEM((2, D_MODEL, D_MODEL), F32), pltpu.SemaphoreType.DMA((2,))],
        compiler_params=_params(1),
        name="merge",
    )(x2, g, w_in_t, osb, ofx, mq, mem, gmem, wkv, gk, wsb, wfx, wmem, wout)


def _mlp_kernel(x_ref, g_ref, wup_hbm, wdn_hbm, o_ref, wup_ref, wdn_ref, up_stage, dn_stage, sem):
    n_chunks = D_FF // FF_CHUNK

    def chunk_copies(c, slot):
        sl = pl.ds(c * FF_CHUNK, FF_CHUNK)
        return (pltpu.make_async_copy(wup_hbm.at[:, sl], up_stage.at[slot], sem.at[0, slot]),
                pltpu.make_async_copy(wdn_hbm.at[sl, :], dn_stage.at[slot], sem.at[1, slot]))

    def body(stream_weights):
        if stream_weights:
            for cp in chunk_copies(0, 0):
                cp.start()
        x = x_ref[...]
        h = _rms(x, g_ref[...]).astype(BF16)
        acc = x
        for c in range(n_chunks):
            sl = slice(c * FF_CHUNK, (c + 1) * FF_CHUNK)
            if stream_weights:
                slot = c % 2
                if c + 1 < n_chunks:
                    for cp in chunk_copies(c + 1, 1 - slot):
                        cp.start()
                for cp in chunk_copies(c, slot):
                    cp.wait()
                wup_ref[:, sl] = up_stage[slot].astype(BF16)
                wdn_ref[sl, :] = dn_stage[slot].astype(BF16)
            u = jnp.maximum(jnp.dot(h, wup_ref[:, sl], preferred_element_type=F32), 0.0)
            acc = acc + jnp.dot((u * u).astype(BF16), wdn_ref[sl, :], preferred_element_type=F32)
        o_ref[...] = acc

    pl.when(pl.program_id(0) == 0)(lambda: body(True))
    pl.when(pl.program_id(0) > 0)(lambda: body(False))


def _mlp(x2, g, wup, wdn):
    t = x2.shape[0]
    row = pl.BlockSpec((MLP_ROW_TILE, D_MODEL), lambda i: (i, 0))
    hbm = pl.BlockSpec(memory_space=pl.ANY)
    return pl.pallas_call(
        _mlp_kernel,
        grid=(t // MLP_ROW_TILE,),
        in_specs=[row, _const_spec((1, D_MODEL)), hbm, hbm],
        out_specs=row,
        out_shape=jax.ShapeDtypeStruct((t, D_MODEL), F32),
        scratch_shapes=[pltpu.VMEM((D_MODEL, D_FF), BF16), pltpu.VMEM((D_FF, D_MODEL), BF16),
                        pltpu.VMEM((2, D_MODEL, FF_CHUNK), F32), pltpu.VMEM((2, FF_CHUNK, D_MODEL), F32),
                        pltpu.SemaphoreType.DMA((2, 2))],
        compiler_params=_params(1),
        name="mlp",
    )(x2, g, wup, wdn)


def _layer(x, mem, g_mix, g_memn, w_in, b_forget, g_fq, g_fk, g_mq, g_mk, w_mem_kv,
           w_sb, w_fox, w_mem, w_out, g_mlp, w_up, w_dn):
    b, s, _ = x.shape
    t = b * s
    x2 = x.reshape(t, D_MODEL)
    n_qkv = 3 * D_SB + 3 * D_FOX
    w_in_t = w_in.T
    bf = jnp.pad(b_forget, (0, LANES - FOX_HEADS)).reshape(1, LANES)
    row = lambda a: a.reshape(1, -1)

    sbq, sbk, sbv, fxq, fxk, fxv, mq, aq, ak = _proj(
        x2, b, row(g_mix), w_in_t, bf, row(jnp.tile(g_fq, FOX_HEADS)),
        row(jnp.tile(g_fk, FOX_HEADS)), row(jnp.tile(g_mq, MEM_HEADS)))
    to3 = lambda a: a.reshape(b, s, a.shape[-1])

    o_sb = _sb_attention(to3(sbq), to3(sbk), to3(sbv))
    o_fox = _fox_attention(to3(fxq), to3(fxk), to3(fxv), to3(aq), to3(ak))
    x1 = _merge(x2, row(g_mix), w_in_t, o_sb.reshape(t, D_SB), o_fox.reshape(t, D_FOX), mq,
                mem, row(g_memn), w_mem_kv, row(jnp.tile(g_mk, MEM_HEADS)), w_sb, w_fox, w_mem, w_out)
    out = _mlp(x1, row(g_mlp), w_up, w_dn)
    return out.reshape(b, s, D_MODEL)


def kernel(x, mem, g_mix_norm, g_mem_norm, w_in, b_forget, g_fox_q, g_fox_k, g_mem_q, g_mem_k,
           w_mem_kv, w_branch_sb, w_branch_fox, w_branch_mem, w_out, g_mlp_norm, w_ff_up, w_ff_down):
    for l in range(w_in.shape[0]):
        x = _layer(x, mem, g_mix_norm[l], g_mem_norm[l], w_in[l], b_forget[l], g_fox_q[l], g_fox_k[l],
                   g_mem_q[l], g_mem_k[l], w_mem_kv[l], w_branch_sb[l], w_branch_fox[l],
                   w_branch_mem[l], w_out[l], g_mlp_norm[l], w_ff_up[l], w_ff_down[l])
    return x
```

```python
import functools

import numpy as np
import jax
import jax.numpy as jnp
from jax import lax
from jax.experimental import pallas as pl
from jax.experimental.pallas import tpu as pltpu

D_MODEL = 1024
HD = 64
SB_HEADS = 8
FOX_HEADS = 8
MEM_HEADS = 4
MEM_HD = 128
D_SB = SB_HEADS * HD
D_FOX = FOX_HEADS * HD
D_MEM = MEM_HEADS * MEM_HD
N_BRANCH = 3
D_FF = 4 * D_MODEL
EPS = 1e-6
NEG_INF = -1e30

LANES = 128
AUG_STRIDE = 8
ROW_TILE = 512
PROJ_ROW_TILE = 1024
MLP_ROW_TILE = 512
ATT_Q_TILE = 512
ATT_K_TILE = 256
ATT_PAIRS = 2
SB_EXIT = 110.0
FF_CHUNK = 512
VMEM_LIMIT = 56 * 1024 * 1024

F32 = jnp.float32
BF16 = jnp.bfloat16
_NT = (((1,), (1,)), ((), ()))
_LOG2E = 1.4426950408889634


def _const_spec(shape):
    nd = len(shape)
    return pl.BlockSpec(shape, lambda *_: (0,) * nd, pipeline_mode=pl.Buffered(1))


def _params(n_axes):
    return pltpu.CompilerParams(
        dimension_semantics=("arbitrary",) * n_axes, vmem_limit_bytes=VMEM_LIMIT)


def _rms(x, g):
    ms = jnp.sum(x * x, axis=-1, keepdims=True) * (1.0 / x.shape[-1])
    return (x * lax.rsqrt(ms + EPS)) * g


def _headnorm(p, g, hd):
    rows, n = p.shape
    lane = lax.broadcasted_iota(jnp.int32, (rows, LANES), 1)
    outs = []
    for j in range(n // LANES):
        c = p[:, j * LANES:(j + 1) * LANES]
        c2 = c * c
        if hd == LANES:
            ms = jnp.sum(c2, axis=1, keepdims=True) * (1.0 / LANES)
        else:
            lo = jnp.sum(jnp.where(lane < hd, c2, 0.0), axis=1, keepdims=True)
            hi = jnp.sum(jnp.where(lane >= hd, c2, 0.0), axis=1, keepdims=True)
            ms = jnp.where(lane < hd, lo, hi) * (1.0 / hd)
        outs.append((c * lax.rsqrt(ms + EPS)) * g)
    return jnp.concatenate(outs, axis=1)


def _log_sigmoid(x):
    return jnp.minimum(x, 0.0) - jnp.log(1.0 + jnp.exp(-jnp.abs(x)))


def _decay_columns(lf, carry_ref, p_ref, oq_ref, ok_ref):
    rows = lf.shape[0]
    row = lax.broadcasted_iota(jnp.int32, lf.shape, 0)
    lane = lax.broadcasted_iota(jnp.int32, lf.shape, 1)
    f = lf
    sh = 1
    while sh < rows:
        f = f + jnp.where(row >= sh, pltpu.roll(f, sh, 0), 0.0)
        sh *= 2
    f = f + carry_ref[...]
    carry_ref[...] = f[rows - 1:rows, :]
    f_hi = f.astype(BF16).astype(F32)
    r1 = f - f_hi
    f_mid = r1.astype(BF16).astype(F32)
    f_lo = (r1 - f_mid).astype(BF16).astype(F32)
    keep = lane < FOX_HEADS
    zero = jnp.zeros_like(f)
    pieces = (jnp.where(keep, f_hi, zero) + pltpu.roll(jnp.where(keep, f_mid, zero), FOX_HEADS, 1)
              + pltpu.roll(jnp.where(keep, f_lo, zero), 2 * FOX_HEADS, 1))
    placed = jnp.dot(pieces.astype(BF16), p_ref[...], preferred_element_type=F32)
    return ((placed[:, :LANES] + oq_ref[...]).astype(BF16), (placed[:, LANES:] + ok_ref[...]).astype(BF16))


def _decay_constants():
    p = np.zeros((LANES, 2 * LANES), np.float32)
    oq = np.zeros((1, LANES), np.float32)
    ok = np.zeros((1, LANES), np.float32)
    for h in range(FOX_HEADS):
        for j in range(3):
            p[j * FOX_HEADS + h, h * AUG_STRIDE + j] = 1.0
            p[j * FOX_HEADS + h, LANES + h * AUG_STRIDE + 3 + j] = -1.0
            oq[0, h * AUG_STRIDE + 3 + j] = 1.0
            ok[0, h * AUG_STRIDE + j] = 1.0
    return jnp.asarray(p, BF16), jnp.asarray(oq), jnp.asarray(ok)


def _proj_kernel(steps_per_batch, x_ref, g_ref, wt_ref, bf_ref, gq_ref, gk_ref, gm_ref, p_ref, oq_ref, ok_ref,
                 sbq_ref, sbk_ref, sbv_ref, fxq_ref, fxk_ref, fxv_ref, mq_ref, aq_ref, ak_ref,
                 carry_ref, wbf_ref):
    @pl.when(pl.program_id(0) == 0)
    def _():
        wbf_ref[...] = wt_ref[...].astype(BF16)

    @pl.when(pl.program_id(0) % steps_per_batch == 0)
    def _():
        carry_ref[...] = jnp.zeros_like(carry_ref)

    h = _rms(x_ref[...], g_ref[...]).astype(BF16)

    def mm(c0, n):
        return lax.dot_general(h, wbf_ref[c0:c0 + n, :], _NT, preferred_element_type=F32)

    c0 = 3 * D_SB
    c_f = c0 + 3 * D_FOX
    log_f = _log_sigmoid(mm(c_f, LANES) + bf_ref[...])
    mq_ref[...] = _headnorm(mm(c_f + FOX_HEADS, D_MEM), gm_ref[...], MEM_HD).astype(BF16)
    fxq_ref[...] = (_headnorm(mm(c0, D_FOX), gq_ref[...], HD) * HD ** -0.5).astype(BF16)
    fxk_ref[...] = _headnorm(mm(c0 + D_FOX, D_FOX), gk_ref[...], HD).astype(BF16)
    aq_ref[...], ak_ref[...] = _decay_columns(log_f, carry_ref, p_ref, oq_ref, ok_ref)
    fxv_ref[...] = mm(c0 + 2 * D_FOX, D_FOX).astype(BF16)
    sbq_ref[...] = (mm(0, D_SB) * HD ** -0.5).astype(BF16)
    sbk_ref[...] = mm(D_SB, D_SB).astype(BF16)
    sbv_ref[...] = mm(2 * D_SB, D_SB).astype(BF16)


def _proj(x2, batch, g, w_in_t, bf, gq, gk, gm):
    t = x2.shape[0]
    n_rows = 3 * D_SB + 3 * D_FOX + FOX_HEADS + D_MEM
    p, oq, ok = _decay_constants()
    row = lambda n: pl.BlockSpec((PROJ_ROW_TILE, n), lambda i: (i, 0))
    out_shapes = [jax.ShapeDtypeStruct((t, D_SB), BF16)] * 7 + [jax.ShapeDtypeStruct((t, LANES), BF16)] * 2
    return pl.pallas_call(
        functools.partial(_proj_kernel, t // batch // PROJ_ROW_TILE),
        grid=(t // PROJ_ROW_TILE,),
        in_specs=[row(D_MODEL), _const_spec((1, D_MODEL)), _const_spec((n_rows, D_MODEL)),
                  _const_spec((1, LANES)), _const_spec((1, LANES)), _const_spec((1, LANES)),
                  _const_spec((1, LANES)), _const_spec(p.shape), _const_spec(oq.shape), _const_spec(ok.shape)],
        out_specs=[row(D_SB)] * 7 + [row(LANES)] * 2,
        out_shape=out_shapes,
        scratch_shapes=[pltpu.VMEM((1, LANES), F32), pltpu.VMEM((n_rows, D_MODEL), BF16)],
        compiler_params=_params(1),
        name="proj",
    )(x2, g, w_in_t, bf, gq, gk, gm, p, oq, ok)


def _stack_heads(x, lane):
    zero = jnp.zeros_like(x)
    return jnp.concatenate([jnp.where(lane < HD, x, zero), jnp.where(lane >= HD, x, zero)], axis=0)


def _unstack_heads(x, lane):
    rows = x.shape[0] // 2
    return jnp.where(lane < HD, x[:rows], x[rows:])


def _softplus(z):
    return jnp.maximum(z, 0.0) + jnp.log(1.0 + jnp.exp2(jnp.abs(z) * (-_LOG2E)))


def _sb_kernel(q_ref, k_ref, v_ref, m_ref, o_ref, flag_ref, acc_ref, carry_ref):
    s_len = q_ref.shape[1]
    t = ATT_K_TILE
    nb = s_len // t
    lane = lax.broadcasted_iota(jnp.int32, (t, LANES), 1)
    tri = (lax.broadcasted_iota(jnp.int32, (2 * t, t), 1)
           < (lax.broadcasted_iota(jnp.int32, (2 * t, t), 0) & (t - 1)))
    msum = m_ref[...]
    reps = t // LANES
    order = [(pr, j) for pr in range(ATT_PAIRS) for j in reversed(range(nb))]
    z, sp, c, acc_diag, carry_diag = {}, {}, {}, {}, {}

    def qblock(pr, j):
        return _stack_heads(q_ref[0, j * t:(j + 1) * t, pr * LANES:(pr + 1) * LANES], lane)

    def scores(g):
        pr, j = order[g]
        lhs = qblock(pr, j) if j == nb - 1 else jnp.concatenate([qblock(pr, j), qblock(pr, j + 1)], axis=0)
        z[g] = lax.dot_general(lhs, k_ref[0, j * t:(j + 1) * t, pr * LANES:(pr + 1) * LANES], _NT,
                               preferred_element_type=F32)

    def suffix_sums(g):
        s = _softplus(z[g])
        top = jnp.where(tri, s[:2 * t], 0.0)
        s = top if order[g][1] == nb - 1 else jnp.concatenate([top, s[2 * t:]], axis=0)
        sp[g] = s
        z[g] = z[g] - s
        c[g] = jnp.dot(s.astype(BF16), msum, preferred_element_type=F32)

    def weights_pv(g):
        pr, j = order[g]
        ls = slice(pr * LANES, (pr + 1) * LANES)
        zc = z.pop(g) - c.pop(g)
        w = jnp.where(tri, jnp.exp(zc[:2 * t]), 0.0)
        if j < nb - 1:
            w_prev = jnp.exp(zc[2 * t:] - jnp.concatenate([carry_diag[(pr, j + 1)]] * reps, axis=1))
            w = jnp.concatenate([w, w_prev], axis=0)
        pv = jnp.dot(w.astype(BF16), v_ref[0, j * t:(j + 1) * t, ls], preferred_element_type=F32)
        rs = jnp.sum(sp.pop(g), axis=1, keepdims=True)
        if j < nb - 1:
            done = acc_diag.pop((pr, j + 1)) + pv[2 * t:]
            o_ref[0, (j + 1) * t:(j + 2) * t, ls] = _unstack_heads(done, lane).astype(BF16)
            total = carry_diag.pop((pr, j + 1)) + rs[2 * t:]
            if j >= 1:
                flag_ref[pr * nb + j + 1] = jnp.min(total)
        if j == 0:
            o_ref[0, 0:t, ls] = _unstack_heads(pv[:2 * t], lane).astype(BF16)
        else:
            acc_diag[(pr, j)] = pv[:2 * t]
            carry_diag[(pr, j)] = jnp.broadcast_to(rs[:2 * t], (2 * t, LANES))

    n_groups = len(order)
    for step in range(n_groups + 2):
        if step < n_groups:
            scores(step)
        if 1 <= step <= n_groups:
            suffix_sums(step - 1)
        if step >= 2:
            weights_pv(step - 2)

    def recompute_block(pr, jb):
        ls = slice(pr * LANES, (pr + 1) * LANES)
        rows = pl.ds(pl.multiple_of(jb * t, t), t)
        qb = _stack_heads(q_ref[0, rows, ls], lane)
        acc_ref[...] = jnp.zeros_like(acc_ref)
        carry_ref[...] = jnp.zeros_like(carry_ref)

        def tile(kb, diag):
            keys = pl.ds(pl.multiple_of(kb * t, t), t)
            zz = lax.dot_general(qb, k_ref[0, keys, ls], _NT, preferred_element_type=F32)
            s = _softplus(zz)
            if diag:
                s = jnp.where(tri, s, 0.0)
            cc = jnp.dot(s.astype(BF16), msum, preferred_element_type=F32)
            w = jnp.exp(zz - s - cc - jnp.concatenate([carry_ref[...]] * reps, axis=1))
            if diag:
                w = jnp.where(tri, w, 0.0)
            acc_ref[...] += jnp.dot(w.astype(BF16), v_ref[0, keys, ls], preferred_element_type=F32)
            carry_ref[...] += jnp.sum(s, axis=1, keepdims=True)

        tile(jb, True)

        def body(i, _):
            tile(jb - 1 - i, False)
            return 0

        lax.fori_loop(0, jb, body, 0)
        o_ref[0, rows, ls] = _unstack_heads(acc_ref[...], lane).astype(BF16)

    for pr in range(ATT_PAIRS):
        def redo(jb, _, pr=pr):
            pl.when(flag_ref[pr * nb + jb] < SB_EXIT)(lambda: recompute_block(pr, jb))
            return 0

        lax.fori_loop(2, nb, redo, 0)


def _sb_attention(q, k, v):
    b, s, d = q.shape
    t = ATT_K_TILE
    width = ATT_PAIRS * LANES
    ii = np.arange(t)
    msum = jnp.asarray(ii[:, None] > ii[None, :], BF16)
    spec = pl.BlockSpec((1, s, width), lambda bi, g: (bi, 0, g))
    return pl.pallas_call(
        _sb_kernel,
        grid=(b, d // width),
        in_specs=[spec, spec, spec, _const_spec((t, t))],
        out_specs=spec,
        out_shape=jax.ShapeDtypeStruct((b, s, d), BF16),
        scratch_shapes=[pltpu.SMEM((ATT_PAIRS * (s // t),), F32), pltpu.VMEM((2 * t, LANES), F32),
                        pltpu.VMEM((2 * t, LANES), F32)],
        compiler_params=_params(2),
        name="sb_attn",
    )(q, k, v, msum)


def _fox_kernel(q_ref, aq_ref, k_ref, ak_ref, v_ref, o_ref):
    s_len = q_ref.shape[1]
    t = ATT_Q_TILE
    hb = t // 2
    lane = lax.broadcasted_iota(jnp.int32, (hb, LANES), 1)
    group = lane // AUG_STRIDE
    tri = (lax.broadcasted_iota(jnp.int32, (t, hb), 1)
           <= (lax.broadcasted_iota(jnp.int32, (t, hb), 0) & (hb - 1)))
    ones = jnp.ones((t, LANES), BF16)
    tiles = []
    for pr in range(ATT_PAIRS):
        for qi in range(s_len // t):
            tiles += [(pr, qi, "full", kb * t, t) for kb in range(qi)]
            tiles += [(pr, qi, "diag_all", qi * t, hb), (pr, qi, "diag_b", qi * t + hb, hb)]
    qext, kvext, z, m, acc = {}, {}, {}, {}, {}

    def stacked_q(pr, qi):
        ls = slice(pr * LANES, (pr + 1) * LANES)
        pair = ATT_PAIRS * pl.program_id(1) + pr
        parts = []
        for half in range(2):
            rows = slice(qi * t + half * hb, qi * t + (half + 1) * hb)
            aq = aq_ref[0, rows, :]
            zero = jnp.zeros_like(aq)
            aug = jnp.concatenate([jnp.where(group == 2 * pair, aq, zero),
                                   jnp.where(group == 2 * pair + 1, aq, zero)], axis=0)
            parts.append(jnp.concatenate([_stack_heads(q_ref[0, rows, ls], lane), aug], axis=1))
        return jnp.concatenate(parts, axis=0)

    def keys_values(pr, start, size):
        if (pr, start, size) not in kvext:
            ls = slice(pr * LANES, (pr + 1) * LANES)
            keys = slice(start, start + size)
            kext = jnp.concatenate([k_ref[0, keys, ls], ak_ref[0, keys, :]], axis=1)
            vext = jnp.concatenate([v_ref[0, keys, ls], ones[:size]], axis=1)
            kvext[(pr, start, size)] = (kext, vext)
        return kvext[(pr, start, size)]

    def scores(n):
        pr, qi, kind, start, size = tiles[n]
        if (pr, qi) not in qext:
            qext[(pr, qi)] = stacked_q(pr, qi)
        lhs = qext[(pr, qi)]
        kext = keys_values(pr, start, size)[0]
        if kind == "diag_b":
            s = lax.dot_general(lhs[t:], kext, _NT, preferred_element_type=F32)
            s = jnp.where(tri, s, NEG_INF)
        else:
            s = lax.dot_general(lhs, kext, _NT, preferred_element_type=F32)
            if kind == "diag_all":
                s = jnp.concatenate([jnp.where(tri, s[:t], NEG_INF), s[t:]], axis=0)
        z[n] = s

    def softmax_pv(n):
        pr, qi, kind, start, size = tiles[n]
        key = (pr, qi)
        s = z.pop(n)
        vext = keys_values(pr, start, size)[1]
        row_max = jnp.max(s, axis=1, keepdims=True)
        if key not in m:
            m_new = jnp.broadcast_to(row_max, (s.shape[0], LANES))
        else:
            m_old = m[key][t:] if kind == "diag_b" else m[key]
            m_new = jnp.maximum(m_old, row_max)
        p = jnp.exp(s - jnp.concatenate([m_new] * (size // LANES), axis=1))
        pv = jnp.dot(p.astype(BF16), vext, preferred_element_type=F32)
        if key not in m:
            acc[key], m[key] = pv, m_new
        else:
            acc_old = acc[key][t:] if kind == "diag_b" else acc[key]
            alpha = jnp.exp(m_old - m_new)
            acc_new = jnp.concatenate([alpha, alpha], axis=1) * acc_old + pv
            if kind == "diag_b":
                acc[key] = jnp.concatenate([acc[key][:t], acc_new], axis=0)
                m[key] = jnp.concatenate([m[key][:t], m_new], axis=0)
            else:
                acc[key], m[key] = acc_new, m_new
        if kind == "diag_b":
            a = acc.pop(key)
            o = a[:, :LANES] / a[:, LANES:]
            out = jnp.concatenate([_unstack_heads(o[:t], lane), _unstack_heads(o[t:], lane)], axis=0)
            o_ref[0, qi * t:(qi + 1) * t, pr * LANES:(pr + 1) * LANES] = out.astype(BF16)
            m.pop(key)
            qext.pop(key)

    n_tiles = len(tiles)
    for step in range(n_tiles + 1):
        if step < n_tiles:
            scores(step)
        if step >= 1:
            softmax_pv(step - 1)


def _fox_attention(q, k, v, aq, ak):
    b, s, d = q.shape
    width = ATT_PAIRS * LANES
    spec = pl.BlockSpec((1, s, width), lambda bi, g: (bi, 0, g))
    aspec = pl.BlockSpec((1, s, LANES), lambda bi, g: (bi, 0, 0))
    return pl.pallas_call(
        _fox_kernel,
        grid=(b, d // width),
        in_specs=[spec, aspec, spec, aspec, spec],
        out_specs=spec,
        out_shape=jax.ShapeDtypeStruct((b, s, d), BF16),
        compiler_params=_params(2),
        name="fox_attn",
    )(q, aq, k, ak, v)


def _memory_attention(q_ref, mk_ref, mv_ref):
    z, out = {}, []

    def scores(h):
        sl = slice(h * MEM_HD, (h + 1) * MEM_HD)
        z[h] = lax.dot_general(q_ref[:, sl], mk_ref[:, sl], _NT,
                               preferred_element_type=F32) * MEM_HD ** -0.5

    def softmax_pv(h):
        sl = slice(h * MEM_HD, (h + 1) * MEM_HD)
        zh = z.pop(h)
        e = jnp.exp(zh - jnp.max(zh, axis=1, keepdims=True))
        p = e / jnp.sum(e, axis=1, keepdims=True)
        out.append(jnp.dot(p.astype(BF16), mv_ref[:, sl], preferred_element_type=F32).astype(BF16))

    for step in range(MEM_HEADS + 1):
        if step < MEM_HEADS:
            scores(step)
        if step >= 1:
            softmax_pv(step - 1)
    return jnp.concatenate(out, axis=1)


def _merge_kernel(steps_per_batch, x_ref, g_ref, wint_hbm, osb_ref, ofx_ref, mq_ref,
                  mem_ref, gmem_ref, wkv_hbm, gk_ref, wsb_hbm, wfx_hbm, wmem_hbm, wout_hbm, o_ref,
                  mk_ref, mv_ref, wkv_ref, wgt_ref, wsb_ref, wfx_ref, wmem_ref, wout_ref, stage, sem):
    gate0 = wint_hbm.shape[0] - N_BRANCH * D_MODEL
    weights = [(wkv_hbm, wkv_ref)]
    for i, (src, dst) in enumerate(((wsb_hbm, wsb_ref), (wfx_hbm, wfx_ref), (wmem_hbm, wmem_ref))):
        rows = pl.ds(i * D_MODEL, D_MODEL)
        weights += [(wint_hbm.at[pl.ds(gate0 + i * D_MODEL, D_MODEL), :], wgt_ref.at[rows, :]), (src, dst)]
    weights.append((wout_hbm, wout_ref))

    def copy(k, slot):
        src = weights[k][0]
        return pltpu.make_async_copy(src, stage.at[slot, pl.ds(0, src.shape[0]), :], sem.at[slot])

    def body(stream_weights):
        fetched = [0]

        def fetch():
            if not stream_weights:
                return
            k = fetched[0]
            fetched[0] += 1
            slot = k % 2
            if k + 1 < len(weights):
                copy(k + 1, 1 - slot).start()
            copy(k, slot).wait()
            src, dst = weights[k]
            dst[...] = stage[slot, :src.shape[0], :].astype(BF16)

        def memory_kv():
            mh = _rms(mem_ref[0], gmem_ref[...]).astype(BF16)
            kv = jnp.dot(mh, wkv_ref[...], preferred_element_type=F32)
            mk_ref[...] = _headnorm(kv[:, :D_MEM], gk_ref[...], MEM_HD).astype(BF16)
            mv_ref[...] = kv[:, D_MEM:].astype(BF16)

        if stream_weights:
            copy(0, 0).start()
            fetch()
            memory_kv()
        else:
            pl.when(pl.program_id(0) % steps_per_batch == 0)(memory_kv)

        x = x_ref[...]
        h = _rms(x, g_ref[...]).astype(BF16)
        branches = ((osb_ref, wsb_ref), (ofx_ref, wfx_ref), (None, wmem_ref))
        merged = None
        for i, (o_br, w_br) in enumerate(branches):
            fetch()
            gate = jax.nn.sigmoid(lax.dot_general(h, wgt_ref[i * D_MODEL:(i + 1) * D_MODEL, :], _NT,
                                                  preferred_element_type=F32))
            o = _memory_attention(mq_ref, mk_ref, mv_ref) if o_br is None else o_br[...]
            fetch()
            term = gate * jnp.dot(o, w_br[...], preferred_element_type=F32)
            merged = term if merged is None else merged + term
        fetch()
        o_ref[...] = x + jnp.dot(merged.astype(BF16), wout_ref[...], preferred_element_type=F32)

    pl.when(pl.program_id(0) == 0)(lambda: body(True))
    pl.when(pl.program_id(0) > 0)(lambda: body(False))


def _merge(x2, g, w_in_t, osb, ofx, mq, mem, gmem, wkv, gk, wsb, wfx, wmem, wout):
    t = x2.shape[0]
    b, m, _ = mem.shape
    steps_per_batch = t // b // ROW_TILE
    row = lambda n: pl.BlockSpec((ROW_TILE, n), lambda i: (i, 0))
    hbm = pl.BlockSpec(memory_space=pl.ANY)
    bf16_copy = lambda w: pltpu.VMEM(w.shape, BF16)
    return pl.pallas_call(
        functools.partial(_merge_kernel, steps_per_batch),
        grid=(t // ROW_TILE,),
        in_specs=[row(D_MODEL), _const_spec((1, D_MODEL)), hbm,
                  row(D_SB), row(D_FOX), row(D_MEM),
                  pl.BlockSpec((1, m, D_MODEL), lambda i: (i // steps_per_batch, 0, 0)),
                  _const_spec((1, D_MODEL)), hbm, _const_spec((1, LANES)),
                  hbm, hbm, hbm, hbm],
        out_specs=row(D_MODEL),
        out_shape=jax.ShapeDtypeStruct((t, D_MODEL), F32),
        scratch_shapes=[pltpu.VMEM((m, D_MEM), BF16), pltpu.VMEM((m, D_MEM), BF16),
                        bf16_copy(wkv), pltpu.VMEM((N_BRANCH * D_MODEL, D_MODEL), BF16),
                        bf16_copy(wsb), bf16_copy(wfx), bf16_copy(wmem), bf16_copy(wout),
                        pltpu.VMEM((2, D_MODEL, D_MODEL), F32), pltpu.SemaphoreType.DMA((2,))],
        compiler_params=_params(1),
        name="merge",
    )(x2, g, w_in_t, osb, ofx, mq, mem, gmem, wkv, gk, wsb, wfx, wmem, wout)


def _mlp_kernel(x_ref, g_ref, wup_hbm, wdn_hbm, o_ref, wup_ref, wdn_ref, up_stage, dn_stage, sem):
    n_chunks = D_FF // FF_CHUNK

    def chunk_copies(c, slot):
        sl = pl.ds(c * FF_CHUNK, FF_CHUNK)
        return (pltpu.make_async_copy(wup_hbm.at[:, sl], up_stage.at[slot], sem.at[0, slot]),
                pltpu.make_async_copy(wdn_hbm.at[sl, :], dn_stage.at[slot], sem.at[1, slot]))

    def body(stream_weights):
        if stream_weights:
            for cp in chunk_copies(0, 0):
                cp.start()
        x = x_ref[...]
        h = _rms(x, g_ref[...]).astype(BF16)
        acc = x
        for c in range(n_chunks):
            sl = slice(c * FF_CHUNK, (c + 1) * FF_CHUNK)
            if stream_weights:
                slot = c % 2
                if c + 1 < n_chunks:
                    for cp in chunk_copies(c + 1, 1 - slot):
                        cp.start()
                for cp in chunk_copies(c, slot):
                    cp.wait()
                wup_ref[:, sl] = up_stage[slot].astype(BF16)
                wdn_ref[sl, :] = dn_stage[slot].astype(BF16)
            u = jnp.maximum(jnp.dot(h, wup_ref[:, sl], preferred_element_type=F32), 0.0)
            acc = acc + jnp.dot((u * u).astype(BF16), wdn_ref[sl, :], preferred_element_type=F32)
        o_ref[...] = acc

    pl.when(pl.program_id(0) == 0)(lambda: body(True))
    pl.when(pl.program_id(0) > 0)(lambda: body(False))


def _mlp(x2, g, wup, wdn):
    t = x2.shape[0]
    row = pl.BlockSpec((MLP_ROW_TILE, D_MODEL), lambda i: (i, 0))
    hbm = pl.BlockSpec(memory_space=pl.ANY)
    return pl.pallas_call(
        _mlp_kernel,
        grid=(t // MLP_ROW_TILE,),
        in_specs=[row, _const_spec((1, D_MODEL)), hbm, hbm],
        out_specs=row,
        out_shape=jax.ShapeDtypeStruct((t, D_MODEL), F32),
        scratch_shapes=[pltpu.VMEM((D_MODEL, D_FF), BF16), pltpu.VMEM((D_FF, D_MODEL), BF16),
                        pltpu.VMEM((2, D_MODEL, FF_CHUNK), F32), pltpu.VMEM((2, FF_CHUNK, D_MODEL), F32),
                        pltpu.SemaphoreType.DMA((2, 2))],
        compiler_params=_params(1),
        name="mlp",
    )(x2, g, wup, wdn)


def _layer(x, mem, g_mix, g_memn, w_in, b_forget, g_fq, g_fk, g_mq, g_mk, w_mem_kv,
           w_sb, w_fox, w_mem, w_out, g_mlp, w_up, w_dn):
    b, s, _ = x.shape
    t = b * s
    x2 = x.reshape(t, D_MODEL)
    w_in_t = w_in.T
    bf = jnp.pad(b_forget, (0, LANES - FOX_HEADS)).reshape(1, LANES)
    row = lambda a: a.reshape(1, -1)

    sbq, sbk, sbv, fxq, fxk, fxv, mq, aq, ak = _proj(
        x2, b, row(g_mix), w_in_t, bf, row(jnp.tile(g_fq, LANES // HD)),
        row(jnp.tile(g_fk, LANES // HD)), row(g_mq))
    to3 = lambda a: a.reshape(b, s, a.shape[-1])

    o_sb = _sb_attention(to3(sbq), to3(sbk), to3(sbv))
    o_fox = _fox_attention(to3(fxq), to3(fxk), to3(fxv), to3(aq), to3(ak))
    x1 = _merge(x2, row(g_mix), w_in_t, o_sb.reshape(t, D_SB), o_fox.reshape(t, D_FOX), mq,
                mem, row(g_memn), w_mem_kv, row(g_mk), w_sb, w_fox, w_mem, w_out)
    out = _mlp(x1, row(g_mlp), w_up, w_dn)
    return out.reshape(b, s, D_MODEL)


def kernel(x, mem, g_mix_norm, g_mem_norm, w_in, b_forget, g_fox_q, g_fox_k, g_mem_q, g_mem_k,
           w_mem_kv, w_branch_sb, w_branch_fox, w_branch_mem, w_out, g_mlp_norm, w_ff_up, w_ff_down):
    for l in range(w_in.shape[0]):
        x = _layer(x, mem, g_mix_norm[l], g_mem_norm[l], w_in[l], b_forget[l], g_fox_q[l], g_fox_k[l],
                   g_mem_q[l], g_mem_k[l], w_mem_kv[l], w_branch_sb[l], w_branch_fox[l],
                   w_branch_mem[l], w_out[l], g_mlp_norm[l], w_ff_up[l], w_ff_down[l])
    return x
```

```python
import functools

import numpy as np
import jax
import jax.numpy as jnp
from jax import lax
from jax.experimental import pallas as pl
from jax.experimental.pallas import tpu as pltpu

D_MODEL = 1024
HD = 64
SB_HEADS = 8
FOX_HEADS = 8
MEM_HEADS = 4
MEM_HD = 128
D_SB = SB_HEADS * HD
D_FOX = FOX_HEADS * HD
D_MEM = MEM_HEADS * MEM_HD
N_BRANCH = 3
D_FF = 4 * D_MODEL
EPS = 1e-6
NEG_INF = -1e30

LANES = 128
AUG_STRIDE = 8
ROW_TILE = 512
PROJ_ROW_TILE = 1024
MLP_ROW_TILE = 1024
ATT_Q_TILE = 512
ATT_K_TILE = 256
ATT_PAIRS = 2
SB_EXIT = 110.0
FF_CHUNK = 512
VMEM_LIMIT = 56 * 1024 * 1024

F32 = jnp.float32
BF16 = jnp.bfloat16
_NT = (((1,), (1,)), ((), ()))
_LOG2E = 1.4426950408889634


def _const_spec(shape):
    nd = len(shape)
    return pl.BlockSpec(shape, lambda *_: (0,) * nd, pipeline_mode=pl.Buffered(1))


def _params(n_axes):
    return pltpu.CompilerParams(
        dimension_semantics=("arbitrary",) * n_axes, vmem_limit_bytes=VMEM_LIMIT)


def _rms(x, g):
    ms = jnp.sum(x * x, axis=-1, keepdims=True) * (1.0 / x.shape[-1])
    return (x * lax.rsqrt(ms + EPS)) * g


def _headnorm(p, g, hd):
    rows, n = p.shape
    lane = lax.broadcasted_iota(jnp.int32, (rows, LANES), 1)
    outs = []
    for j in range(n // LANES):
        c = p[:, j * LANES:(j + 1) * LANES]
        c2 = c * c
        if hd == LANES:
            ms = jnp.sum(c2, axis=1, keepdims=True) * (1.0 / LANES)
        else:
            lo = jnp.sum(jnp.where(lane < hd, c2, 0.0), axis=1, keepdims=True)
            hi = jnp.sum(jnp.where(lane >= hd, c2, 0.0), axis=1, keepdims=True)
            ms = jnp.where(lane < hd, lo, hi) * (1.0 / hd)
        outs.append((c * lax.rsqrt(ms + EPS)) * g)
    return jnp.concatenate(outs, axis=1)


def _log_sigmoid(x):
    return jnp.minimum(x, 0.0) - jnp.log(1.0 + jnp.exp(-jnp.abs(x)))


def _decay_columns(lf, carry_ref, p_ref, oq_ref, ok_ref):
    rows = lf.shape[0]
    row = lax.broadcasted_iota(jnp.int32, lf.shape, 0)
    lane = lax.broadcasted_iota(jnp.int32, lf.shape, 1)
    f = lf
    sh = 1
    while sh < rows:
        f = f + jnp.where(row >= sh, pltpu.roll(f, sh, 0), 0.0)
        sh *= 2
    f = f + carry_ref[...]
    carry_ref[...] = f[rows - 1:rows, :]
    f_hi = f.astype(BF16).astype(F32)
    r1 = f - f_hi
    f_mid = r1.astype(BF16).astype(F32)
    f_lo = (r1 - f_mid).astype(BF16).astype(F32)
    keep = lane < FOX_HEADS
    zero = jnp.zeros_like(f)
    pieces = (jnp.where(keep, f_hi, zero) + pltpu.roll(jnp.where(keep, f_mid, zero), FOX_HEADS, 1)
              + pltpu.roll(jnp.where(keep, f_lo, zero), 2 * FOX_HEADS, 1))
    placed = jnp.dot(pieces.astype(BF16), p_ref[...], preferred_element_type=F32)
    return ((placed[:, :LANES] + oq_ref[...]).astype(BF16), (placed[:, LANES:] + ok_ref[...]).astype(BF16))


def _decay_constants():
    p = np.zeros((LANES, 2 * LANES), np.float32)
    oq = np.zeros((1, LANES), np.float32)
    ok = np.zeros((1, LANES), np.float32)
    for h in range(FOX_HEADS):
        for j in range(3):
            p[j * FOX_HEADS + h, h * AUG_STRIDE + j] = 1.0
            p[j * FOX_HEADS + h, LANES + h * AUG_STRIDE + 3 + j] = -1.0
            oq[0, h * AUG_STRIDE + 3 + j] = 1.0
            ok[0, h * AUG_STRIDE + j] = 1.0
    return jnp.asarray(p, BF16), jnp.asarray(oq), jnp.asarray(ok)


def _proj_kernel(steps_per_batch, x_ref, g_ref, wt_ref, bf_ref, gq_ref, gk_ref, gm_ref, p_ref, oq_ref, ok_ref,
                 sbq_ref, sbk_ref, sbv_ref, fxq_ref, fxk_ref, fxv_ref, mq_ref, aq_ref, ak_ref,
                 carry_ref, wbf_ref):
    @pl.when(pl.program_id(0) == 0)
    def _():
        wbf_ref[...] = wt_ref[...].astype(BF16)

    @pl.when(pl.program_id(0) % steps_per_batch == 0)
    def _():
        carry_ref[...] = jnp.zeros_like(carry_ref)

    h = _rms(x_ref[...], g_ref[...]).astype(BF16)

    def mm(c0, n):
        return lax.dot_general(h, wbf_ref[c0:c0 + n, :], _NT, preferred_element_type=F32)

    c0 = 3 * D_SB
    c_f = c0 + 3 * D_FOX
    log_f = _log_sigmoid(mm(c_f, LANES) + bf_ref[...])
    mq_ref[...] = _headnorm(mm(c_f + FOX_HEADS, D_MEM), gm_ref[...], MEM_HD).astype(BF16)
    fxq_ref[...] = (_headnorm(mm(c0, D_FOX), gq_ref[...], HD) * HD ** -0.5).astype(BF16)
    fxk_ref[...] = _headnorm(mm(c0 + D_FOX, D_FOX), gk_ref[...], HD).astype(BF16)
    aq_ref[...], ak_ref[...] = _decay_columns(log_f, carry_ref, p_ref, oq_ref, ok_ref)
    fxv_ref[...] = mm(c0 + 2 * D_FOX, D_FOX).astype(BF16)
    sbq_ref[...] = (mm(0, D_SB) * HD ** -0.5).astype(BF16)
    sbk_ref[...] = mm(D_SB, D_SB).astype(BF16)
    sbv_ref[...] = mm(2 * D_SB, D_SB).astype(BF16)


def _proj(x2, batch, g, w_in_t, bf, gq, gk, gm):
    t = x2.shape[0]
    n_rows = 3 * D_SB + 3 * D_FOX + FOX_HEADS + D_MEM
    p, oq, ok = _decay_constants()
    row = lambda n: pl.BlockSpec((PROJ_ROW_TILE, n), lambda i: (i, 0))
    out_shapes = [jax.ShapeDtypeStruct((t, D_SB), BF16)] * 7 + [jax.ShapeDtypeStruct((t, LANES), BF16)] * 2
    return pl.pallas_call(
        functools.partial(_proj_kernel, t // batch // PROJ_ROW_TILE),
        grid=(t // PROJ_ROW_TILE,),
        in_specs=[row(D_MODEL), _const_spec((1, D_MODEL)), _const_spec((n_rows, D_MODEL)),
                  _const_spec((1, LANES)), _const_spec((1, LANES)), _const_spec((1, LANES)),
                  _const_spec((1, LANES)), _const_spec(p.shape), _const_spec(oq.shape), _const_spec(ok.shape)],
        out_specs=[row(D_SB)] * 7 + [row(LANES)] * 2,
        out_shape=out_shapes,
        scratch_shapes=[pltpu.VMEM((1, LANES), F32), pltpu.VMEM((n_rows, D_MODEL), BF16)],
        compiler_params=_params(1),
        name="proj",
    )(x2, g, w_in_t, bf, gq, gk, gm, p, oq, ok)


def _stack_heads(x, lane):
    zero = jnp.zeros_like(x)
    return jnp.concatenate([jnp.where(lane < HD, x, zero), jnp.where(lane >= HD, x, zero)], axis=0)


def _unstack_heads(x, lane):
    rows = x.shape[0] // 2
    return jnp.where(lane < HD, x[:rows], x[rows:])


def _softplus(z):
    return jnp.maximum(z, 0.0) + jnp.log(1.0 + jnp.exp2(jnp.abs(z) * (-_LOG2E)))


def _sb_kernel(q_ref, k_ref, v_ref, m_ref, o_ref, flag_ref, acc_ref, carry_ref):
    s_len = q_ref.shape[1]
    t = ATT_K_TILE
    nb = s_len // t
    lane = lax.broadcasted_iota(jnp.int32, (t, LANES), 1)
    tri = (lax.broadcasted_iota(jnp.int32, (2 * t, t), 1)
           < (lax.broadcasted_iota(jnp.int32, (2 * t, t), 0) & (t - 1)))
    msum = m_ref[...]
    reps = t // LANES
    order = [(pr, j) for pr in range(ATT_PAIRS) for j in reversed(range(nb))]
    z, sp, c, acc_diag, carry_diag = {}, {}, {}, {}, {}

    def qblock(pr, j):
        return _stack_heads(q_ref[0, j * t:(j + 1) * t, pr * LANES:(pr + 1) * LANES], lane)

    def scores(g):
        pr, j = order[g]
        lhs = qblock(pr, j) if j == nb - 1 else jnp.concatenate([qblock(pr, j), qblock(pr, j + 1)], axis=0)
        z[g] = lax.dot_general(lhs, k_ref[0, j * t:(j + 1) * t, pr * LANES:(pr + 1) * LANES], _NT,
                               preferred_element_type=F32)

    def suffix_sums(g):
        s = _softplus(z[g])
        top = jnp.where(tri, s[:2 * t], 0.0)
        s = top if order[g][1] == nb - 1 else jnp.concatenate([top, s[2 * t:]], axis=0)
        sp[g] = s
        z[g] = z[g] - s
        c[g] = jnp.dot(s.astype(BF16), msum, preferred_element_type=F32)

    def weights_pv(g):
        pr, j = order[g]
        ls = slice(pr * LANES, (pr + 1) * LANES)
        zc = z.pop(g) - c.pop(g)
        w = jnp.where(tri, jnp.exp(zc[:2 * t]), 0.0)
        if j < nb - 1:
            w_prev = jnp.exp(zc[2 * t:] - jnp.concatenate([carry_diag[(pr, j + 1)]] * reps, axis=1))
            w = jnp.concatenate([w, w_prev], axis=0)
        pv = jnp.dot(w.astype(BF16), v_ref[0, j * t:(j + 1) * t, ls], preferred_element_type=F32)
        rs = jnp.sum(sp.pop(g), axis=1, keepdims=True)
        if j < nb - 1:
            done = acc_diag.pop((pr, j + 1)) + pv[2 * t:]
            o_ref[0, (j + 1) * t:(j + 2) * t, ls] = _unstack_heads(done, lane).astype(BF16)
            total = carry_diag.pop((pr, j + 1)) + rs[2 * t:]
            if j >= 1:
                flag_ref[pr * nb + j + 1] = jnp.min(total)
        if j == 0:
            o_ref[0, 0:t, ls] = _unstack_heads(pv[:2 * t], lane).astype(BF16)
        else:
            acc_diag[(pr, j)] = pv[:2 * t]
            carry_diag[(pr, j)] = jnp.broadcast_to(rs[:2 * t], (2 * t, LANES))

    n_groups = len(order)
    for step in range(n_groups + 2):
        if step < n_groups:
            scores(step)
        if 1 <= step <= n_groups:
            suffix_sums(step - 1)
        if step >= 2:
            weights_pv(step - 2)

    def recompute_block(pr, jb):
        ls = slice(pr * LANES, (pr + 1) * LANES)
        rows = pl.ds(pl.multiple_of(jb * t, t), t)
        qb = _stack_heads(q_ref[0, rows, ls], lane)
        acc_ref[...] = jnp.zeros_like(acc_ref)
        carry_ref[...] = jnp.zeros_like(carry_ref)

        def tile(kb, diag):
            keys = pl.ds(pl.multiple_of(kb * t, t), t)
            zz = lax.dot_general(qb, k_ref[0, keys, ls], _NT, preferred_element_type=F32)
            s = _softplus(zz)
            if diag:
                s = jnp.where(tri, s, 0.0)
            cc = jnp.dot(s.astype(BF16), msum, preferred_element_type=F32)
            w = jnp.exp(zz - s - cc - jnp.concatenate([carry_ref[...]] * reps, axis=1))
            if diag:
                w = jnp.where(tri, w, 0.0)
            acc_ref[...] += jnp.dot(w.astype(BF16), v_ref[0, keys, ls], preferred_element_type=F32)
            carry_ref[...] += jnp.sum(s, axis=1, keepdims=True)

        tile(jb, True)

        def body(i, _):
            tile(jb - 1 - i, False)
            return 0

        lax.fori_loop(0, jb, body, 0)
        o_ref[0, rows, ls] = _unstack_heads(acc_ref[...], lane).astype(BF16)

    for pr in range(ATT_PAIRS):
        def redo(jb, _, pr=pr):
            pl.when(flag_ref[pr * nb + jb] < SB_EXIT)(lambda: recompute_block(pr, jb))
            return 0

        lax.fori_loop(2, nb, redo, 0)


def _sb_attention(q, k, v):
    b, s, d = q.shape
    t = ATT_K_TILE
    width = ATT_PAIRS * LANES
    ii = np.arange(t)
    msum = jnp.asarray(ii[:, None] > ii[None, :], BF16)
    spec = pl.BlockSpec((1, s, width), lambda bi, g: (bi, 0, g))
    return pl.pallas_call(
        _sb_kernel,
        grid=(b, d // width),
        in_specs=[spec, spec, spec, _const_spec((t, t))],
        out_specs=spec,
        out_shape=jax.ShapeDtypeStruct((b, s, d), BF16),
        scratch_shapes=[pltpu.SMEM((ATT_PAIRS * (s // t),), F32), pltpu.VMEM((2 * t, LANES), F32),
                        pltpu.VMEM((2 * t, LANES), F32)],
        compiler_params=_params(2),
        name="sb_attn",
    )(q, k, v, msum)


def _fox_kernel(q_ref, aq_ref, k_ref, ak_ref, v_ref, o_ref):
    s_len = q_ref.shape[1]
    t = ATT_Q_TILE
    hb = t // 2
    lane = lax.broadcasted_iota(jnp.int32, (hb, LANES), 1)
    group = lane // AUG_STRIDE
    tri = (lax.broadcasted_iota(jnp.int32, (t, hb), 1)
           <= (lax.broadcasted_iota(jnp.int32, (t, hb), 0) & (hb - 1)))
    ones = jnp.ones((t, LANES), BF16)
    tiles = []
    for pr in range(ATT_PAIRS):
        for qi in range(s_len // t):
            tiles += [(pr, qi, "full", kb * t, t) for kb in range(qi)]
            tiles += [(pr, qi, "diag_all", qi * t, hb), (pr, qi, "diag_b", qi * t + hb, hb)]
    qext, kvext, z, m, acc = {}, {}, {}, {}, {}

    def stacked_q(pr, qi):
        ls = slice(pr * LANES, (pr + 1) * LANES)
        pair = ATT_PAIRS * pl.program_id(1) + pr
        parts = []
        for half in range(2):
            rows = slice(qi * t + half * hb, qi * t + (half + 1) * hb)
            aq = aq_ref[0, rows, :]
            zero = jnp.zeros_like(aq)
            aug = jnp.concatenate([jnp.where(group == 2 * pair, aq, zero),
                                   jnp.where(group == 2 * pair + 1, aq, zero)], axis=0)
            parts.append(jnp.concatenate([_stack_heads(q_ref[0, rows, ls], lane), aug], axis=1))
        return jnp.concatenate(parts, axis=0)

    def keys_values(pr, start, size):
        if (pr, start, size) not in kvext:
            ls = slice(pr * LANES, (pr + 1) * LANES)
            keys = slice(start, start + size)
            kext = jnp.concatenate([k_ref[0, keys, ls], ak_ref[0, keys, :]], axis=1)
            vext = jnp.concatenate([v_ref[0, keys, ls], ones[:size]], axis=1)
            kvext[(pr, start, size)] = (kext, vext)
        return kvext[(pr, start, size)]

    def scores(n):
        pr, qi, kind, start, size = tiles[n]
        if (pr, qi) not in qext:
            qext[(pr, qi)] = stacked_q(pr, qi)
        lhs = qext[(pr, qi)]
        kext = keys_values(pr, start, size)[0]
        if kind == "diag_b":
            s = lax.dot_general(lhs[t:], kext, _NT, preferred_element_type=F32)
            s = jnp.where(tri, s, NEG_INF)
        else:
            s = lax.dot_general(lhs, kext, _NT, preferred_element_type=F32)
            if kind == "diag_all":
                s = jnp.concatenate([jnp.where(tri, s[:t], NEG_INF), s[t:]], axis=0)
        z[n] = s

    def softmax_pv(n):
        pr, qi, kind, start, size = tiles[n]
        key = (pr, qi)
        s = z.pop(n)
        vext = keys_values(pr, start, size)[1]
        row_max = jnp.max(s, axis=1, keepdims=True)
        if key not in m:
            m_new = jnp.broadcast_to(row_max, (s.shape[0], LANES))
        else:
            m_old = m[key][t:] if kind == "diag_b" else m[key]
            m_new = jnp.maximum(m_old, row_max)
        p = jnp.exp(s - jnp.concatenate([m_new] * (size // LANES), axis=1))
        pv = jnp.dot(p.astype(BF16), vext, preferred_element_type=F32)
        if key not in m:
            acc[key], m[key] = pv, m_new
        else:
            acc_old = acc[key][t:] if kind == "diag_b" else acc[key]
            alpha = jnp.exp(m_old - m_new)
            acc_new = jnp.concatenate([alpha, alpha], axis=1) * acc_old + pv
            if kind == "diag_b":
                acc[key] = jnp.concatenate([acc[key][:t], acc_new], axis=0)
                m[key] = jnp.concatenate([m[key][:t], m_new], axis=0)
            else:
                acc[key], m[key] = acc_new, m_new
        if kind == "diag_b":
            a = acc.pop(key)
            o = a[:, :LANES] / a[:, LANES:]
            out = jnp.concatenate([_unstack_heads(o[:t], lane), _unstack_heads(o[t:], lane)], axis=0)
            o_ref[0, qi * t:(qi + 1) * t, pr * LANES:(pr + 1) * LANES] = out.astype(BF16)
            m.pop(key)
            qext.pop(key)

    n_tiles = len(tiles)
    for step in range(n_tiles + 1):
        if step < n_tiles:
            scores(step)
        if step >= 1:
            softmax_pv(step - 1)


def _fox_attention(q, k, v, aq, ak):
    b, s, d = q.shape
    width = ATT_PAIRS * LANES
    spec = pl.BlockSpec((1, s, width), lambda bi, g: (bi, 0, g))
    aspec = pl.BlockSpec((1, s, LANES), lambda bi, g: (bi, 0, 0))
    return pl.pallas_call(
        _fox_kernel,
        grid=(b, d // width),
        in_specs=[spec, aspec, spec, aspec, spec],
        out_specs=spec,
        out_shape=jax.ShapeDtypeStruct((b, s, d), BF16),
        compiler_params=_params(2),
        name="fox_attn",
    )(q, aq, k, ak, v)


def _memory_attention(q_ref, mk_ref, mv_ref):
    z, out = {}, []

    def scores(h):
        sl = slice(h * MEM_HD, (h + 1) * MEM_HD)
        z[h] = lax.dot_general(q_ref[:, sl], mk_ref[:, sl], _NT,
                               preferred_element_type=F32) * MEM_HD ** -0.5

    def softmax_pv(h):
        sl = slice(h * MEM_HD, (h + 1) * MEM_HD)
        zh = z.pop(h)
        e = jnp.exp(zh - jnp.max(zh, axis=1, keepdims=True))
        p = e / jnp.sum(e, axis=1, keepdims=True)
        out.append(jnp.dot(p.astype(BF16), mv_ref[:, sl], preferred_element_type=F32).astype(BF16))

    for step in range(MEM_HEADS + 1):
        if step < MEM_HEADS:
            scores(step)
        if step >= 1:
            softmax_pv(step - 1)
    return jnp.concatenate(out, axis=1)


def _merge_kernel(steps_per_batch, x_ref, g_ref, wint_hbm, osb_ref, ofx_ref, mq_ref,
                  mem_ref, gmem_ref, wkv_hbm, gk_ref, wsb_hbm, wfx_hbm, wmem_hbm, wout_hbm, o_ref,
                  mk_ref, mv_ref, wkv_ref, wgt_ref, wsb_ref, wfx_ref, wmem_ref, wout_ref, stage, sem):
    gate0 = wint_hbm.shape[0] - N_BRANCH * D_MODEL
    weights = [(wkv_hbm, wkv_ref)]
    for i, (src, dst) in enumerate(((wsb_hbm, wsb_ref), (wfx_hbm, wfx_ref), (wmem_hbm, wmem_ref))):
        rows = pl.ds(i * D_MODEL, D_MODEL)
        weights += [(wint_hbm.at[pl.ds(gate0 + i * D_MODEL, D_MODEL), :], wgt_ref.at[rows, :]), (src, dst)]
    weights.append((wout_hbm, wout_ref))

    def copy(k, slot):
        src = weights[k][0]
        return pltpu.make_async_copy(src, stage.at[slot, pl.ds(0, src.shape[0]), :], sem.at[slot])

    def body(stream_weights):
        fetched = [0]

        def fetch():
            if not stream_weights:
                return
            k = fetched[0]
            fetched[0] += 1
            slot = k % 2
            if k + 1 < len(weights):
                copy(k + 1, 1 - slot).start()
            copy(k, slot).wait()
            src, dst = weights[k]
            dst[...] = stage[slot, :src.shape[0], :].astype(BF16)

        def memory_kv():
            mh = _rms(mem_ref[0], gmem_ref[...]).astype(BF16)
            kv = jnp.dot(mh, wkv_ref[...], preferred_element_type=F32)
            mk_ref[...] = _headnorm(kv[:, :D_MEM], gk_ref[...], MEM_HD).astype(BF16)
            mv_ref[...] = kv[:, D_MEM:].astype(BF16)

        if stream_weights:
            copy(0, 0).start()
            fetch()
            memory_kv()
        else:
            pl.when(pl.program_id(0) % steps_per_batch == 0)(memory_kv)

        x = x_ref[...]
        h = _rms(x, g_ref[...]).astype(BF16)
        branches = ((osb_ref, wsb_ref), (ofx_ref, wfx_ref), (None, wmem_ref))
        merged = None
        for i, (o_br, w_br) in enumerate(branches):
            fetch()
            gate = jax.nn.sigmoid(lax.dot_general(h, wgt_ref[i * D_MODEL:(i + 1) * D_MODEL, :], _NT,
                                                  preferred_element_type=F32))
            o = _memory_attention(mq_ref, mk_ref, mv_ref) if o_br is None else o_br[...]
            fetch()
            term = gate * jnp.dot(o, w_br[...], preferred_element_type=F32)
            merged = term if merged is None else merged + term
        fetch()
        o_ref[...] = x + jnp.dot(merged.astype(BF16), wout_ref[...], preferred_element_type=F32)

    pl.when(pl.program_id(0) == 0)(lambda: body(True))
    pl.when(pl.program_id(0) > 0)(lambda: body(False))


def _merge(x2, g, w_in_t, osb, ofx, mq, mem, gmem, wkv, gk, wsb, wfx, wmem, wout):
    t = x2.shape[0]
    b, m, _ = mem.shape
    steps_per_batch = t // b // ROW_TILE
    row = lambda n: pl.BlockSpec((ROW_TILE, n), lambda i: (i, 0))
    hbm = pl.BlockSpec(memory_space=pl.ANY)
    bf16_copy = lambda w: pltpu.VMEM(w.shape, BF16)
    return pl.pallas_call(
        functools.partial(_merge_kernel, steps_per_batch),
        grid=(t // ROW_TILE,),
        in_specs=[row(D_MODEL), _const_spec((1, D_MODEL)), hbm,
                  row(D_SB), row(D_FOX), row(D_MEM),
                  pl.BlockSpec((1, m, D_MODEL), lambda i: (i // steps_per_batch, 0, 0)),
                  _const_spec((1, D_MODEL)), hbm, _const_spec((1, LANES)),
                  hbm, hbm, hbm, hbm],
        out_specs=row(D_MODEL),
        out_shape=jax.ShapeDtypeStruct((t, D_MODEL), F32),
        scratch_shapes=[pltpu.VMEM((m, D_MEM), BF16), pltpu.VMEM((m, D_MEM), BF16),
                        bf16_copy(wkv), pltpu.VMEM((N_BRANCH * D_MODEL, D_MODEL), BF16),
                        bf16_copy(wsb), bf16_copy(wfx), bf16_copy(wmem), bf16_copy(wout),
                        pltpu.VMEM((2, D_MODEL, D_MODEL), F32), pltpu.SemaphoreType.DMA((2,))],
        compiler_params=_params(1),
        name="merge",
    )(x2, g, w_in_t, osb, ofx, mq, mem, gmem, wkv, gk, wsb, wfx, wmem, wout)


def _mlp_kernel(x_ref, g_ref, wup_hbm, wdn_hbm, o_ref, wup_ref, wdn_ref, up_stage, dn_stage, sem):
    n_chunks = D_FF // FF_CHUNK

    def chunk_copies(c, slot):
        sl = pl.ds(c * FF_CHUNK, FF_CHUNK)
        return (pltpu.make_async_copy(wup_hbm.at[:, sl], up_stage.at[slot], sem.at[0, slot]),
                pltpu.make_async_copy(wdn_hbm.at[sl, :], dn_stage.at[slot], sem.at[1, slot]))

    @pl.when(pl.program_id(0) == 0)
    def _():
        for cp in chunk_copies(0, 0):
            cp.start()
        for c in range(n_chunks):
            sl = slice(c * FF_CHUNK, (c + 1) * FF_CHUNK)
            slot = c % 2
            if c + 1 < n_chunks:
                for cp in chunk_copies(c + 1, 1 - slot):
                    cp.start()
            for cp in chunk_copies(c, slot):
                cp.wait()
            wup_ref[:, sl] = up_stage[slot].astype(BF16)
            wdn_ref[sl, :] = dn_stage[slot].astype(BF16)

    x = x_ref[...]
    h = _rms(x, g_ref[...]).astype(BF16)
    acc = x
    for c in range(n_chunks):
        sl = slice(c * FF_CHUNK, (c + 1) * FF_CHUNK)
        u = jnp.maximum(jnp.dot(h, wup_ref[:, sl], preferred_element_type=F32), 0.0)
        acc = acc + jnp.dot((u * u).astype(BF16), wdn_ref[sl, :], preferred_element_type=F32)
    o_ref[...] = acc


def _mlp(x2, g, wup, wdn):
    t = x2.shape[0]
    row = pl.BlockSpec((MLP_ROW_TILE, D_MODEL), lambda i: (i, 0))
    hbm = pl.BlockSpec(memory_space=pl.ANY)
    return pl.pallas_call(
        _mlp_kernel,
        grid=(t // MLP_ROW_TILE,),
        in_specs=[row, _const_spec((1, D_MODEL)), hbm, hbm],
        out_specs=row,
        out_shape=jax.ShapeDtypeStruct((t, D_MODEL), F32),
        scratch_shapes=[pltpu.VMEM((D_MODEL, D_FF), BF16), pltpu.VMEM((D_FF, D_MODEL), BF16),
                        pltpu.VMEM((2, D_MODEL, FF_CHUNK), F32), pltpu.VMEM((2, FF_CHUNK, D_MODEL), F32),
                        pltpu.SemaphoreType.DMA((2, 2))],
        compiler_params=_params(1),
        name="mlp",
    )(x2, g, wup, wdn)


def _layer(x, mem, g_mix, g_memn, w_in, b_forget, g_fq, g_fk, g_mq, g_mk, w_mem_kv,
           w_sb, w_fox, w_mem, w_out, g_mlp, w_up, w_dn):
    b, s, _ = x.shape
    t = b * s
    x2 = x.reshape(t, D_MODEL)
    w_in_t = w_in.T
    bf = jnp.pad(b_forget, (0, LANES - FOX_HEADS)).reshape(1, LANES)
    row = lambda a: a.reshape(1, -1)

    sbq, sbk, sbv, fxq, fxk, fxv, mq, aq, ak = _proj(
        x2, b, row(g_mix), w_in_t, bf, row(jnp.tile(g_fq, LANES // HD)),
        row(jnp.tile(g_fk, LANES // HD)), row(g_mq))
    to3 = lambda a: a.reshape(b, s, a.shape[-1])

    o_sb = _sb_attention(to3(sbq), to3(sbk), to3(sbv))
    o_fox = _fox_attention(to3(fxq), to3(fxk), to3(fxv), to3(aq), to3(ak))
    x1 = _merge(x2, row(g_mix), w_in_t, o_sb.reshape(t, D_SB), o_fox.reshape(t, D_FOX), mq,
                mem, row(g_memn), w_mem_kv, row(g_mk), w_sb, w_fox, w_mem, w_out)
    out = _mlp(x1, row(g_mlp), w_up, w_dn)
    return out.reshape(b, s, D_MODEL)


def kernel(x, mem, g_mix_norm, g_mem_norm, w_in, b_forget, g_fox_q, g_fox_k, g_mem_q, g_mem_k,
           w_mem_kv, w_branch_sb, w_branch_fox, w_branch_mem, w_out, g_mlp_norm, w_ff_up, w_ff_down):
    for l in range(w_in.shape[0]):
        x = _layer(x, mem, g_mix_norm[l], g_mem_norm[l], w_in[l], b_forget[l], g_fox_q[l], g_fox_k[l],
                   g_mem_q[l], g_mem_k[l], w_mem_kv[l], w_branch_sb[l], w_branch_fox[l],
                   w_branch_mem[l], w_out[l], g_mlp_norm[l], w_ff_up[l], w_ff_down[l])
    return x
```

```python
import functools

import numpy as np
import jax
import jax.numpy as jnp
from jax import lax
from jax.experimental import pallas as pl
from jax.experimental.pallas import tpu as pltpu

D_MODEL = 1024
HD = 64
SB_HEADS = 8
FOX_HEADS = 8
MEM_HEADS = 4
MEM_HD = 128
D_SB = SB_HEADS * HD
D_FOX = FOX_HEADS * HD
D_MEM = MEM_HEADS * MEM_HD
N_BRANCH = 3
D_FF = 4 * D_MODEL
EPS = 1e-6
NEG_INF = -1e30

LANES = 128
AUG_STRIDE = 8
ROW_TILE = 512
PROJ_ROW_TILE = 1024
MLP_ROW_TILE = 512
ATT_Q_TILE = 512
ATT_K_TILE = 256
ATT_PAIRS = 2
SB_EXIT = 110.0
FF_CHUNK = 512
VMEM_LIMIT = 56 * 1024 * 1024

F32 = jnp.float32
BF16 = jnp.bfloat16
_NT = (((1,), (1,)), ((), ()))
_LOG2E = 1.4426950408889634


def _const_spec(shape):
    nd = len(shape)
    return pl.BlockSpec(shape, lambda *_: (0,) * nd, pipeline_mode=pl.Buffered(1))


def _params(n_axes):
    return pltpu.CompilerParams(
        dimension_semantics=("arbitrary",) * n_axes, vmem_limit_bytes=VMEM_LIMIT)


def _rms(x, g):
    ms = jnp.sum(x * x, axis=-1, keepdims=True) * (1.0 / x.shape[-1])
    return (x * lax.rsqrt(ms + EPS)) * g


def _headnorm(p, g, hd):
    rows, n = p.shape
    lane = lax.broadcasted_iota(jnp.int32, (rows, LANES), 1)
    outs = []
    for j in range(n // LANES):
        c = p[:, j * LANES:(j + 1) * LANES]
        c2 = c * c
        if hd == LANES:
            ms = jnp.sum(c2, axis=1, keepdims=True) * (1.0 / LANES)
        else:
            lo = jnp.sum(jnp.where(lane < hd, c2, 0.0), axis=1, keepdims=True)
            hi = jnp.sum(jnp.where(lane >= hd, c2, 0.0), axis=1, keepdims=True)
            ms = jnp.where(lane < hd, lo, hi) * (1.0 / hd)
        outs.append((c * lax.rsqrt(ms + EPS)) * g)
    return jnp.concatenate(outs, axis=1)


def _log_sigmoid(x):
    return jnp.minimum(x, 0.0) - jnp.log(1.0 + jnp.exp(-jnp.abs(x)))


def _decay_columns(lf, carry_ref, p_ref, oq_ref, ok_ref):
    rows = lf.shape[0]
    row = lax.broadcasted_iota(jnp.int32, lf.shape, 0)
    lane = lax.broadcasted_iota(jnp.int32, lf.shape, 1)
    f = lf
    sh = 1
    while sh < rows:
        f = f + jnp.where(row >= sh, pltpu.roll(f, sh, 0), 0.0)
        sh *= 2
    f = f + carry_ref[...]
    carry_ref[...] = f[rows - 1:rows, :]
    f_hi = f.astype(BF16).astype(F32)
    r1 = f - f_hi
    f_mid = r1.astype(BF16).astype(F32)
    f_lo = (r1 - f_mid).astype(BF16).astype(F32)
    keep = lane < FOX_HEADS
    zero = jnp.zeros_like(f)
    pieces = (jnp.where(keep, f_hi, zero) + pltpu.roll(jnp.where(keep, f_mid, zero), FOX_HEADS, 1)
              + pltpu.roll(jnp.where(keep, f_lo, zero), 2 * FOX_HEADS, 1))
    placed = jnp.dot(pieces.astype(BF16), p_ref[...], preferred_element_type=F32)
    return ((placed[:, :LANES] + oq_ref[...]).astype(BF16), (placed[:, LANES:] + ok_ref[...]).astype(BF16))


def _decay_constants():
    p = np.zeros((LANES, 2 * LANES), np.float32)
    oq = np.zeros((1, LANES), np.float32)
    ok = np.zeros((1, LANES), np.float32)
    for h in range(FOX_HEADS):
        for j in range(3):
            p[j * FOX_HEADS + h, h * AUG_STRIDE + j] = 1.0
            p[j * FOX_HEADS + h, LANES + h * AUG_STRIDE + 3 + j] = -1.0
            oq[0, h * AUG_STRIDE + 3 + j] = 1.0
            ok[0, h * AUG_STRIDE + j] = 1.0
    return jnp.asarray(p, BF16), jnp.asarray(oq), jnp.asarray(ok)


def _proj_kernel(steps_per_batch, x_ref, g_ref, wt_ref, bf_ref, gq_ref, gk_ref, gm_ref, p_ref, oq_ref, ok_ref,
                 sb_ref, fx_ref, mq_ref, aug_ref,
                 carry_ref, wbf_ref):
    @pl.when(pl.program_id(0) == 0)
    def _():
        wbf_ref[...] = wt_ref[...].astype(BF16)

    @pl.when(pl.program_id(0) % steps_per_batch == 0)
    def _():
        carry_ref[...] = jnp.zeros_like(carry_ref)

    h = _rms(x_ref[...], g_ref[...]).astype(BF16)

    def mm(c0, n):
        return lax.dot_general(h, wbf_ref[c0:c0 + n, :], _NT, preferred_element_type=F32)

    c0 = 3 * D_SB
    c_f = c0 + 3 * D_FOX
    log_f = _log_sigmoid(mm(c_f, LANES) + bf_ref[...])
    mq_ref[...] = _headnorm(mm(c_f + FOX_HEADS, D_MEM), gm_ref[...], MEM_HD).astype(BF16)
    fx_ref[:, :D_FOX] = (_headnorm(mm(c0, D_FOX), gq_ref[...], HD) * HD ** -0.5).astype(BF16)
    fx_ref[:, D_FOX:2 * D_FOX] = _headnorm(mm(c0 + D_FOX, D_FOX), gk_ref[...], HD).astype(BF16)
    aug_ref[:, :LANES], aug_ref[:, LANES:] = _decay_columns(log_f, carry_ref, p_ref, oq_ref, ok_ref)
    fx_ref[:, 2 * D_FOX:] = mm(c0 + 2 * D_FOX, D_FOX).astype(BF16)
    sb_ref[:, :D_SB] = (mm(0, D_SB) * HD ** -0.5).astype(BF16)
    sb_ref[:, D_SB:2 * D_SB] = mm(D_SB, D_SB).astype(BF16)
    sb_ref[:, 2 * D_SB:] = mm(2 * D_SB, D_SB).astype(BF16)


def _proj(x2, batch, g, w_in_t, bf, gq, gk, gm):
    t = x2.shape[0]
    n_rows = 3 * D_SB + 3 * D_FOX + FOX_HEADS + D_MEM
    p, oq, ok = _decay_constants()
    row = lambda n: pl.BlockSpec((PROJ_ROW_TILE, n), lambda i: (i, 0))
    widths = (3 * D_SB, 3 * D_FOX, D_MEM, 2 * LANES)
    out_shapes = [jax.ShapeDtypeStruct((t, n), BF16) for n in widths]
    return pl.pallas_call(
        functools.partial(_proj_kernel, t // batch // PROJ_ROW_TILE),
        grid=(t // PROJ_ROW_TILE,),
        in_specs=[row(D_MODEL), _const_spec((1, D_MODEL)), _const_spec((n_rows, D_MODEL)),
                  _const_spec((1, LANES)), _const_spec((1, LANES)), _const_spec((1, LANES)),
                  _const_spec((1, LANES)), _const_spec(p.shape), _const_spec(oq.shape), _const_spec(ok.shape)],
        out_specs=[row(n) for n in widths],
        out_shape=out_shapes,
        scratch_shapes=[pltpu.VMEM((1, LANES), F32), pltpu.VMEM((n_rows, D_MODEL), BF16)],
        compiler_params=_params(1),
        name="proj",
    )(x2, g, w_in_t, bf, gq, gk, gm, p, oq, ok)


def _stack_heads(x, lane):
    zero = jnp.zeros_like(x)
    return jnp.concatenate([jnp.where(lane < HD, x, zero), jnp.where(lane >= HD, x, zero)], axis=0)


def _unstack_heads(x, lane):
    rows = x.shape[0] // 2
    return jnp.where(lane < HD, x[:rows], x[rows:])


def _softplus(z):
    return jnp.maximum(z, 0.0) + jnp.log(1.0 + jnp.exp2(jnp.abs(z) * (-_LOG2E)))


def _sb_kernel(q_ref, k_ref, v_ref, m_ref, o_ref, flag_ref, acc_ref, carry_ref):
    s_len = q_ref.shape[1]
    t = ATT_K_TILE
    nb = s_len // t
    lane = lax.broadcasted_iota(jnp.int32, (t, LANES), 1)
    tri = (lax.broadcasted_iota(jnp.int32, (2 * t, t), 1)
           < (lax.broadcasted_iota(jnp.int32, (2 * t, t), 0) & (t - 1)))
    msum = m_ref[...]
    reps = t // LANES
    order = [(pr, j) for pr in range(ATT_PAIRS) for j in reversed(range(nb))]
    z, sp, c, acc_diag, carry_diag = {}, {}, {}, {}, {}

    def qblock(pr, j):
        return _stack_heads(q_ref[0, j * t:(j + 1) * t, pr * LANES:(pr + 1) * LANES], lane)

    def scores(g):
        pr, j = order[g]
        lhs = qblock(pr, j) if j == nb - 1 else jnp.concatenate([qblock(pr, j), qblock(pr, j + 1)], axis=0)
        z[g] = lax.dot_general(lhs, k_ref[0, j * t:(j + 1) * t, pr * LANES:(pr + 1) * LANES], _NT,
                               preferred_element_type=F32)

    def suffix_sums(g):
        s = _softplus(z[g])
        top = jnp.where(tri, s[:2 * t], 0.0)
        s = top if order[g][1] == nb - 1 else jnp.concatenate([top, s[2 * t:]], axis=0)
        sp[g] = s
        z[g] = z[g] - s
        c[g] = jnp.dot(s.astype(BF16), msum, preferred_element_type=F32)

    def weights_pv(g):
        pr, j = order[g]
        ls = slice(pr * LANES, (pr + 1) * LANES)
        zc = z.pop(g) - c.pop(g)
        w = jnp.where(tri, jnp.exp(zc[:2 * t]), 0.0)
        if j < nb - 1:
            w_prev = jnp.exp(zc[2 * t:] - jnp.concatenate([carry_diag[(pr, j + 1)]] * reps, axis=1))
            w = jnp.concatenate([w, w_prev], axis=0)
        pv = jnp.dot(w.astype(BF16), v_ref[0, j * t:(j + 1) * t, ls], preferred_element_type=F32)
        rs = jnp.sum(sp.pop(g), axis=1, keepdims=True)
        if j < nb - 1:
            done = acc_diag.pop((pr, j + 1)) + pv[2 * t:]
            o_ref[0, (j + 1) * t:(j + 2) * t, ls] = _unstack_heads(done, lane).astype(BF16)
            total = carry_diag.pop((pr, j + 1)) + rs[2 * t:]
            if j >= 1:
                flag_ref[pr * nb + j + 1] = jnp.min(total)
        if j == 0:
            o_ref[0, 0:t, ls] = _unstack_heads(pv[:2 * t], lane).astype(BF16)
        else:
            acc_diag[(pr, j)] = pv[:2 * t]
            carry_diag[(pr, j)] = jnp.broadcast_to(rs[:2 * t], (2 * t, LANES))

    n_groups = len(order)
    for step in range(n_groups + 2):
        if step < n_groups:
            scores(step)
        if 1 <= step <= n_groups:
            suffix_sums(step - 1)
        if step >= 2:
            weights_pv(step - 2)

    def recompute_block(pr, jb):
        ls = slice(pr * LANES, (pr + 1) * LANES)
        rows = pl.ds(pl.multiple_of(jb * t, t), t)
        qb = _stack_heads(q_ref[0, rows, ls], lane)
        acc_ref[...] = jnp.zeros_like(acc_ref)
        carry_ref[...] = jnp.zeros_like(carry_ref)

        def tile(kb, diag):
            keys = pl.ds(pl.multiple_of(kb * t, t), t)
            zz = lax.dot_general(qb, k_ref[0, keys, ls], _NT, preferred_element_type=F32)
            s = _softplus(zz)
            if diag:
                s = jnp.where(tri, s, 0.0)
            cc = jnp.dot(s.astype(BF16), msum, preferred_element_type=F32)
            w = jnp.exp(zz - s - cc - jnp.concatenate([carry_ref[...]] * reps, axis=1))
            if diag:
                w = jnp.where(tri, w, 0.0)
            acc_ref[...] += jnp.dot(w.astype(BF16), v_ref[0, keys, ls], preferred_element_type=F32)
            carry_ref[...] += jnp.sum(s, axis=1, keepdims=True)

        tile(jb, True)

        def body(i, _):
            tile(jb - 1 - i, False)
            return 0

        lax.fori_loop(0, jb, body, 0)
        o_ref[0, rows, ls] = _unstack_heads(acc_ref[...], lane).astype(BF16)

    for pr in range(ATT_PAIRS):
        def redo(jb, _, pr=pr):
            pl.when(flag_ref[pr * nb + jb] < SB_EXIT)(lambda: recompute_block(pr, jb))
            return 0

        lax.fori_loop(2, nb, redo, 0)


def _qkv_specs(s, d):
    width = ATT_PAIRS * LANES
    n = d // width
    part = lambda j: pl.BlockSpec((1, s, width), lambda bi, g: (bi, 0, j * n + g))
    return part(0), part(1), part(2), n


def _sb_attention(qkv):
    b, s, d3 = qkv.shape
    d = d3 // 3
    t = ATT_K_TILE
    ii = np.arange(t)
    msum = jnp.asarray(ii[:, None] > ii[None, :], BF16)
    qspec, kspec, vspec, n = _qkv_specs(s, d)
    return pl.pallas_call(
        _sb_kernel,
        grid=(b, n),
        in_specs=[qspec, kspec, vspec, _const_spec((t, t))],
        out_specs=qspec,
        out_shape=jax.ShapeDtypeStruct((b, s, d), BF16),
        scratch_shapes=[pltpu.SMEM((ATT_PAIRS * (s // t),), F32), pltpu.VMEM((2 * t, LANES), F32),
                        pltpu.VMEM((2 * t, LANES), F32)],
        compiler_params=_params(2),
        name="sb_attn",
    )(qkv, qkv, qkv, msum)


def _fox_kernel(q_ref, aq_ref, k_ref, ak_ref, v_ref, o_ref):
    s_len = q_ref.shape[1]
    t = ATT_Q_TILE
    hb = t // 2
    lane = lax.broadcasted_iota(jnp.int32, (hb, LANES), 1)
    group = lane // AUG_STRIDE
    tri = (lax.broadcasted_iota(jnp.int32, (t, hb), 1)
           <= (lax.broadcasted_iota(jnp.int32, (t, hb), 0) & (hb - 1)))
    ones = jnp.ones((t, LANES), BF16)
    tiles = []
    for pr in range(ATT_PAIRS):
        for qi in range(s_len // t):
            tiles += [(pr, qi, "full", kb * t, t) for kb in range(qi)]
            tiles += [(pr, qi, "diag_all", qi * t, hb), (pr, qi, "diag_b", qi * t + hb, hb)]
    qext, kvext, z, m, acc = {}, {}, {}, {}, {}

    def stacked_q(pr, qi):
        ls = slice(pr * LANES, (pr + 1) * LANES)
        pair = ATT_PAIRS * pl.program_id(1) + pr
        parts = []
        for half in range(2):
            rows = slice(qi * t + half * hb, qi * t + (half + 1) * hb)
            aq = aq_ref[0, rows, :]
            zero = jnp.zeros_like(aq)
            aug = jnp.concatenate([jnp.where(group == 2 * pair, aq, zero),
                                   jnp.where(group == 2 * pair + 1, aq, zero)], axis=0)
            parts.append(jnp.concatenate([_stack_heads(q_ref[0, rows, ls], lane), aug], axis=1))
        return jnp.concatenate(parts, axis=0)

    def keys_values(pr, start, size):
        if (pr, start, size) not in kvext:
            ls = slice(pr * LANES, (pr + 1) * LANES)
            keys = slice(start, start + size)
            kext = jnp.concatenate([k_ref[0, keys, ls], ak_ref[0, keys, :]], axis=1)
            vext = jnp.concatenate([v_ref[0, keys, ls], ones[:size]], axis=1)
            kvext[(pr, start, size)] = (kext, vext)
        return kvext[(pr, start, size)]

    def scores(n):
        pr, qi, kind, start, size = tiles[n]
        if (pr, qi) not in qext:
            qext[(pr, qi)] = stacked_q(pr, qi)
        lhs = qext[(pr, qi)]
        kext = keys_values(pr, start, size)[0]
        if kind == "diag_b":
            s = lax.dot_general(lhs[t:], kext, _NT, preferred_element_type=F32)
            s = jnp.where(tri, s, NEG_INF)
        else:
            s = lax.dot_general(lhs, kext, _NT, preferred_element_type=F32)
            if kind == "diag_all":
                s = jnp.concatenate([jnp.where(tri, s[:t], NEG_INF), s[t:]], axis=0)
        z[n] = s

    def softmax_pv(n):
        pr, qi, kind, start, size = tiles[n]
        key = (pr, qi)
        s = z.pop(n)
        vext = keys_values(pr, start, size)[1]
        row_max = jnp.max(s, axis=1, keepdims=True)
        if key not in m:
            m_new = jnp.broadcast_to(row_max, (s.shape[0], LANES))
        else:
            m_old = m[key][t:] if kind == "diag_b" else m[key]
            m_new = jnp.maximum(m_old, row_max)
        p = jnp.exp(s - jnp.concatenate([m_new] * (size // LANES), axis=1))
        pv = jnp.dot(p.astype(BF16), vext, preferred_element_type=F32)
        if key not in m:
            acc[key], m[key] = pv, m_new
        else:
            acc_old = acc[key][t:] if kind == "diag_b" else acc[key]
            alpha = jnp.exp(m_old - m_new)
            acc_new = jnp.concatenate([alpha, alpha], axis=1) * acc_old + pv
            if kind == "diag_b":
                acc[key] = jnp.concatenate([acc[key][:t], acc_new], axis=0)
                m[key] = jnp.concatenate([m[key][:t], m_new], axis=0)
            else:
                acc[key], m[key] = acc_new, m_new
        if kind == "diag_b":
            a = acc.pop(key)
            o = a[:, :LANES] / a[:, LANES:]
            out = jnp.concatenate([_unstack_heads(o[:t], lane), _unstack_heads(o[t:], lane)], axis=0)
            o_ref[0, qi * t:(qi + 1) * t, pr * LANES:(pr + 1) * LANES] = out.astype(BF16)
            m.pop(key)
            qext.pop(key)

    n_tiles = len(tiles)
    for step in range(n_tiles + 1):
        if step < n_tiles:
            scores(step)
        if step >= 1:
            softmax_pv(step - 1)


def _fox_attention(qkv, aug):
    b, s, d3 = qkv.shape
    d = d3 // 3
    qspec, kspec, vspec, n = _qkv_specs(s, d)
    aspec = lambda j: pl.BlockSpec((1, s, LANES), lambda bi, g: (bi, 0, j))
    return pl.pallas_call(
        _fox_kernel,
        grid=(b, n),
        in_specs=[qspec, aspec(0), kspec, aspec(1), vspec],
        out_specs=qspec,
        out_shape=jax.ShapeDtypeStruct((b, s, d), BF16),
        compiler_params=_params(2),
        name="fox_attn",
    )(qkv, aug, qkv, aug, qkv)


def _memory_attention(q_ref, mk_ref, mv_ref):
    z, out = {}, []

    def scores(h):
        sl = slice(h * MEM_HD, (h + 1) * MEM_HD)
        z[h] = lax.dot_general(q_ref[:, sl], mk_ref[:, sl], _NT,
                               preferred_element_type=F32) * MEM_HD ** -0.5

    def softmax_pv(h):
        sl = slice(h * MEM_HD, (h + 1) * MEM_HD)
        zh = z.pop(h)
        e = jnp.exp(zh - jnp.max(zh, axis=1, keepdims=True))
        p = e / jnp.sum(e, axis=1, keepdims=True)
        out.append(jnp.dot(p.astype(BF16), mv_ref[:, sl], preferred_element_type=F32).astype(BF16))

    for step in range(MEM_HEADS + 1):
        if step < MEM_HEADS:
            scores(step)
        if step >= 1:
            softmax_pv(step - 1)
    return jnp.concatenate(out, axis=1)


def _merge_kernel(steps_per_batch, x_ref, g_ref, wint_hbm, osb_ref, ofx_ref, mq_ref,
                  mem_ref, gmem_ref, wkv_hbm, gk_ref, wsb_hbm, wfx_hbm, wmem_hbm, wout_hbm, o_ref,
                  mk_ref, mv_ref, wkv_ref, wgt_ref, wsb_ref, wfx_ref, wmem_ref, wout_ref, stage, sem):
    gate0 = wint_hbm.shape[0] - N_BRANCH * D_MODEL
    weights = [(wkv_hbm, wkv_ref)]
    for i, (src, dst) in enumerate(((wsb_hbm, wsb_ref), (wfx_hbm, wfx_ref), (wmem_hbm, wmem_ref))):
        rows = pl.ds(i * D_MODEL, D_MODEL)
        weights += [(wint_hbm.at[pl.ds(gate0 + i * D_MODEL, D_MODEL), :], wgt_ref.at[rows, :]), (src, dst)]
    weights.append((wout_hbm, wout_ref))

    def copy(k, slot):
        src = weights[k][0]
        return pltpu.make_async_copy(src, stage.at[slot, pl.ds(0, src.shape[0]), :], sem.at[slot])

    def body(stream_weights):
        fetched = [0]

        def fetch():
            if not stream_weights:
                return
            k = fetched[0]
            fetched[0] += 1
            slot = k % 2
            if k + 1 < len(weights):
                copy(k + 1, 1 - slot).start()
            copy(k, slot).wait()
            src, dst = weights[k]
            dst[...] = stage[slot, :src.shape[0], :].astype(BF16)

        def memory_kv():
            mh = _rms(mem_ref[0], gmem_ref[...]).astype(BF16)
            kv = jnp.dot(mh, wkv_ref[...], preferred_element_type=F32)
            mk_ref[...] = _headnorm(kv[:, :D_MEM], gk_ref[...], MEM_HD).astype(BF16)
            mv_ref[...] = kv[:, D_MEM:].astype(BF16)

        if stream_weights:
            copy(0, 0).start()
            fetch()
            memory_kv()
        else:
            pl.when(pl.program_id(0) % steps_per_batch == 0)(memory_kv)

        x = x_ref[...]
        h = _rms(x, g_ref[...]).astype(BF16)
        branches = ((osb_ref, wsb_ref), (ofx_ref, wfx_ref), (None, wmem_ref))
        merged = None
        for i, (o_br, w_br) in enumerate(branches):
            fetch()
            gate = jax.nn.sigmoid(lax.dot_general(h, wgt_ref[i * D_MODEL:(i + 1) * D_MODEL, :], _NT,
                                                  preferred_element_type=F32))
            o = _memory_attention(mq_ref, mk_ref, mv_ref) if o_br is None else o_br[...]
            fetch()
            term = gate * jnp.dot(o, w_br[...], preferred_element_type=F32)
            merged = term if merged is None else merged + term
        fetch()
        o_ref[...] = x + jnp.dot(merged.astype(BF16), wout_ref[...], preferred_element_type=F32)

    pl.when(pl.program_id(0) == 0)(lambda: body(True))
    pl.when(pl.program_id(0) > 0)(lambda: body(False))


def _merge(x2, g, w_in_t, osb, ofx, mq, mem, gmem, wkv, gk, wsb, wfx, wmem, wout):
    t = x2.shape[0]
    b, m, _ = mem.shape
    steps_per_batch = t // b // ROW_TILE
    row = lambda n: pl.BlockSpec((ROW_TILE, n), lambda i: (i, 0))
    hbm = pl.BlockSpec(memory_space=pl.ANY)
    bf16_copy = lambda w: pltpu.VMEM(w.shape, BF16)
    return pl.pallas_call(
        functools.partial(_merge_kernel, steps_per_batch),
        grid=(t // ROW_TILE,),
        in_specs=[row(D_MODEL), _const_spec((1, D_MODEL)), hbm,
                  row(D_SB), row(D_FOX), row(D_MEM),
                  pl.BlockSpec((1, m, D_MODEL), lambda i: (i // steps_per_batch, 0, 0)),
                  _const_spec((1, D_MODEL)), hbm, _const_spec((1, LANES)),
                  hbm, hbm, hbm, hbm],
        out_specs=row(D_MODEL),
        out_shape=jax.ShapeDtypeStruct((t, D_MODEL), F32),
        scratch_shapes=[pltpu.VMEM((m, D_MEM), BF16), pltpu.VMEM((m, D_MEM), BF16),
                        bf16_copy(wkv), pltpu.VMEM((N_BRANCH * D_MODEL, D_MODEL), BF16),
                        bf16_copy(wsb), bf16_copy(wfx), bf16_copy(wmem), bf16_copy(wout),
                        pltpu.VMEM((2, D_MODEL, D_MODEL), F32), pltpu.SemaphoreType.DMA((2,))],
        compiler_params=_params(1),
        name="merge",
    )(x2, g, w_in_t, osb, ofx, mq, mem, gmem, wkv, gk, wsb, wfx, wmem, wout)


def _mlp_kernel(x_ref, g_ref, wup_hbm, wdn_hbm, o_ref, wup_ref, wdn_ref, up_stage, dn_stage, sem):
    n_chunks = D_FF // FF_CHUNK

    def chunk_copies(c, slot):
        sl = pl.ds(c * FF_CHUNK, FF_CHUNK)
        return (pltpu.make_async_copy(wup_hbm.at[:, sl], up_stage.at[slot], sem.at[0, slot]),
                pltpu.make_async_copy(wdn_hbm.at[sl, :], dn_stage.at[slot], sem.at[1, slot]))

    def body(stream_weights):
        if stream_weights:
            for cp in chunk_copies(0, 0):
                cp.start()
        x = x_ref[...]
        h = _rms(x, g_ref[...]).astype(BF16)
        acc = x
        for c in range(n_chunks):
            sl = slice(c * FF_CHUNK, (c + 1) * FF_CHUNK)
            if stream_weights:
                slot = c % 2
                if c + 1 < n_chunks:
                    for cp in chunk_copies(c + 1, 1 - slot):
                        cp.start()
                for cp in chunk_copies(c, slot):
                    cp.wait()
                wup_ref[:, sl] = up_stage[slot].astype(BF16)
                wdn_ref[sl, :] = dn_stage[slot].astype(BF16)
            u = jnp.maximum(jnp.dot(h, wup_ref[:, sl], preferred_element_type=F32), 0.0)
            acc = acc + jnp.dot((u * u).astype(BF16), wdn_ref[sl, :], preferred_element_type=F32)
        o_ref[...] = acc

    pl.when(pl.program_id(0) == 0)(lambda: body(True))
    pl.when(pl.program_id(0) > 0)(lambda: body(False))


def _mlp(x2, g, wup, wdn):
    t = x2.shape[0]
    row = pl.BlockSpec((MLP_ROW_TILE, D_MODEL), lambda i: (i, 0))
    hbm = pl.BlockSpec(memory_space=pl.ANY)
    return pl.pallas_call(
        _mlp_kernel,
        grid=(t // MLP_ROW_TILE,),
        in_specs=[row, _const_spec((1, D_MODEL)), hbm, hbm],
        out_specs=row,
        out_shape=jax.ShapeDtypeStruct((t, D_MODEL), F32),
        scratch_shapes=[pltpu.VMEM((D_MODEL, D_FF), BF16), pltpu.VMEM((D_FF, D_MODEL), BF16),
                        pltpu.VMEM((2, D_MODEL, FF_CHUNK), F32), pltpu.VMEM((2, FF_CHUNK, D_MODEL), F32),
                        pltpu.SemaphoreType.DMA((2, 2))],
        compiler_params=_params(1),
        name="mlp",
    )(x2, g, wup, wdn)


def _layer(x, mem, g_mix, g_memn, w_in, b_forget, g_fq, g_fk, g_mq, g_mk, w_mem_kv,
           w_sb, w_fox, w_mem, w_out, g_mlp, w_up, w_dn):
    b, s, _ = x.shape
    t = b * s
    x2 = x.reshape(t, D_MODEL)
    w_in_t = w_in.T
    bf = jnp.pad(b_forget, (0, LANES - FOX_HEADS)).reshape(1, LANES)
    row = lambda a: a.reshape(1, -1)

    sb_qkv, fox_qkv, mq, aug = _proj(
        x2, b, row(g_mix), w_in_t, bf, row(jnp.tile(g_fq, LANES // HD)),
        row(jnp.tile(g_fk, LANES // HD)), row(g_mq))
    to3 = lambda a: a.reshape(b, s, a.shape[-1])

    o_sb = _sb_attention(to3(sb_qkv))
    o_fox = _fox_attention(to3(fox_qkv), to3(aug))
    x1 = _merge(x2, row(g_mix), w_in_t, o_sb.reshape(t, D_SB), o_fox.reshape(t, D_FOX), mq,
                mem, row(g_memn), w_mem_kv, row(g_mk), w_sb, w_fox, w_mem, w_out)
    out = _mlp(x1, row(g_mlp), w_up, w_dn)
    return out.reshape(b, s, D_MODEL)


def kernel(x, mem, g_mix_norm, g_mem_norm, w_in, b_forget, g_fox_q, g_fox_k, g_mem_q, g_mem_k,
           w_mem_kv, w_branch_sb, w_branch_fox, w_branch_mem, w_out, g_mlp_norm, w_ff_up, w_ff_down):
    for l in range(w_in.shape[0]):
        x = _layer(x, mem, g_mix_norm[l], g_mem_norm[l], w_in[l], b_forget[l], g_fox_q[l], g_fox_k[l],
                   g_mem_q[l], g_mem_k[l], w_mem_kv[l], w_branch_sb[l], w_branch_fox[l],
                   w_branch_mem[l], w_out[l], g_mlp_norm[l], w_ff_up[l], w_ff_down[l])
    return x
```
